```python
import jax, jax.numpy as jnp
from jax import lax
import numpy as np

D_MODEL = 1024
BATCH = 8
SEQ = 2048
DEPTH = 4
DEC_BATCH = 128
DEC_SEQ = 8
PAST_LEN = 16384
PAGE_SIZE = 128

PLE_DIM = 256
N_MIXERS = 3
EPS = 1e-6
D_FF = ((8 * D_MODEL + 3 * 256 - 1) // (3 * 256)) * 256
RW_HEAD = 64
RW_HEADS = D_MODEL // RW_HEAD
RW_LORA_W = 64
RW_LORA_A = 64
RW_LORA_V = 32
RW_LORA_G = 160
RW_GN_EPS = 64e-5
S5_GROUP = 16
S5_GROUPS = D_MODEL // S5_GROUP
S5_STATE = 64
M_INNER = 2 * D_MODEL
M_HEADDIM = 64
M_HEADS = M_INNER // M_HEADDIM
M_STATE = 128
M_GROUPS = 4
M_CONV = 4
M_CHUNK = 64
M_CONV_DIM = M_INNER + 2 * M_GROUPS * M_STATE

kernel_name = 'hybrid_rwkv7_s5_mamba2_step'


def rmsnorm(x, g):
    xf = x.astype(jnp.float32)
    y = xf * lax.rsqrt(jnp.mean(xf * xf, axis=-1, keepdims=True) + EPS)
    return (y * g.astype(jnp.float32)).astype(x.dtype)


def wkv7_scan(r, w, k, v, a, b, s0):
    def step(s, inp):
        r_t, w_t, k_t, v_t, a_t, b_t = inp
        sa = jnp.einsum('bhij,bhj->bhi', s, a_t)
        s = (s * w_t[:, :, None, :] + sa[..., None] * b_t[:, :, None, :]
             + v_t[..., None] * k_t[:, :, None, :])
        return s, jnp.einsum('bhij,bhj->bhi', s, r_t)
    seq = tuple(jnp.moveaxis(t.astype(jnp.float32), 1, 0) for t in (r, w, k, v, a, b))
    s_fin, out = lax.scan(step, s0.astype(jnp.float32), seq)
    return jnp.moveaxis(out, 0, 1), s_fin


def rwkv7_mix(x, shift_prev, wkv_prev, v_first, mu, w0, w1, w2, a0, a1, a2, g1, g2,
              k_k, k_a, r_k, w_rkv, w_o, lnx_w, lnx_b, v_lora=None):
    bsz, L, D = x.shape
    H, N = RW_HEADS, RW_HEAD
    x_prev = jnp.concatenate([shift_prev[:, None].astype(x.dtype), x[:, :-1]], axis=1)
    xx = x_prev - x
    xr, xw, xk, xv, xa, xg = (x + xx * mu[j] for j in range(6))
    r = xr @ w_rkv[0]
    k = xk @ w_rkv[1]
    v = xv @ w_rkv[2]
    w_log = -jax.nn.softplus(-(w0 + jnp.tanh(xw @ w1) @ w2)) - 0.5
    decay = jnp.exp(-jnp.exp(w_log.astype(jnp.float32)))
    if v_first is None:
        v_first = v
    else:
        v0, v1, v2 = v_lora
        v = v + (v_first - v) * jax.nn.sigmoid(v0 + (xv @ v1) @ v2)
    a = jax.nn.sigmoid(a0 + (xa @ a1) @ a2)
    g = jax.nn.sigmoid(xg @ g1) @ g2
    heads = lambda t: t.reshape(bsz, L, H, N)
    kk = heads(k * k_k).astype(jnp.float32)
    kk = kk * lax.rsqrt(jnp.maximum(jnp.sum(kk * kk, -1, keepdims=True), 1e-24))
    k = k * (1 + (a - 1) * k_a)
    r_h, k_h, v_h, a_h = heads(r), heads(k), heads(v), heads(a)
    out, wkv_new = wkv7_scan(r_h, heads(decay), k_h, v_h, -kk, kk * a_h, wkv_prev)
    mean = jnp.mean(out, -1, keepdims=True)
    var = jnp.mean(jnp.square(out - mean), -1, keepdims=True)
    o = ((out - mean) * lax.rsqrt(var + RW_GN_EPS)).reshape(bsz, L, D) * lnx_w + lnx_b
    bonus = (jnp.sum(r_h * k_h * r_k, -1, keepdims=True) * v_h).reshape(bsz, L, D)
    y = ((o + bonus) * g) @ w_o
    return y.astype(x.dtype), x[:, -1], wkv_new, v_first


def _complex_affine_combine(e1, e2):
    a1r, a1i, b1r, b1i = e1
    a2r, a2i, b2r, b2i = e2
    return (a2r * a1r - a2i * a1i, a2r * a1i + a2i * a1r,
            a2r * b1r - a2i * b1i + b2r, a2r * b1i + a2i * b1r + b2i)


def s5_mix(x, h_re0, h_im0, a_re, a_im, log_dt, b_re, b_im, c_re, c_im, d_skip, glu_v, glu_g):
    bsz, L, D = x.shape
    f32 = jnp.float32
    dt = jnp.exp(log_dt.astype(f32))[:, None]
    lr, li = a_re.astype(f32), a_im.astype(f32)
    mag = jnp.exp(lr * dt)
    ab_re, ab_im = mag * jnp.cos(li * dt), mag * jnp.sin(li * dt)
    den = lr * lr + li * li
    q_re = ((ab_re - 1.0) * lr + ab_im * li) / den
    q_im = (ab_im * lr - (ab_re - 1.0) * li) / den
    br, bi = b_re.astype(f32), b_im.astype(f32)
    bb_re = q_re[..., None] * br - q_im[..., None] * bi
    bb_im = q_re[..., None] * bi + q_im[..., None] * br
    u = x.astype(f32).reshape(bsz, L, S5_GROUPS, S5_GROUP)
    bu_re = jnp.einsum('blgh,gph->blgp', u, bb_re)
    bu_im = jnp.einsum('blgh,gph->blgp', u, bb_im)
    h_re0 = h_re0.astype(f32)
    h_im0 = h_im0.astype(f32)
    bu_re = bu_re.at[:, 0].add(ab_re * h_re0 - ab_im * h_im0)
    bu_im = bu_im.at[:, 0].add(ab_re * h_im0 + ab_im * h_re0)
    a_seq_re = jnp.broadcast_to(ab_re, (1, L) + ab_re.shape)
    a_seq_im = jnp.broadcast_to(ab_im, (1, L) + ab_im.shape)
    _, _, h_re, h_im = lax.associative_scan(
        _complex_affine_combine, (a_seq_re, a_seq_im, bu_re, bu_im), axis=1)
    y = (jnp.einsum('blgp,ghp->blgh', h_re, c_re.astype(f32))
         - jnp.einsum('blgp,ghp->blgh', h_im, c_im.astype(f32)))
    y = y.reshape(bsz, L, D) + d_skip * x.astype(f32)
    g = jax.nn.gelu(y)
    out = (g @ glu_v) * jax.nn.sigmoid(g @ glu_g)
    return out.astype(x.dtype), h_re[:, -1], h_im[:, -1]


def segsum(x):
    T = x.shape[-1]
    xr = jnp.broadcast_to(x[..., :, None], x.shape + (T,))
    strict = jnp.tril(jnp.ones((T, T), bool), -1)
    cs = jnp.cumsum(jnp.where(strict, xr, 0.0), axis=-2)
    return jnp.where(jnp.tril(jnp.ones((T, T), bool)), cs, -jnp.inf)


def ssd_chunked(X, A, Bm, Cm, s0):
    b, L, H, P = X.shape
    G, N = Bm.shape[2], Bm.shape[3]
    R = H // G
    T = min(M_CHUNK, L)
    Lp = -(-L // T) * T
    pad = Lp - L
    if pad:
        padw = lambda t: jnp.pad(t, [(0, 0), (0, pad)] + [(0, 0)] * (t.ndim - 2))
        X, A, Bm, Cm = padw(X), padw(A), padw(Bm), padw(Cm)
    c = Lp // T
    Xc = X.reshape(b, c, T, G, R, P)
    Ac = A.reshape(b, c, T, G, R).transpose(0, 1, 3, 4, 2)
    Bc = Bm.reshape(b, c, T, G, N)
    Cc = Cm.reshape(b, c, T, G, N)
    A_cum = jnp.cumsum(Ac, axis=-1)
    Lm = jnp.exp(segsum(Ac))
    CB = jnp.einsum('bclgn,bcsgn->bcgls', Cc, Bc)
    y_diag = jnp.einsum('bcgls,bcgrls,bcsgrp->bclgrp', CB, Lm, Xc)
    decay_states = jnp.exp(A_cum[..., -1:] - A_cum)
    states = jnp.einsum('bclgn,bcgrl,bclgrp->bcgrpn', Bc, decay_states, Xc)
    states = jnp.concatenate([s0.astype(states.dtype).reshape(b, 1, G, R, P, N), states], axis=1)
    chunk_a = jnp.pad(A_cum[..., -1], ((0, 0), (1, 0), (0, 0), (0, 0)))
    decay_chunk = jnp.exp(segsum(jnp.moveaxis(chunk_a, 1, -1)))
    new_states = jnp.einsum('bgrzc,bcgrpn->bzgrpn', decay_chunk, states)
    prev_states, final = new_states[:, :-1], new_states[:, -1]
    y_off = jnp.einsum('bclgn,bcgrpn,bcgrl->bclgrp', Cc, prev_states, jnp.exp(A_cum))
    y = (y_diag + y_off).reshape(b, Lp, H, P)[:, :L]
    return y, final.reshape(b, H, P, N)


def mamba2_mix(x, conv_prev, ssm_prev, in_proj, conv_w, conv_b, dt_bias, a_log, d_skip, norm_w, out_proj):
    bsz, L, _ = x.shape
    f32 = jnp.float32
    zxbcdt = x @ in_proj
    z = zxbcdt[..., :M_INNER]
    xbc = zxbcdt[..., M_INNER:M_INNER + M_CONV_DIM]
    dt_raw = zxbcdt[..., M_INNER + M_CONV_DIM:]
    xp = jnp.concatenate([conv_prev.astype(xbc.dtype), xbc], axis=1)
    conv = conv_b + sum(xp[:, j:j + L] * conv_w[j] for j in range(M_CONV))
    new_conv = xp[:, -(M_CONV - 1):]
    xbc = jax.nn.silu(conv)
    xs = xbc[..., :M_INNER].reshape(bsz, L, M_HEADS, M_HEADDIM)
    Bm = xbc[..., M_INNER:M_INNER + M_GROUPS * M_STATE].reshape(bsz, L, M_GROUPS, M_STATE)
    Cm = xbc[..., M_INNER + M_GROUPS * M_STATE:].reshape(bsz, L, M_GROUPS, M_STATE)
    dt = jax.nn.softplus(dt_raw.astype(f32) + dt_bias)
    A = -jnp.exp(a_log.astype(f32))
    y, ssm_new = ssd_chunked(xs * dt[..., None], dt * A, Bm, Cm, ssm_prev)
    y = y + d_skip[:, None] * xs
    y = y.reshape(bsz, L, M_INNER) * jax.nn.silu(z)
    yg = y.astype(f32).reshape(bsz, L, M_GROUPS, M_INNER // M_GROUPS)
    yg = yg * lax.rsqrt(jnp.mean(yg * yg, -1, keepdims=True) + EPS)
    y = yg.reshape(bsz, L, M_INNER) * norm_w
    return (y @ out_proj).astype(x.dtype), new_conv, ssm_new


def trunk(x, p, states, layer_params, norm_mix, norm_ffn, norm_ple, ffn_w1, ffn_w3, ffn_w2,
          ple_proj, ple_gate, final_norm):
    h = x
    v_first = None
    new_states = []
    for i in range(DEPTH):
        st_a, st_b = states[2 * i], states[2 * i + 1]
        prm = layer_params[i]
        xn = rmsnorm(h, norm_mix[i])
        kind = i % N_MIXERS
        if kind == 0:
            y, n_a, n_b, v_first = rwkv7_mix(xn, st_a, st_b, v_first, *prm)
        elif kind == 1:
            y, n_a, n_b = s5_mix(xn, st_a, st_b, *prm)
        else:
            y, n_a, n_b = mamba2_mix(xn, st_a, st_b, *prm)
        new_states += [n_a, n_b]
        h = h + y
        hn = rmsnorm(h, norm_ffn[i])
        h = h + (jax.nn.silu(hn @ ffn_w1[i]) * (hn @ ffn_w3[i])) @ ffn_w2[i]
        gate = jax.nn.sigmoid(rmsnorm(h, norm_ple[i]) @ ple_gate[i])
        h = h + gate * (p[i] @ ple_proj[i])
    return rmsnorm(h, final_norm), new_states


def setup_inputs(seed: int = 0) -> dict:
    key = jax.random.key(seed)
    keys = iter(jax.random.split(key, 256))
    f32 = jnp.float32
    D = D_MODEL

    def nrm(shape, scale):
        return scale * jax.random.normal(next(keys), shape, f32)

    def uni(shape, lo, hi):
        return jax.random.uniform(next(keys), shape, f32, lo, hi)

    d = {}
    d['x_prompt'] = nrm((BATCH, SEQ, D), 1.0)
    d['x_sample'] = nrm((DEC_BATCH, DEC_SEQ, D), 1.0)
    d['p_prompt'] = nrm((DEPTH, BATCH, SEQ, PLE_DIM), 1.0)
    d['p_sample'] = nrm((DEPTH, DEC_BATCH, DEC_SEQ, PLE_DIM), 1.0)
    d['state_l0_shift'] = nrm((DEC_BATCH, D), 1.0)
    d['state_l0_wkv'] = nrm((DEC_BATCH, RW_HEADS, RW_HEAD, RW_HEAD), 0.3)
    d['state_l1_s5_re'] = nrm((DEC_BATCH, S5_GROUPS, S5_STATE), 0.3)
    d['state_l1_s5_im'] = nrm((DEC_BATCH, S5_GROUPS, S5_STATE), 0.3)
    d['state_l2_conv'] = nrm((DEC_BATCH, M_CONV - 1, M_CONV_DIM), 1.0)
    d['state_l2_ssm'] = nrm((DEC_BATCH, M_HEADS, M_HEADDIM, M_STATE), 0.1)
    d['state_l3_shift'] = nrm((DEC_BATCH, D), 1.0)
    d['state_l3_wkv'] = nrm((DEC_BATCH, RW_HEADS, RW_HEAD, RW_HEAD), 0.3)

    def add_rwkv(pre, with_v):
        d[pre + 'mu'] = uni((6, D), 0.0, 1.0)
        d[pre + 'w0'] = uni((D,), -6.0, 1.0)
        d[pre + 'w1'] = nrm((D, RW_LORA_W), D ** -0.5)
        d[pre + 'w2'] = nrm((RW_LORA_W, D), 0.1 * RW_LORA_W ** -0.5)
        d[pre + 'a0'] = nrm((D,), 0.1)
        d[pre + 'a1'] = nrm((D, RW_LORA_A), D ** -0.5)
        d[pre + 'a2'] = nrm((RW_LORA_A, D), 0.1 * RW_LORA_A ** -0.5)
        d[pre + 'g1'] = nrm((D, RW_LORA_G), D ** -0.5)
        d[pre + 'g2'] = nrm((RW_LORA_G, D), RW_LORA_G ** -0.5)
        d[pre + 'k_k'] = 0.85 + nrm((D,), 0.05)
        d[pre + 'k_a'] = 1.0 + nrm((D,), 0.05)
        d[pre + 'r_k'] = nrm((RW_HEADS, RW_HEAD), 0.1)
        d[pre + 'w_rkv'] = nrm((3, D, D), D ** -0.5)
        d[pre + 'w_o'] = nrm((D, D), D ** -0.5)
        d[pre + 'lnx_w'] = 1.0 + nrm((D,), 0.05)
        d[pre + 'lnx_b'] = nrm((D,), 0.01)
        if with_v:
            d[pre + 'v0'] = nrm((D,), 0.1)
            d[pre + 'v1'] = nrm((D, RW_LORA_V), D ** -0.5)
            d[pre + 'v2'] = nrm((RW_LORA_V, D), 0.1 * RW_LORA_V ** -0.5)

    add_rwkv('l0_', False)
    n = jnp.arange(S5_STATE, dtype=f32)
    d['l1_a_re'] = -0.5 + nrm((S5_GROUPS, S5_STATE), 0.01)
    d['l1_a_im'] = np.pi * n[None, :] + nrm((S5_GROUPS, S5_STATE), 0.01)
    d['l1_log_dt'] = uni((S5_GROUPS,), float(np.log(1e-3)), float(np.log(1e-1)))
    d['l1_b_re'] = nrm((S5_GROUPS, S5_STATE, S5_GROUP), (2 * S5_GROUP) ** -0.5)
    d['l1_b_im'] = nrm((S5_GROUPS, S5_STATE, S5_GROUP), (2 * S5_GROUP) ** -0.5)
    d['l1_c_re'] = nrm((S5_GROUPS, S5_GROUP, S5_STATE), S5_STATE ** -0.5)
    d['l1_c_im'] = nrm((S5_GROUPS, S5_GROUP, S5_STATE), S5_STATE ** -0.5)
    d['l1_d'] = nrm((D,), 1.0)
    d['l1_glu_v'] = nrm((D, D), D ** -0.5)
    d['l1_glu_g'] = nrm((D, D), D ** -0.5)

    d['l2_in_proj'] = nrm((D, M_INNER + M_CONV_DIM + M_HEADS), D ** -0.5)
    d['l2_conv_w'] = nrm((M_CONV, M_CONV_DIM), 0.5)
    d['l2_conv_b'] = nrm((M_CONV_DIM,), 0.1)
    dt0 = jnp.exp(uni((M_HEADS,), float(np.log(1e-3)), float(np.log(1e-1))))
    d['l2_dt_bias'] = dt0 + jnp.log(-jnp.expm1(-dt0))
    d['l2_a_log'] = jnp.log(uni((M_HEADS,), 1.0, 16.0))
    d['l2_d'] = 1.0 + nrm((M_HEADS,), 0.1)
    d['l2_norm_w'] = 1.0 + nrm((M_INNER,), 0.05)
    d['l2_out_proj'] = nrm((M_INNER, D), M_INNER ** -0.5)

    add_rwkv('l3_', True)

    d['norm_mix'] = 1.0 + nrm((DEPTH, D), 0.05)
    d['norm_ffn'] = 1.0 + nrm((DEPTH, D), 0.05)
    d['norm_ple'] = 1.0 + nrm((DEPTH, D), 0.05)
    d['ffn_w1'] = nrm((DEPTH, D, D_FF), D ** -0.5)
    d['ffn_w3'] = nrm((DEPTH, D, D_FF), D ** -0.5)
    d['ffn_w2'] = nrm((DEPTH, D_FF, D), D_FF ** -0.5)
    d['ple_proj'] = nrm((DEPTH, PLE_DIM, D), PLE_DIM ** -0.5)
    d['ple_gate'] = nrm((DEPTH, D, D), D ** -0.5)
    d['final_norm'] = 1.0 + nrm((D,), 0.05)
    return d


def reference(x_prompt, x_sample, p_prompt, p_sample,
              state_l0_shift, state_l0_wkv, state_l1_s5_re, state_l1_s5_im,
              state_l2_conv, state_l2_ssm, state_l3_shift, state_l3_wkv,
              l0_mu, l0_w0, l0_w1, l0_w2, l0_a0, l0_a1, l0_a2, l0_g1, l0_g2,
              l0_k_k, l0_k_a, l0_r_k, l0_w_rkv, l0_w_o, l0_lnx_w, l0_lnx_b,
              l1_a_re, l1_a_im, l1_log_dt, l1_b_re, l1_b_im, l1_c_re, l1_c_im,
              l1_d, l1_glu_v, l1_glu_g,
              l2_in_proj, l2_conv_w, l2_conv_b, l2_dt_bias, l2_a_log, l2_d,
              l2_norm_w, l2_out_proj,
              l3_mu, l3_w0, l3_w1, l3_w2, l3_a0, l3_a1, l3_a2, l3_g1, l3_g2,
              l3_k_k, l3_k_a, l3_r_k, l3_w_rkv, l3_w_o, l3_lnx_w, l3_lnx_b,
              l3_v0, l3_v1, l3_v2,
              norm_mix, norm_ffn, norm_ple, ffn_w1, ffn_w3, ffn_w2,
              ple_proj, ple_gate, final_norm):
    rw0 = (l0_mu, l0_w0, l0_w1, l0_w2, l0_a0, l0_a1, l0_a2, l0_g1, l0_g2,
           l0_k_k, l0_k_a, l0_r_k, l0_w_rkv, l0_w_o, l0_lnx_w, l0_lnx_b)
    s5p = (l1_a_re, l1_a_im, l1_log_dt, l1_b_re, l1_b_im, l1_c_re, l1_c_im,
           l1_d, l1_glu_v, l1_glu_g)
    mbp = (l2_in_proj, l2_conv_w, l2_conv_b, l2_dt_bias, l2_a_log, l2_d, l2_norm_w, l2_out_proj)
    rw3 = (l3_mu, l3_w0, l3_w1, l3_w2, l3_a0, l3_a1, l3_a2, l3_g1, l3_g2,
           l3_k_k, l3_k_a, l3_r_k, l3_w_rkv, l3_w_o, l3_lnx_w, l3_lnx_b,
           (l3_v0, l3_v1, l3_v2))
    layer_params = (rw0, s5p, mbp, rw3)
    sample_states = [state_l0_shift, state_l0_wkv, state_l1_s5_re, state_l1_s5_im,
                     state_l2_conv, state_l2_ssm, state_l3_shift, state_l3_wkv]
    bp = x_prompt.shape[0]
    prompt_states = [jnp.zeros((bp,) + s.shape[1:], jnp.float32) for s in sample_states]

    y_prompt, new_p = trunk(x_prompt, p_prompt, prompt_states, layer_params,
                            norm_mix, norm_ffn, norm_ple, ffn_w1, ffn_w3, ffn_w2,
                            ple_proj, ple_gate, final_norm)
    y_sample, new_s = trunk(x_sample, p_sample, sample_states, layer_params,
                            norm_mix, norm_ffn, norm_ple, ffn_w1, ffn_w3, ffn_w2,
                            ple_proj, ple_gate, final_norm)
    (p_shift0, p_wkv0, p_s5re1, p_s5im1, p_conv2, p_ssm2, p_shift3, p_wkv3) = new_p
    (s_shift0, s_wkv0, s_s5re1, s_s5im1, s_conv2, s_ssm2, s_shift3, s_wkv3) = new_s
    return (y_prompt, y_sample,
            p_shift0, p_wkv0, p_s5re1, p_s5im1, p_conv2, p_ssm2, p_shift3, p_wkv3,
            s_shift0, s_wkv0, s_s5re1, s_s5im1, s_conv2, s_ssm2, s_shift3, s_wkv3)
```

```python
import functools
import math

import jax
import jax.numpy as jnp
from jax import lax
from jax.experimental import pallas as pl
from jax.experimental.pallas import tpu as pltpu

F32 = jnp.float32
BF16 = jnp.bfloat16
HIGHEST = lax.Precision.HIGHEST

D_MODEL = 1024
DEPTH = 4
PLE_DIM = 256
EPS = 1e-6
D_FF = 2816
RW_HEAD = 64
RW_HEADS = D_MODEL // RW_HEAD
RW_GN_EPS = 64e-5
S5_GROUP = 16
S5_GROUPS = D_MODEL // S5_GROUP
S5_STATE = 64
S5_LANES = S5_GROUPS * S5_STATE
M_INNER = 2 * D_MODEL
M_HEADDIM = 64
M_HEADS = M_INNER // M_HEADDIM
M_STATE = 128
M_GROUPS = 4
M_CONV = 4
M_CHUNK = 64
M_CONV_DIM = M_INNER + 2 * M_GROUPS * M_STATE

LANES = 128
SUBLANES = 8
VMEM_LIMIT = 56 * 1024 * 1024

SCAN_PREC = HIGHEST


def _cparams(sem):
    return pltpu.CompilerParams(dimension_semantics=sem, vmem_limit_bytes=VMEM_LIMIT)


def _dot(a, b, prec=None):
    return jnp.dot(a, b, preferred_element_type=F32, precision=prec)


def _dot_nt(a, b, prec=None):
    return lax.dot_general(a, b, (((1,), (1,)), ((), ())),
                           preferred_element_type=F32, precision=prec)


def _dot_tn(a, b, prec=None):
    return lax.dot_general(a, b, (((0,), (0,)), ((), ())),
                           preferred_element_type=F32, precision=prec)


def _rms(x, g):
    return x * lax.rsqrt(jnp.mean(x * x, axis=-1, keepdims=True) + EPS) * g


def _log_sigmoid(z):
    return -(jnp.maximum(-z, 0.0) + jnp.log1p(jnp.exp(-jnp.abs(z))))


def _softplus(z):
    return jnp.maximum(z, 0.0) + jnp.log1p(jnp.exp(-jnp.abs(z)))


def _full(shape):
    n = len(shape)
    return pl.BlockSpec(shape, lambda *_: (0,) * n)


def _rmsnorm_kernel(x_ref, g_ref, o_ref):
    o_ref[...] = _rms(x_ref[...], g_ref[...])


def rmsnorm_rows(x, g, tm):
    n, d = x.shape
    return pl.pallas_call(
        _rmsnorm_kernel,
        grid=(n // tm,),
        in_specs=[pl.BlockSpec((tm, d), lambda i: (i, 0)), _full((1, d))],
        out_specs=pl.BlockSpec((tm, d), lambda i: (i, 0)),
        out_shape=jax.ShapeDtypeStruct((n, d), F32),
        compiler_params=_cparams(("parallel",)),
        name="rmsnorm",
    )(x, g.reshape(1, d))


def _ffn_kernel(h_ref, y_ref, p_ref, nf_ref, np_ref, nn_ref, w1_ref, w3_ref, w2_ref,
                pg_ref, pp_ref, h_out, xn_out, hin_s, hn_s, acc_s, *, n_ff):
    j = pl.program_id(1)

    @pl.when(j == 0)
    def _():
        hin = h_ref[...] + y_ref[...]
        hin_s[...] = hin
        hn_s[...] = _rms(hin, nf_ref[...]).astype(BF16)
        acc_s[...] = jnp.zeros_like(acc_s)

    hn = hn_s[...]
    a = _dot(hn, w1_ref[...])
    b = _dot(hn, w3_ref[...])
    t = (a * jax.nn.sigmoid(a) * b).astype(BF16)
    acc_s[...] += _dot(t, w2_ref[...])

    @pl.when(j == n_ff - 1)
    def _():
        h2 = hin_s[...] + acc_s[...]
        gate = jax.nn.sigmoid(_dot(_rms(h2, np_ref[...]).astype(BF16), pg_ref[...]))
        h3 = h2 + gate * _dot(p_ref[...].astype(BF16), pp_ref[...])
        h_out[...] = h3
        xn_out[...] = _rms(h3, nn_ref[...])


def ffn_ple(h, y, p, nf, npl, nn, w1, w3, w2, pg, pp, *, tm, tf):
    n, d = h.shape
    n_ff = D_FF // tf
    row = lambda i, j: (i, 0)
    return pl.pallas_call(
        functools.partial(_ffn_kernel, n_ff=n_ff),
        grid=(n // tm, n_ff),
        in_specs=[
            pl.BlockSpec((tm, d), row), pl.BlockSpec((tm, d), row),
            pl.BlockSpec((tm, PLE_DIM), row),
            _full((1, d)), _full((1, d)), _full((1, d)),
            pl.BlockSpec((d, tf), lambda i, j: (0, j)),
            pl.BlockSpec((d, tf), lambda i, j: (0, j)),
            pl.BlockSpec((tf, d), lambda i, j: (j, 0)),
            _full((d, d)), _full((PLE_DIM, d)),
        ],
        out_specs=[pl.BlockSpec((tm, d), row), pl.BlockSpec((tm, d), row)],
        out_shape=[jax.ShapeDtypeStruct((n, d), F32)] * 2,
        scratch_shapes=[pltpu.VMEM((tm, d), F32), pltpu.VMEM((tm, d), BF16),
                        pltpu.VMEM((tm, d), F32)],
        compiler_params=_cparams(("parallel", "arbitrary")),
        name="ffn_ple",
    )(h, y, p, nf.reshape(1, d), npl.reshape(1, d), nn.reshape(1, d), w1, w3, w2, pg, pp)


def _mm_kernel(x_ref, w_ref, o_ref):
    o_ref[...] = _dot(x_ref[...].astype(BF16), w_ref[...])


def matmul_rows(x, w, *, tm):
    n, k = x.shape
    m = w.shape[1]
    return pl.pallas_call(
        _mm_kernel,
        grid=(n // tm,),
        in_specs=[pl.BlockSpec((tm, k), lambda i: (i, 0)), _full((k, m))],
        out_specs=pl.BlockSpec((tm, m), lambda i: (i, 0)),
        out_shape=jax.ShapeDtypeStruct((n, m), F32),
        compiler_params=_cparams(("parallel",)),
        name="matmul_rows",
    )(x, w)


def _glu_kernel(x_ref, wv_ref, wg_ref, o_ref):
    g = x_ref[...].astype(BF16)
    o_ref[...] = _dot(g, wv_ref[...]) * jax.nn.sigmoid(_dot(g, wg_ref[...]))


def glu_rows(x, wv, wg, *, tm):
    n, k = x.shape
    m = wv.shape[1]
    return pl.pallas_call(
        _glu_kernel,
        grid=(n // tm,),
        in_specs=[pl.BlockSpec((tm, k), lambda i: (i, 0)), _full((k, m)), _full((k, m))],
        out_specs=pl.BlockSpec((tm, m), lambda i: (i, 0)),
        out_shape=jax.ShapeDtypeStruct((n, m), F32),
        compiler_params=_cparams(("parallel",)),
        name="glu_rows",
    )(x, wv, wg)


def _rwkv_proj_kernel(*refs, bb, tl, with_v):
    if with_v:
        (x_ref, sh_ref, vf_ref, mu_ref, b0_ref, wrkv_ref, w1_ref, w2_ref, a1_ref, a2_ref,
         g1_ref, g2_ref, v1_ref, v2_ref,
         r_out, lw_out, k_out, v_out, a_out, g_out, carry) = refs
    else:
        (x_ref, sh_ref, mu_ref, b0_ref, wrkv_ref, w1_ref, w2_ref, a1_ref, a2_ref,
         g1_ref, g2_ref,
         r_out, lw_out, k_out, v_out, a_out, g_out, carry) = refs
    d = x_ref.shape[-1]
    m = bb * tl

    @pl.when(pl.program_id(1) == 0)
    def _():
        carry[...] = sh_ref[...]

    x3 = x_ref[...]
    x = x3.reshape(m, d)
    prev = jnp.broadcast_to(carry[...], (bb, tl, d)).reshape(m, d)
    rolled = pltpu.roll(x, 1, 0)
    t_in_tile = lax.broadcasted_iota(jnp.int32, (m, d), 0) & (tl - 1)
    x_prev = jnp.where(t_in_tile == 0, prev, rolled)
    carry[...] = x_ref[:, tl - 1:tl, :]

    xx = x_prev - x

    def mix(j):
        return (x + xx * mu_ref[j:j + 1, :]).astype(BF16)

    xr, xw, xk, xv, xa, xg = (mix(j) for j in range(6))
    r = _dot(xr, wrkv_ref[0])
    k = _dot(xk, wrkv_ref[1])
    v = _dot(xv, wrkv_ref[2])
    zw = b0_ref[0:1, :] + _dot(jnp.tanh(_dot(xw, w1_ref[...])).astype(BF16), w2_ref[...])
    lw = -jnp.exp(_log_sigmoid(zw) - 0.5)
    a = jax.nn.sigmoid(b0_ref[1:2, :] + _dot(_dot(xa, a1_ref[...]).astype(BF16), a2_ref[...]))
    g = _dot(jax.nn.sigmoid(_dot(xg, g1_ref[...])).astype(BF16), g2_ref[...])
    if with_v:
        vf = vf_ref[...].reshape(m, d)
        lam = jax.nn.sigmoid(b0_ref[2:3, :]
                             + _dot(_dot(xv, v1_ref[...]).astype(BF16), v2_ref[...]))
        v = v + (vf - v) * lam
    shp = (bb, tl, d)
    r_out[...] = r.reshape(shp)
    lw_out[...] = lw.reshape(shp)
    k_out[...] = k.reshape(shp)
    v_out[...] = v.reshape(shp)
    a_out[...] = a.reshape(shp)
    g_out[...] = g.reshape(shp)


def rwkv_proj(xn, shift, v_first, prm, *, bb, tl):
    bsz, L, d = xn.shape
    with_v = v_first is not None
    seq = pl.BlockSpec((bb, tl, d), lambda b, l: (b, l, 0))
    ins = [xn, shift.reshape(bsz, 1, d)]
    specs = [seq, pl.BlockSpec((bb, 1, d), lambda b, l: (b, 0, 0))]
    if with_v:
        ins.append(v_first)
        specs.append(seq)
    names = ["mu", "b0", "w_rkv", "w1", "w2", "a1", "a2", "g1", "g2"]
    if with_v:
        names += ["v1", "v2"]
    for nme in names:
        ins.append(prm[nme])
        specs.append(_full(prm[nme].shape))
    return pl.pallas_call(
        functools.partial(_rwkv_proj_kernel, bb=bb, tl=tl, with_v=with_v),
        grid=(bsz // bb, L // tl),
        in_specs=specs,
        out_specs=[seq] * 6,
        out_shape=[jax.ShapeDtypeStruct((bsz, L, d), F32)] * 6,
        scratch_shapes=[pltpu.VMEM((bb, 1, d), F32)],
        compiler_params=_cparams(("parallel", "arbitrary")),
        name="rwkv_proj",
    )(*ins)


def _unit_lower_inverse(n_mat, c, prec):
    row = lax.broadcasted_iota(jnp.int32, (c, c), 0)
    col = lax.broadcasted_iota(jnp.int32, (c, c), 1)
    eye = jnp.where(row == col, 1.0, 0.0).astype(F32)

    def same_block(shift):
        return (row >> shift) == (col >> shift)

    n8 = jnp.where(same_block(3), n_mat, 0.0)
    n8_2 = _dot(n8, n8, prec)
    n8_4 = _dot(n8_2, n8_2, prec)
    t = eye + n8
    t = t + _dot(t, n8_2, prec)
    t = t + _dot(t, n8_4, prec)
    shift = 3
    while (1 << shift) < c:
        off = jnp.where(same_block(shift), 0.0, jnp.where(same_block(shift + 1), n_mat, 0.0))
        t = t + _dot(_dot(t, off, prec), t, prec)
        shift += 1
    return t


def _wkv_kernel(r_ref, lw_ref, k_ref, v_ref, a_ref, g_ref, hp_ref, s0_ref,
                z_out, s_out, s_scr, *, bb, tl, c, prec):
    n = RW_HEAD
    nheads = LANES // n
    nchunk = tl // c

    @pl.when(pl.program_id(2) == 0)
    def _():
        s_scr[...] = s0_ref[...]

    row = lax.broadcasted_iota(jnp.int32, (c, c), 0)
    col = lax.broadcasted_iota(jnp.int32, (c, c), 1)
    strict = row > col
    incl = row >= col
    tril = jnp.where(incl, 1.0, 0.0).astype(F32)
    k_k = hp_ref[0:1, :]
    k_a = hp_ref[1:2, :]
    r_k = hp_ref[2:3, :]
    ln_w = hp_ref[3:4, :]
    ln_b = hp_ref[4:5, :]

    def chunk(ci, carry):
        rows = pl.ds(pl.multiple_of(ci * c, c), c)
        for s in range(bb):
            r = r_ref[s, rows, :]
            lw = lw_ref[s, rows, :]
            k = k_ref[s, rows, :]
            v = v_ref[s, rows, :]
            a = a_ref[s, rows, :]
            g = g_ref[s, rows, :]
            cum = _dot(tril, lw, HIGHEST)
            cum_end = cum[c - 1:c, :]
            r_t = r * jnp.exp(cum)
            e_neg = jnp.exp(-cum)
            e_prev = jnp.exp(cum - lw)
            e_end = jnp.exp(cum_end - cum)
            g_end = jnp.exp(cum_end)
            kk_raw = k * k_k
            k_mod = k * (1.0 + (a - 1.0) * k_a)
            k_t = k_mod * e_neg
            k_end = k_mod * e_end
            rk = r * k_mod * r_k
            z_heads = []
            for h in range(nheads):
                sl = slice(h * n, (h + 1) * n)
                kk = kk_raw[:, sl]
                kk = kk * lax.rsqrt(jnp.maximum(jnp.sum(kk * kk, -1, keepdims=True), 1e-24))
                b_h = kk * a[:, sl]
                a_t = -kk * e_prev[:, sl]
                b_t = b_h * e_neg[:, sl]
                b_end = b_h * e_end[:, sl]
                v_h = v[:, sl]
                ar = jnp.concatenate([a_t, r_t[:, sl]], axis=0)
                ab = _dot_nt(ar, b_t, prec)
                ak = _dot_nt(ar, k_t[:, sl], prec)
                n_mat = jnp.where(strict, ab[:c], 0.0)
                a_ak = jnp.where(strict, ak[:c], 0.0)
                a_rb = jnp.where(incl, ab[c:], 0.0)
                a_rk = jnp.where(incl, ak[c:], 0.0)
                t_inv = _unit_lower_inverse(n_mat, c, prec)
                s0 = s_scr[s, h]
                rhs = _dot_nt(a_t, s0, prec) + _dot(a_ak, v_h, prec)
                u = _dot(t_inv, rhs, prec)
                o = (_dot_nt(r_t[:, sl], s0, prec) + _dot(a_rk, v_h, prec)
                     + _dot(a_rb, u, prec))
                s_scr[s, h] = (s0 * g_end[:, sl] + _dot_tn(v_h, k_end[:, sl], prec)
                               + _dot_tn(u, b_end, prec))
                mean = jnp.mean(o, -1, keepdims=True)
                var = jnp.mean(jnp.square(o - mean), -1, keepdims=True)
                o = (o - mean) * lax.rsqrt(var + RW_GN_EPS)
                bonus = jnp.sum(rk[:, sl], -1, keepdims=True) * v_h
                z_heads.append((o, bonus))
            o_all = jnp.concatenate([zh[0] for zh in z_heads], axis=-1)
            bonus_all = jnp.concatenate([zh[1] for zh in z_heads], axis=-1)
            z_out[s, rows, :] = (o_all * ln_w + ln_b + bonus_all) * g
        return carry

    lax.fori_loop(0, nchunk, chunk, 0)

    @pl.when(pl.program_id(2) == pl.num_programs(2) - 1)
    def _():
        s_out[...] = s_scr[...]


def wkv_scan(r, lw, k, v, a, g, hp, s0, *, bb, tl, c, prec):
    bsz, L, d = r.shape
    nheads = LANES // RW_HEAD
    seq = pl.BlockSpec((bb, tl, LANES), lambda b, h, l: (b, l, h))
    st = pl.BlockSpec((bb, nheads, RW_HEAD, RW_HEAD), lambda b, h, l: (b, h, 0, 0))
    return pl.pallas_call(
        functools.partial(_wkv_kernel, bb=bb, tl=tl, c=c, prec=prec),
        grid=(bsz // bb, d // LANES, L // tl),
        in_specs=[seq] * 6 + [pl.BlockSpec((SUBLANES, LANES), lambda b, h, l: (0, h)), st],
        out_specs=[seq, st],
        out_shape=[jax.ShapeDtypeStruct((bsz, L, d), F32),
                   jax.ShapeDtypeStruct(s0.shape, F32)],
        scratch_shapes=[pltpu.VMEM((bb, nheads, RW_HEAD, RW_HEAD), F32)],
        compiler_params=_cparams(("parallel", "parallel", "arbitrary")),
        name="wkv_scan",
    )(r, lw, k, v, a, g, hp, s0)


def rwkv_layer(xn, shift, wkv, v_first, prm, cfg):
    r, lw, k, v, a, g = rwkv_proj(xn, shift, v_first, prm, bb=cfg["proj_bb"], tl=cfg["proj_tl"])
    z, wkv_new = wkv_scan(r, lw, k, v, a, g, prm["hp"], wkv,
                          bb=cfg["scan_bb"], tl=cfg["wkv_tl"], c=cfg["wkv_c"], prec=SCAN_PREC)
    bsz, L, d = xn.shape
    y = matmul_rows(z.reshape(bsz * L, d), prm["w_o"], tm=cfg["tm"]).reshape(bsz, L, d)
    return y, xn[:, -1], wkv_new, (v if v_first is None else v_first)


def _prep_rwkv(mu, w0, w1, w2, a0, a1, a2, g1, g2, k_k, k_a, r_k, w_rkv, w_o, lnx_w, lnx_b,
               v_lora=None):
    d = D_MODEL
    zero = jnp.zeros((d,), F32)
    prm = dict(mu=mu, w_rkv=w_rkv.astype(BF16), w_o=w_o.astype(BF16),
               w1=w1.astype(BF16), w2=w2.astype(BF16), a1=a1.astype(BF16), a2=a2.astype(BF16),
               g1=g1.astype(BF16), g2=g2.astype(BF16))
    v0 = zero
    if v_lora is not None:
        v0, v1, v2 = v_lora
        prm["v1"] = v1.astype(BF16)
        prm["v2"] = v2.astype(BF16)
    prm["b0"] = jnp.stack([w0, a0, v0] + [zero] * 5)
    prm["hp"] = jnp.stack([k_k, k_a, r_k.reshape(d), lnx_w, lnx_b] + [zero] * 3)
    return prm


S5_CH = 8
S5_CL = S5_LANES // S5_CH


def _s5_kernel(u_ref, w_ref, cm_ref, ar_ref, ai_ref, d_ref, h0r_ref, h0i_ref,
               g_out, hr_out, hi_out, hs, hc, *, bsz, tr):
    i = pl.program_id(1)
    cl = S5_CL

    @pl.when(i == 0)
    def _():
        hc[:, :cl] = h0r_ref[...]
        hc[:, cl:] = h0i_ref[...]

    u = u_ref[...]
    hs[...] = _dot(u, w_ref[...], HIGHEST)
    ar = jnp.broadcast_to(ar_ref[...], (bsz, cl))
    ai = jnp.broadcast_to(ai_ref[...], (bsz, cl))

    def step(t, carry):
        hr, hi = carry
        rows = pl.ds(pl.multiple_of(t * bsz, bsz), bsz)
        nr = ar * hr - ai * hi + hs[rows, :cl]
        ni = ar * hi + ai * hr + hs[rows, cl:]
        hs[rows, :cl] = nr
        hs[rows, cl:] = ni
        return nr, ni

    hr, hi = lax.fori_loop(0, tr // bsz, step, (hc[:, :cl], hc[:, cl:]))
    hc[:, :cl] = hr
    hc[:, cl:] = hi
    y = _dot(hs[...], cm_ref[...], HIGHEST) + d_ref[...] * u
    g_out[...] = jax.nn.gelu(y)

    @pl.when(i == pl.num_programs(1) - 1)
    def _():
        hr_out[...] = hr
        hi_out[...] = hi


def s5_scan(u_tm, h0r, h0i, prm, *, bsz, tr):
    rows, d = u_tm.shape
    cl = S5_CL
    st = pl.BlockSpec((bsz, cl), lambda c, i: (0, c))
    return pl.pallas_call(
        functools.partial(_s5_kernel, bsz=bsz, tr=tr),
        grid=(S5_CH, rows // tr),
        in_specs=[
            pl.BlockSpec((tr, LANES), lambda c, i: (i, c)),
            pl.BlockSpec((None, LANES, 2 * cl), lambda c, i: (c, 0, 0)),
            pl.BlockSpec((None, 2 * cl, LANES), lambda c, i: (c, 0, 0)),
            pl.BlockSpec((None, 1, cl), lambda c, i: (c, 0, 0)),
            pl.BlockSpec((None, 1, cl), lambda c, i: (c, 0, 0)),
            pl.BlockSpec((1, LANES), lambda c, i: (0, c)),
            st, st,
        ],
        out_specs=[pl.BlockSpec((tr, LANES), lambda c, i: (i, c)), st, st],
        out_shape=[jax.ShapeDtypeStruct((rows, d), F32),
                   jax.ShapeDtypeStruct((bsz, S5_LANES), F32),
                   jax.ShapeDtypeStruct((bsz, S5_LANES), F32)],
        scratch_shapes=[pltpu.VMEM((tr, 2 * cl), F32), pltpu.VMEM((bsz, 2 * cl), F32)],
        compiler_params=_cparams(("parallel", "arbitrary")),
        name="s5_scan",
    )(u_tm, prm["w_in"], prm["c_out"], prm["ab_re"], prm["ab_im"], prm["d"], h0r, h0i)


def _prep_s5(a_re, a_im, log_dt, b_re, b_im, c_re, c_im, d_skip, glu_v, glu_g):
    dt = jnp.exp(log_dt.astype(F32))[:, None]
    lr, li = a_re.astype(F32), a_im.astype(F32)
    mag = jnp.exp(lr * dt)
    ab_re, ab_im = mag * jnp.cos(li * dt), mag * jnp.sin(li * dt)
    den = lr * lr + li * li
    q_re = ((ab_re - 1.0) * lr + ab_im * li) / den
    q_im = (ab_im * lr - (ab_re - 1.0) * li) / den
    bb_re = q_re[..., None] * b_re - q_im[..., None] * b_im
    bb_im = q_re[..., None] * b_im + q_im[..., None] * b_re
    gl = S5_GROUPS // S5_CH
    eye = jnp.eye(gl, dtype=F32)

    def in_blocks(bb):
        t = bb.reshape(S5_CH, gl, S5_STATE, S5_GROUP)
        return jnp.einsum('cgph,gk->cghkp', t, eye).reshape(S5_CH, LANES, S5_CL)

    def out_blocks(cc):
        t = cc.reshape(S5_CH, gl, S5_GROUP, S5_STATE)
        return jnp.einsum('cghp,gk->ckpgh', t, eye).reshape(S5_CH, S5_CL, LANES)

    return dict(
        w_in=jnp.concatenate([in_blocks(bb_re), in_blocks(bb_im)], axis=-1),
        c_out=jnp.concatenate([out_blocks(c_re.astype(F32)), -out_blocks(c_im.astype(F32))],
                              axis=1),
        ab_re=ab_re.reshape(S5_CH, 1, S5_CL), ab_im=ab_im.reshape(S5_CH, 1, S5_CL),
        d=d_skip.reshape(1, D_MODEL),
        glu_v=glu_v.astype(BF16), glu_g=glu_g.astype(BF16))


def s5_layer(xn, h_re0, h_im0, prm, cfg):
    bsz, L, d = xn.shape
    u_tm = jnp.swapaxes(xn, 0, 1).reshape(L * bsz, d)
    g, hr, hi = s5_scan(u_tm, h_re0.reshape(bsz, S5_LANES), h_im0.reshape(bsz, S5_LANES), prm,
                        bsz=bsz, tr=cfg["s5_tr"])
    y_tm = glu_rows(g, prm["glu_v"], prm["glu_g"], tm=cfg["tm"])
    y = jnp.swapaxes(y_tm.reshape(L, bsz, d), 0, 1)
    return (y, hr.reshape(bsz, S5_GROUPS, S5_STATE), hi.reshape(bsz, S5_GROUPS, S5_STATE))


def _mamba_in_kernel(x_ref, cp_ref, wz_ref, wx_ref, wdt_ref, cw_ref, cb_ref, dtb_ref,
                     z_out, xs_out, b_out, c_out, dt_out, nc_out, carry, *, bb, tl):
    d = x_ref.shape[-1]
    m = bb * tl
    cd = M_CONV_DIM
    l = pl.program_id(1)

    @pl.when(l == 0)
    def _():
        carry[:, :SUBLANES - (M_CONV - 1), :] = jnp.zeros(
            (bb, SUBLANES - (M_CONV - 1), cd), F32)
        carry[:, SUBLANES - (M_CONV - 1):, :] = cp_ref[...]

    x = x_ref[...].reshape(m, d).astype(BF16)
    z_out[...] = _dot(x, wz_ref[...]).reshape(bb, tl, M_INNER)
    dt_raw = _dot(x, wdt_ref[...])[:, :M_HEADS]
    dt_out[...] = _softplus(dt_raw + dtb_ref[...]).reshape(bb, tl, M_HEADS)
    xbc = _dot(x, wx_ref[...]).reshape(bb, tl, cd)
    full = jnp.concatenate([carry[...], xbc], axis=1).reshape(bb * (tl + SUBLANES), cd)
    conv = cb_ref[...] + xbc.reshape(m, cd) * cw_ref[M_CONV - 1:M_CONV, :]
    for j in range(1, M_CONV):
        sh = pltpu.roll(full, j, 0).reshape(bb, tl + SUBLANES, cd)[:, SUBLANES:, :]
        conv = conv + sh.reshape(m, cd) * cw_ref[M_CONV - 1 - j:M_CONV - j, :]
    act = conv * jax.nn.sigmoid(conv)
    xs_out[...] = act[:, :M_INNER].reshape(bb, tl, M_INNER)
    nbc = M_GROUPS * M_STATE
    b_out[...] = act[:, M_INNER:M_INNER + nbc].reshape(bb, tl, nbc)
    c_out[...] = act[:, M_INNER + nbc:].reshape(bb, tl, nbc)
    full3 = full.reshape(bb, tl + SUBLANES, cd)
    carry[...] = full3[:, tl:, :]

    @pl.when(l == pl.num_programs(1) - 1)
    def _():
        nc_out[...] = full3[:, tl + SUBLANES - (M_CONV - 1):, :]


def mamba_in(xn, conv_prev, prm, *, bb, tl):
    bsz, L, d = xn.shape
    cd = M_CONV_DIM
    nbc = M_GROUPS * M_STATE
    seq = lambda w: pl.BlockSpec((bb, tl, w), lambda b, l: (b, l, 0))
    cps = pl.BlockSpec((bb, M_CONV - 1, cd), lambda b, l: (b, 0, 0))
    return pl.pallas_call(
        functools.partial(_mamba_in_kernel, bb=bb, tl=tl),
        grid=(bsz // bb, L // tl),
        in_specs=[seq(d), cps, _full((d, M_INNER)), _full((d, cd)), _full((d, LANES)),
                  _full((M_CONV, cd)), _full((1, cd)), _full((1, M_HEADS))],
        out_specs=[seq(M_INNER), seq(M_INNER), seq(nbc), seq(nbc), seq(M_HEADS), cps],
        out_shape=[jax.ShapeDtypeStruct((bsz, L, M_INNER), F32),
                   jax.ShapeDtypeStruct((bsz, L, M_INNER), F32),
                   jax.ShapeDtypeStruct((bsz, L, nbc), F32),
                   jax.ShapeDtypeStruct((bsz, L, nbc), F32),
                   jax.ShapeDtypeStruct((bsz, L, M_HEADS), F32),
                   jax.ShapeDtypeStruct((bsz, M_CONV - 1, cd), F32)],
        scratch_shapes=[pltpu.VMEM((bb, SUBLANES, cd), F32)],
        compiler_params=_cparams(("parallel", "arbitrary")),
        name="mamba_in",
    )(xn, conv_prev, prm["wz"], prm["wx"], prm["wdt"], prm["conv_w"], prm["conv_b"],
      prm["dt_bias"])


def _ssd_kernel(xs_ref, b_ref, c_ref, z_ref, dt_ref, hv_ref, ex_ref, nw_ref, s0_ref,
                y_out, s_out, s_scr, *, bb, tl, c):
    p, n = M_HEADDIM, M_STATE
    hpg = M_HEADS // M_GROUPS
    gw = M_INNER // M_GROUPS
    nchunk = tl // c

    @pl.when(pl.program_id(1) == 0)
    def _():
        s_scr[...] = s0_ref[...]

    row = lax.broadcasted_iota(jnp.int32, (c, c), 0)
    col = lax.broadcasted_iota(jnp.int32, (c, c), 1)
    incl = row >= col
    tril = jnp.where(incl, 1.0, 0.0).astype(F32)
    triu = jnp.where(row <= col, 1.0, 0.0).astype(F32)
    a_neg = hv_ref[0:1, :]
    d_exp = nw_ref[1:2, :]
    norm_w = nw_ref[0:1, :]
    expand = ex_ref[...]

    def chunk(i, carry):
        s = i // nchunk
        ci = i - s * nchunk
        rows = pl.ds(pl.multiple_of(ci * c, c), c)
        xs = xs_ref[s, rows, :]
        bm = b_ref[s, rows, :]
        cm = c_ref[s, rows, :]
        dt = dt_ref[s, rows, :]
        a = dt * a_neg
        cum = _dot(tril, a, HIGHEST)
        cum_t = _dot_tn(a, triu, HIGHEST)
        cum_end = cum[c - 1:c, :]
        e_end = jnp.exp(cum_end)
        dt_x = _dot(dt, expand, HIGHEST)
        dec_x = _dot(jnp.exp(cum_end - cum), expand, HIGHEST)
        ecum_x = _dot(jnp.exp(cum), expand, HIGHEST)
        xdt = xs * dt_x
        xend = xdt * dec_x
        yd, yo = [], []
        for gi in range(M_GROUPS):
            b_g = bm[:, gi * n:(gi + 1) * n]
            c_g = cm[:, gi * n:(gi + 1) * n]
            cb = _dot_nt(c_g, b_g)
            for hh in range(hpg):
                h = gi * hpg + hh
                sl = slice(h * p, (h + 1) * p)
                seg = cum[:, h:h + 1] - cum_t[h:h + 1, :]
                lmat = jnp.where(incl, jnp.exp(jnp.where(incl, seg, 0.0)), 0.0)
                s_h = s_scr[s, h]
                yd.append(_dot(cb * lmat, xdt[:, sl]))
                yo.append(_dot_nt(c_g, s_h))
                s_scr[s, h] = s_h * e_end[:, h:h + 1] + _dot_tn(xend[:, sl], b_g)
        y = (jnp.concatenate(yd, axis=-1) + jnp.concatenate(yo, axis=-1) * ecum_x
             + d_exp * xs)
        zz = z_ref[s, rows, :]
        y = y * (zz * jax.nn.sigmoid(zz))
        outs = []
        for gi in range(M_GROUPS):
            yg = y[:, gi * gw:(gi + 1) * gw]
            outs.append(yg * lax.rsqrt(jnp.mean(yg * yg, -1, keepdims=True) + EPS))
        y_out[s, rows, :] = jnp.concatenate(outs, axis=-1) * norm_w
        return carry

    lax.fori_loop(0, bb * nchunk, chunk, 0)

    @pl.when(pl.program_id(1) == pl.num_programs(1) - 1)
    def _():
        s_out[...] = s_scr[...]


def ssd_scan(xs, bm, cm, z, dt, prm, s0, *, bb, tl, c):
    bsz, L, _ = xs.shape
    nbc = M_GROUPS * M_STATE
    seq = lambda w: pl.BlockSpec((bb, tl, w), lambda b, l: (b, l, 0))
    st = pl.BlockSpec((bb, M_HEADS, M_HEADDIM, M_STATE), lambda b, l: (b, 0, 0, 0))
    return pl.pallas_call(
        functools.partial(_ssd_kernel, bb=bb, tl=tl, c=c),
        grid=(bsz // bb, L // tl),
        in_specs=[seq(M_INNER), seq(nbc), seq(nbc), seq(M_INNER), seq(M_HEADS),
                  _full((SUBLANES, M_HEADS)), _full((M_HEADS, M_INNER)),
                  _full((SUBLANES, M_INNER)), st],
        out_specs=[seq(M_INNER), st],
        out_shape=[jax.ShapeDtypeStruct((bsz, L, M_INNER), F32),
                   jax.ShapeDtypeStruct(s0.shape, F32)],
        scratch_shapes=[pltpu.VMEM((bb, M_HEADS, M_HEADDIM, M_STATE), F32)],
        compiler_params=_cparams(("parallel", "arbitrary")),
        name="ssd_scan",
    )(xs, bm, cm, z, dt, prm["hv"], prm["expand"], prm["nw"], s0)


def _prep_mamba(in_proj, conv_w, conv_b, dt_bias, a_log, d_skip, norm_w, out_proj):
    d = D_MODEL
    wdt = jnp.zeros((d, LANES), F32).at[:, :M_HEADS].set(in_proj[:, M_INNER + M_CONV_DIM:])
    expand = jnp.repeat(jnp.eye(M_HEADS, dtype=F32), M_HEADDIM, axis=1)
    zrow_h = jnp.zeros((M_HEADS,), F32)
    zrow_i = jnp.zeros((M_INNER,), F32)
    return dict(
        wz=in_proj[:, :M_INNER].astype(BF16),
        wx=in_proj[:, M_INNER:M_INNER + M_CONV_DIM].astype(BF16),
        wdt=wdt.astype(BF16),
        conv_w=conv_w, conv_b=conv_b.reshape(1, M_CONV_DIM),
        dt_bias=dt_bias.reshape(1, M_HEADS),
        hv=jnp.stack([-jnp.exp(a_log.astype(F32))] + [zrow_h] * 7),
        expand=expand,
        nw=jnp.stack([norm_w, jnp.repeat(d_skip, M_HEADDIM)] + [zrow_i] * 6),
        out_proj=out_proj.astype(BF16))


def mamba_layer(xn, conv_prev, ssm_prev, prm, cfg):
    bsz, L, d = xn.shape
    z, xs, bm, cm, dt, new_conv = mamba_in(xn, conv_prev, prm, bb=cfg["mamba_bb"],
                                           tl=cfg["mamba_tl"])
    y, ssm_new = ssd_scan(xs, bm, cm, z, dt, prm, ssm_prev,
                          bb=cfg["ssd_bb"], tl=cfg["ssd_tl"], c=cfg["ssd_c"])
    out = matmul_rows(y.reshape(bsz * L, M_INNER), prm["out_proj"], tm=cfg["tm"])
    return out.reshape(bsz, L, d), new_conv, ssm_new


def _config(bsz, L):
    if L >= 512:
        return dict(tm=512, tf=D_FF // 2, proj_bb=1, proj_tl=256,
                    scan_bb=min(bsz, 8), wkv_tl=512, wkv_c=64, mamba_bb=1, mamba_tl=128,
                    ssd_bb=1, ssd_tl=256, ssd_c=M_CHUNK, s5_tr=512)
    rows = bsz * L
    tm = min(512, rows)
    return dict(tm=tm, tf=D_FF // 2, proj_bb=min(bsz, 256 // L), proj_tl=L,
                scan_bb=min(bsz, 8), wkv_tl=L, wkv_c=L, mamba_bb=min(bsz, 128 // L), mamba_tl=L,
                ssd_bb=min(bsz, 4), ssd_tl=L, ssd_c=min(M_CHUNK, L), s5_tr=rows)


def _trunk(x, p, states, layer_prms, shared, cfg):
    bsz, L, d = x.shape
    n = bsz * L
    tm = cfg["tm"]
    xn = rmsnorm_rows(x.reshape(n, d), shared["norm_mix"][0], tm).reshape(bsz, L, d)
    h = x.reshape(n, d)
    v_first = None
    new_states = []
    for i in range(DEPTH):
        st_a, st_b = states[2 * i], states[2 * i + 1]
        prm = layer_prms[i]
        kind = i % 3
        if kind == 0:
            y, n_a, n_b, v_first = rwkv_layer(xn, st_a, st_b, v_first, prm, cfg)
        elif kind == 1:
            y, n_a, n_b = s5_layer(xn, st_a, st_b, prm, cfg)
        else:
            y, n_a, n_b = mamba_layer(xn, st_a, st_b, prm, cfg)
        new_states += [n_a, n_b]
        nxt = shared["norm_mix"][i + 1] if i + 1 < DEPTH else shared["final_norm"]
        h, xn_flat = ffn_ple(h, y.reshape(n, d), p[i].reshape(n, PLE_DIM),
                             shared["norm_ffn"][i], shared["norm_ple"][i], nxt,
                             shared["ffn_w1"][i], shared["ffn_w3"][i], shared["ffn_w2"][i],
                             shared["ple_gate"][i], shared["ple_proj"][i], tm=tm, tf=cfg["tf"])
        xn = xn_flat.reshape(bsz, L, d)
    return xn, new_states


def kernel(x_prompt, x_sample, p_prompt, p_sample, state_l0_shift, state_l0_wkv, state_l1_s5_re, state_l1_s5_im, state_l2_conv, state_l2_ssm, state_l3_shift, state_l3_wkv, l0_mu, l0_w0, l0_w1, l0_w2, l0_a0, l0_a1, l0_a2, l0_g1, l0_g2, l0_k_k, l0_k_a, l0_r_k, l0_w_rkv, l0_w_o, l0_lnx_w, l0_lnx_b, l1_a_re, l1_a_im, l1_log_dt, l1_b_re, l1_b_im, l1_c_re, l1_c_im, l1_d, l1_glu_v, l1_glu_g, l2_in_proj, l2_conv_w, l2_conv_b, l2_dt_bias, l2_a_log, l2_d, l2_norm_w, l2_out_proj, l3_mu, l3_w0, l3_w1, l3_w2, l3_a0, l3_a1, l3_a2, l3_g1, l3_g2, l3_k_k, l3_k_a, l3_r_k, l3_w_rkv, l3_w_o, l3_lnx_w, l3_lnx_b, l3_v0, l3_v1, l3_v2, norm_mix, norm_ffn, norm_ple, ffn_w1, ffn_w3, ffn_w2, ple_proj, ple_gate, final_norm):
    layer_prms = (
        _prep_rwkv(l0_mu, l0_w0, l0_w1, l0_w2, l0_a0, l0_a1, l0_a2, l0_g1, l0_g2,
                   l0_k_k, l0_k_a, l0_r_k, l0_w_rkv, l0_w_o, l0_lnx_w, l0_lnx_b),
        _prep_s5(l1_a_re, l1_a_im, l1_log_dt, l1_b_re, l1_b_im, l1_c_re, l1_c_im,
                 l1_d, l1_glu_v, l1_glu_g),
        _prep_mamba(l2_in_proj, l2_conv_w, l2_conv_b, l2_dt_bias, l2_a_log, l2_d,
                    l2_norm_w, l2_out_proj),
        _prep_rwkv(l3_mu, l3_w0, l3_w1, l3_w2, l3_a0, l3_a1, l3_a2, l3_g1, l3_g2,
                   l3_k_k, l3_k_a, l3_r_k, l3_w_rkv, l3_w_o, l3_lnx_w, l3_lnx_b,
                   (l3_v0, l3_v1, l3_v2)),
    )
    shared = dict(norm_mix=norm_mix, norm_ffn=norm_ffn, norm_ple=norm_ple,
                  final_norm=final_norm,
                  ffn_w1=ffn_w1.astype(BF16), ffn_w3=ffn_w3.astype(BF16),
                  ffn_w2=ffn_w2.astype(BF16), ple_gate=ple_gate.astype(BF16),
                  ple_proj=ple_proj.astype(BF16))
    sample_states = [state_l0_shift, state_l0_wkv, state_l1_s5_re, state_l1_s5_im,
                     state_l2_conv, state_l2_ssm, state_l3_shift, state_l3_wkv]
    bp = x_prompt.shape[0]
    prompt_states = [jnp.zeros((bp,) + s.shape[1:], F32) for s in sample_states]
    y_prompt, new_p = _trunk(x_prompt, p_prompt, prompt_states, layer_prms, shared,
                             _config(*x_prompt.shape[:2]))
    y_sample, new_s = _trunk(x_sample, p_sample, sample_states, layer_prms, shared,
                             _config(*x_sample.shape[:2]))
    return (y_prompt, y_sample, *new_p, *new_s)
```

```python
import functools
import math

import jax
import jax.numpy as jnp
from jax import lax
from jax.experimental import pallas as pl
from jax.experimental.pallas import tpu as pltpu

F32 = jnp.float32
BF16 = jnp.bfloat16
HIGHEST = lax.Precision.HIGHEST

D_MODEL = 1024
DEPTH = 4
PLE_DIM = 256
EPS = 1e-6
D_FF = 2816
RW_HEAD = 64
RW_HEADS = D_MODEL // RW_HEAD
RW_GN_EPS = 64e-5
S5_GROUP = 16
S5_GROUPS = D_MODEL // S5_GROUP
S5_STATE = 64
S5_LANES = S5_GROUPS * S5_STATE
M_INNER = 2 * D_MODEL
M_HEADDIM = 64
M_HEADS = M_INNER // M_HEADDIM
M_STATE = 128
M_GROUPS = 4
M_CONV = 4
M_CHUNK = 64
M_CONV_DIM = M_INNER + 2 * M_GROUPS * M_STATE

LANES = 128
SUBLANES = 8
VMEM_LIMIT = 56 * 1024 * 1024

SCAN_PREC = None


def _cparams(sem):
    return pltpu.CompilerParams(dimension_semantics=sem, vmem_limit_bytes=VMEM_LIMIT)


def _operands(a, b, prec):
    if prec is None:
        return a.astype(BF16), b.astype(BF16)
    return a, b


def _dot(a, b, prec=None):
    a, b = _operands(a, b, prec)
    return jnp.dot(a, b, preferred_element_type=F32, precision=prec)


def _dot_nt(a, b, prec=None):
    a, b = _operands(a, b, prec)
    return lax.dot_general(a, b, (((1,), (1,)), ((), ())),
                           preferred_element_type=F32, precision=prec)


def _dot_tn(a, b, prec=None):
    a, b = _operands(a, b, prec)
    return lax.dot_general(a, b, (((0,), (0,)), ((), ())),
                           preferred_element_type=F32, precision=prec)


def _rms(x, g):
    return x * lax.rsqrt(jnp.mean(x * x, axis=-1, keepdims=True) + EPS) * g


def _log_sigmoid(z):
    return -(jnp.maximum(-z, 0.0) + jnp.log1p(jnp.exp(-jnp.abs(z))))


def _softplus(z):
    return jnp.maximum(z, 0.0) + jnp.log1p(jnp.exp(-jnp.abs(z)))


def _full(shape):
    n = len(shape)
    return pl.BlockSpec(shape, lambda *_: (0,) * n)


def _rmsnorm_kernel(x_ref, g_ref, o_ref):
    o_ref[...] = _rms(x_ref[...], g_ref[...])


def rmsnorm_rows(x, g, tm):
    n, d = x.shape
    return pl.pallas_call(
        _rmsnorm_kernel,
        grid=(n // tm,),
        in_specs=[pl.BlockSpec((tm, d), lambda i: (i, 0)), _full((1, d))],
        out_specs=pl.BlockSpec((tm, d), lambda i: (i, 0)),
        out_shape=jax.ShapeDtypeStruct((n, d), F32),
        compiler_params=_cparams(("parallel",)),
        name="rmsnorm",
    )(x, g.reshape(1, d))


def _ffn_kernel(h_ref, y_ref, p_ref, nf_ref, np_ref, nn_ref, w1_ref, w3_ref, w2_ref,
                pg_ref, pp_ref, h_out, xn_out, hin_s, hn_s, acc_s, *, n_ff):
    j = pl.program_id(1)

    @pl.when(j == 0)
    def _():
        hin = h_ref[...] + y_ref[...]
        hin_s[...] = hin
        hn_s[...] = _rms(hin, nf_ref[...]).astype(BF16)
        acc_s[...] = jnp.zeros_like(acc_s)

    hn = hn_s[...]
    a = _dot(hn, w1_ref[...])
    b = _dot(hn, w3_ref[...])
    t = (a * jax.nn.sigmoid(a) * b).astype(BF16)
    acc_s[...] += _dot(t, w2_ref[...])

    @pl.when(j == n_ff - 1)
    def _():
        h2 = hin_s[...] + acc_s[...]
        gate = jax.nn.sigmoid(_dot(_rms(h2, np_ref[...]).astype(BF16), pg_ref[...]))
        h3 = h2 + gate * _dot(p_ref[...].astype(BF16), pp_ref[...])
        h_out[...] = h3
        xn_out[...] = _rms(h3, nn_ref[...])


def ffn_ple(h, y, p, nf, npl, nn, w1, w3, w2, pg, pp, *, tm, tf):
    n, d = h.shape
    n_ff = D_FF // tf
    row = lambda i, j: (i, 0)
    return pl.pallas_call(
        functools.partial(_ffn_kernel, n_ff=n_ff),
        grid=(n // tm, n_ff),
        in_specs=[
            pl.BlockSpec((tm, d), row), pl.BlockSpec((tm, d), row),
            pl.BlockSpec((tm, PLE_DIM), row),
            _full((1, d)), _full((1, d)), _full((1, d)),
            pl.BlockSpec((d, tf), lambda i, j: (0, j)),
            pl.BlockSpec((d, tf), lambda i, j: (0, j)),
            pl.BlockSpec((tf, d), lambda i, j: (j, 0)),
            _full((d, d)), _full((PLE_DIM, d)),
        ],
        out_specs=[pl.BlockSpec((tm, d), row), pl.BlockSpec((tm, d), row)],
        out_shape=[jax.ShapeDtypeStruct((n, d), F32)] * 2,
        scratch_shapes=[pltpu.VMEM((tm, d), F32), pltpu.VMEM((tm, d), BF16),
                        pltpu.VMEM((tm, d), F32)],
        compiler_params=_cparams(("parallel", "arbitrary")),
        name="ffn_ple",
    )(h, y, p, nf.reshape(1, d), npl.reshape(1, d), nn.reshape(1, d), w1, w3, w2, pg, pp)


def _mm_kernel(x_ref, w_ref, o_ref):
    o_ref[...] = _dot(x_ref[...].astype(BF16), w_ref[...])


def matmul_rows(x, w, *, tm):
    n, k = x.shape
    m = w.shape[1]
    return pl.pallas_call(
        _mm_kernel,
        grid=(n // tm,),
        in_specs=[pl.BlockSpec((tm, k), lambda i: (i, 0)), _full((k, m))],
        out_specs=pl.BlockSpec((tm, m), lambda i: (i, 0)),
        out_shape=jax.ShapeDtypeStruct((n, m), F32),
        compiler_params=_cparams(("parallel",)),
        name="matmul_rows",
    )(x, w)


def _glu_kernel(x_ref, wv_ref, wg_ref, o_ref):
    g = x_ref[...].astype(BF16)
    o_ref[...] = _dot(g, wv_ref[...]) * jax.nn.sigmoid(_dot(g, wg_ref[...]))


def glu_rows(x, wv, wg, *, tm):
    n, k = x.shape
    m = wv.shape[1]
    return pl.pallas_call(
        _glu_kernel,
        grid=(n // tm,),
        in_specs=[pl.BlockSpec((tm, k), lambda i: (i, 0)), _full((k, m)), _full((k, m))],
        out_specs=pl.BlockSpec((tm, m), lambda i: (i, 0)),
        out_shape=jax.ShapeDtypeStruct((n, m), F32),
        compiler_params=_cparams(("parallel",)),
        name="glu_rows",
    )(x, wv, wg)


def _rwkv_proj_kernel(*refs, bb, tl, with_v):
    if with_v:
        (x_ref, sh_ref, vf_ref, mu_ref, b0_ref, wrkv_ref, w1_ref, w2_ref, a1_ref, a2_ref,
         g1_ref, g2_ref, v1_ref, v2_ref,
         r_out, lw_out, k_out, v_out, a_out, g_out, carry) = refs
    else:
        (x_ref, sh_ref, mu_ref, b0_ref, wrkv_ref, w1_ref, w2_ref, a1_ref, a2_ref,
         g1_ref, g2_ref,
         r_out, lw_out, k_out, v_out, a_out, g_out, carry) = refs
    d = x_ref.shape[-1]
    m = bb * tl

    @pl.when(pl.program_id(1) == 0)
    def _():
        carry[...] = sh_ref[...]

    x3 = x_ref[...]
    x = x3.reshape(m, d)
    prev = jnp.broadcast_to(carry[...], (bb, tl, d)).reshape(m, d)
    rolled = pltpu.roll(x, 1, 0)
    t_in_tile = lax.broadcasted_iota(jnp.int32, (m, d), 0) & (tl - 1)
    x_prev = jnp.where(t_in_tile == 0, prev, rolled)
    carry[...] = x_ref[:, tl - 1:tl, :]

    xx = x_prev - x

    def mix(j):
        return (x + xx * mu_ref[j:j + 1, :]).astype(BF16)

    xr, xw, xk, xv, xa, xg = (mix(j) for j in range(6))
    r = _dot(xr, wrkv_ref[0])
    k = _dot(xk, wrkv_ref[1])
    v = _dot(xv, wrkv_ref[2])
    zw = b0_ref[0:1, :] + _dot(jnp.tanh(_dot(xw, w1_ref[...])).astype(BF16), w2_ref[...])
    lw = -jnp.exp(_log_sigmoid(zw) - 0.5)
    a = jax.nn.sigmoid(b0_ref[1:2, :] + _dot(_dot(xa, a1_ref[...]).astype(BF16), a2_ref[...]))
    g = _dot(jax.nn.sigmoid(_dot(xg, g1_ref[...])).astype(BF16), g2_ref[...])
    if with_v:
        vf = vf_ref[...].reshape(m, d)
        lam = jax.nn.sigmoid(b0_ref[2:3, :]
                             + _dot(_dot(xv, v1_ref[...]).astype(BF16), v2_ref[...]))
        v = v + (vf - v) * lam
    shp = (bb, tl, d)
    r_out[...] = r.reshape(shp)
    lw_out[...] = lw.reshape(shp)
    k_out[...] = k.reshape(shp)
    v_out[...] = v.reshape(shp)
    a_out[...] = a.reshape(shp)
    g_out[...] = g.reshape(shp)


def rwkv_proj(xn, shift, v_first, prm, *, bb, tl):
    bsz, L, d = xn.shape
    with_v = v_first is not None
    seq = pl.BlockSpec((bb, tl, d), lambda b, l: (b, l, 0))
    ins = [xn, shift.reshape(bsz, 1, d)]
    specs = [seq, pl.BlockSpec((bb, 1, d), lambda b, l: (b, 0, 0))]
    if with_v:
        ins.append(v_first)
        specs.append(seq)
    names = ["mu", "b0", "w_rkv", "w1", "w2", "a1", "a2", "g1", "g2"]
    if with_v:
        names += ["v1", "v2"]
    for nme in names:
        ins.append(prm[nme])
        specs.append(_full(prm[nme].shape))
    return pl.pallas_call(
        functools.partial(_rwkv_proj_kernel, bb=bb, tl=tl, with_v=with_v),
        grid=(bsz // bb, L // tl),
        in_specs=specs,
        out_specs=[seq] * 6,
        out_shape=[jax.ShapeDtypeStruct((bsz, L, d), F32)] * 6,
        scratch_shapes=[pltpu.VMEM((bb, 1, d), F32)],
        compiler_params=_cparams(("parallel", "arbitrary")),
        name="rwkv_proj",
    )(*ins)


def _unit_lower_inverse(n_mats, c, prec):
    row = lax.broadcasted_iota(jnp.int32, (c, c), 0)
    col = lax.broadcasted_iota(jnp.int32, (c, c), 1)
    eye = jnp.where(row == col, 1.0, 0.0).astype(F32)

    def same_block(shift):
        return (row >> shift) == (col >> shift)

    blk8 = same_block(3)
    n8 = [jnp.where(blk8, m, 0.0) for m in n_mats]
    n8_2 = [_dot(m, m, prec) for m in n8]
    t = [eye + m for m in n8]
    t = [ti + _dot(ti, m2, prec) for ti, m2 in zip(t, n8_2)]
    n8_4 = [_dot(m2, m2, prec) for m2 in n8_2]
    t = [ti + _dot(ti, m4, prec) for ti, m4 in zip(t, n8_4)]
    shift = 3
    while (1 << shift) < c:
        inner, outer = same_block(shift), same_block(shift + 1)
        off = [jnp.where(inner, 0.0, jnp.where(outer, m, 0.0)) for m in n_mats]
        left = [_dot(ti, oi, prec) for ti, oi in zip(t, off)]
        t = [ti + _dot(li, ti, prec) for ti, li in zip(t, left)]
        shift += 1
    return t


def _wkv_kernel(r_ref, lw_ref, k_ref, v_ref, a_ref, g_ref, hp_ref, s0_ref,
                z_out, s_out, s_scr, *, bb, tl, c, prec):
    n = RW_HEAD
    nheads = LANES // n
    nchunk = tl // c

    @pl.when(pl.program_id(2) == 0)
    def _():
        s_scr[...] = s0_ref[...]

    row = lax.broadcasted_iota(jnp.int32, (c, c), 0)
    col = lax.broadcasted_iota(jnp.int32, (c, c), 1)
    strict = row > col
    incl = row >= col
    tril = jnp.where(incl, 1.0, 0.0).astype(F32)
    k_k = hp_ref[0:1, :]
    k_a = hp_ref[1:2, :]
    r_k = hp_ref[2:3, :]
    ln_w = hp_ref[3:4, :]
    ln_b = hp_ref[4:5, :]

    chains = [(s, h) for s in range(bb) for h in range(nheads)]

    def chunk(ci, carry):
        rows = pl.ds(pl.multiple_of(ci * c, c), c)
        lw = [lw_ref[s, rows, :] for s in range(bb)]
        cum = [_dot(tril, x, HIGHEST) for x in lw]
        seq = []
        for s in range(bb):
            r = r_ref[s, rows, :]
            k = k_ref[s, rows, :]
            a = a_ref[s, rows, :]
            cum_end = cum[s][c - 1:c, :]
            e_neg = jnp.exp(-cum[s])
            e_end = jnp.exp(cum_end - cum[s])
            k_mod = k * (1.0 + (a - 1.0) * k_a)
            seq.append(dict(
                a=a, v=v_ref[s, rows, :], kk_raw=k * k_k,
                r_t=r * jnp.exp(cum[s]), e_neg=e_neg, e_end=e_end,
                e_prev=jnp.exp(cum[s] - lw[s]), g_end=jnp.exp(cum_end),
                k_t=k_mod * e_neg, k_end=k_mod * e_end, rk=r * k_mod * r_k))
        ch = []
        for s, h in chains:
            q = seq[s]
            sl = slice(h * n, (h + 1) * n)
            kk = q["kk_raw"][:, sl]
            kk = kk * lax.rsqrt(jnp.maximum(jnp.sum(kk * kk, -1, keepdims=True), 1e-24))
            b_h = kk * q["a"][:, sl]
            a_t = -kk * q["e_prev"][:, sl]
            r_t = q["r_t"][:, sl]
            ch.append(dict(
                a_t=a_t, r_t=r_t, ar=jnp.concatenate([a_t, r_t], axis=0),
                b_t=b_h * q["e_neg"][:, sl], b_end=b_h * q["e_end"][:, sl],
                k_t=q["k_t"][:, sl], k_end=q["k_end"][:, sl], v=q["v"][:, sl],
                g_end=q["g_end"][:, sl],
                bonus=jnp.sum(q["rk"][:, sl], -1, keepdims=True) * q["v"][:, sl]))
        ab = [_dot_nt(x["ar"], x["b_t"], prec) for x in ch]
        ak = [_dot_nt(x["ar"], x["k_t"], prec) for x in ch]
        t_inv = _unit_lower_inverse([jnp.where(strict, m[:c], 0.0) for m in ab], c, prec)
        s0 = [s_scr[s, h] for s, h in chains]
        rhs = [_dot_nt(x["a_t"], si, prec) for x, si in zip(ch, s0)]
        rhs = [ri + _dot(jnp.where(strict, m[:c], 0.0), x["v"], prec)
               for ri, m, x in zip(rhs, ak, ch)]
        o = [_dot_nt(x["r_t"], si, prec) for x, si in zip(ch, s0)]
        o = [oi + _dot(jnp.where(incl, m[c:], 0.0), x["v"], prec) for oi, m, x in zip(o, ak, ch)]
        s_new = [si * x["g_end"] + _dot_tn(x["v"], x["k_end"], prec) for x, si in zip(ch, s0)]
        u = [_dot(ti, ri, prec) for ti, ri in zip(t_inv, rhs)]
        o = [oi + _dot(jnp.where(incl, m[c:], 0.0), ui, prec) for oi, m, ui in zip(o, ab, u)]
        s_new = [sn + _dot_tn(ui, x["b_end"], prec) for sn, ui, x in zip(s_new, u, ch)]
        for (s, h), sn in zip(chains, s_new):
            s_scr[s, h] = sn
        normed = []
        for oi in o:
            mean = jnp.mean(oi, -1, keepdims=True)
            var = jnp.mean(jnp.square(oi - mean), -1, keepdims=True)
            normed.append((oi - mean) * lax.rsqrt(var + RW_GN_EPS))
        for s in range(bb):
            idx = [i for i, (si, _) in enumerate(chains) if si == s]
            o_all = jnp.concatenate([normed[i] for i in idx], axis=-1)
            bonus_all = jnp.concatenate([ch[i]["bonus"] for i in idx], axis=-1)
            z_out[s, rows, :] = (o_all * ln_w + ln_b + bonus_all) * g_ref[s, rows, :]
        return carry

    lax.fori_loop(0, nchunk, chunk, 0)

    @pl.when(pl.program_id(2) == pl.num_programs(2) - 1)
    def _():
        s_out[...] = s_scr[...]


def wkv_scan(r, lw, k, v, a, g, hp, s0, *, bb, tl, c, prec):
    bsz, L, d = r.shape
    nheads = LANES // RW_HEAD
    seq = pl.BlockSpec((bb, tl, LANES), lambda b, h, l: (b, l, h))
    st = pl.BlockSpec((bb, nheads, RW_HEAD, RW_HEAD), lambda b, h, l: (b, h, 0, 0))
    return pl.pallas_call(
        functools.partial(_wkv_kernel, bb=bb, tl=tl, c=c, prec=prec),
        grid=(bsz // bb, d // LANES, L // tl),
        in_specs=[seq] * 6 + [pl.BlockSpec((SUBLANES, LANES), lambda b, h, l: (0, h)), st],
        out_specs=[seq, st],
        out_shape=[jax.ShapeDtypeStruct((bsz, L, d), F32),
                   jax.ShapeDtypeStruct(s0.shape, F32)],
        scratch_shapes=[pltpu.VMEM((bb, nheads, RW_HEAD, RW_HEAD), F32)],
        compiler_params=_cparams(("parallel", "parallel", "arbitrary")),
        name="wkv_scan",
    )(r, lw, k, v, a, g, hp, s0)


def rwkv_layer(xn, shift, wkv, v_first, prm, cfg):
    r, lw, k, v, a, g = rwkv_proj(xn, shift, v_first, prm, bb=cfg["proj_bb"], tl=cfg["proj_tl"])
    z, wkv_new = wkv_scan(r, lw, k, v, a, g, prm["hp"], wkv,
                          bb=cfg["scan_bb"], tl=cfg["wkv_tl"], c=cfg["wkv_c"], prec=SCAN_PREC)
    bsz, L, d = xn.shape
    y = matmul_rows(z.reshape(bsz * L, d), prm["w_o"], tm=cfg["tm"]).reshape(bsz, L, d)
    return y, xn[:, -1], wkv_new, (v if v_first is None else v_first)


def _prep_rwkv(mu, w0, w1, w2, a0, a1, a2, g1, g2, k_k, k_a, r_k, w_rkv, w_o, lnx_w, lnx_b,
               v_lora=None):
    d = D_MODEL
    zero = jnp.zeros((d,), F32)
    prm = dict(mu=mu, w_rkv=w_rkv.astype(BF16), w_o=w_o.astype(BF16),
               w1=w1.astype(BF16), w2=w2.astype(BF16), a1=a1.astype(BF16), a2=a2.astype(BF16),
               g1=g1.astype(BF16), g2=g2.astype(BF16))
    v0 = zero
    if v_lora is not None:
        v0, v1, v2 = v_lora
        prm["v1"] = v1.astype(BF16)
        prm["v2"] = v2.astype(BF16)
    prm["b0"] = jnp.stack([w0, a0, v0] + [zero] * 5)
    prm["hp"] = jnp.stack([k_k, k_a, r_k.reshape(d), lnx_w, lnx_b] + [zero] * 3)
    return prm


S5_CH = 8
S5_CL = S5_LANES // S5_CH


def _s5_kernel(u_ref, w_ref, cm_ref, ar_ref, ai_ref, d_ref, h0r_ref, h0i_ref,
               g_out, hr_out, hi_out, hs, hc, *, bsz, tr):
    i = pl.program_id(1)
    cl = S5_CL

    @pl.when(i == 0)
    def _():
        hc[:, :cl] = h0r_ref[...]
        hc[:, cl:] = h0i_ref[...]

    u = u_ref[...]
    hs[...] = _dot(u, w_ref[...], HIGHEST)
    ar = jnp.broadcast_to(ar_ref[...], (bsz, cl))
    ai = jnp.broadcast_to(ai_ref[...], (bsz, cl))

    def step(t, carry):
        hr, hi = carry
        rows = pl.ds(pl.multiple_of(t * bsz, bsz), bsz)
        nr = ar * hr - ai * hi + hs[rows, :cl]
        ni = ar * hi + ai * hr + hs[rows, cl:]
        hs[rows, :cl] = nr
        hs[rows, cl:] = ni
        return nr, ni

    hr, hi = lax.fori_loop(0, tr // bsz, step, (hc[:, :cl], hc[:, cl:]))
    hc[:, :cl] = hr
    hc[:, cl:] = hi
    y = _dot(hs[...], cm_ref[...], HIGHEST) + d_ref[...] * u
    g_out[...] = jax.nn.gelu(y)

    @pl.when(i == pl.num_programs(1) - 1)
    def _():
        hr_out[...] = hr
        hi_out[...] = hi


def s5_scan(u_tm, h0r, h0i, prm, *, bsz, tr):
    rows, d = u_tm.shape
    cl = S5_CL
    st = pl.BlockSpec((bsz, cl), lambda c, i: (0, c))
    return pl.pallas_call(
        functools.partial(_s5_kernel, bsz=bsz, tr=tr),
        grid=(S5_CH, rows // tr),
        in_specs=[
            pl.BlockSpec((tr, LANES), lambda c, i: (i, c)),
            pl.BlockSpec((None, LANES, 2 * cl), lambda c, i: (c, 0, 0)),
            pl.BlockSpec((None, 2 * cl, LANES), lambda c, i: (c, 0, 0)),
            pl.BlockSpec((None, 1, cl), lambda c, i: (c, 0, 0)),
            pl.BlockSpec((None, 1, cl), lambda c, i: (c, 0, 0)),
            pl.BlockSpec((1, LANES), lambda c, i: (0, c)),
            st, st,
        ],
        out_specs=[pl.BlockSpec((tr, LANES), lambda c, i: (i, c)), st, st],
        out_shape=[jax.ShapeDtypeStruct((rows, d), F32),
                   jax.ShapeDtypeStruct((bsz, S5_LANES), F32),
                   jax.ShapeDtypeStruct((bsz, S5_LANES), F32)],
        scratch_shapes=[pltpu.VMEM((tr, 2 * cl), F32), pltpu.VMEM((bsz, 2 * cl), F32)],
        compiler_params=_cparams(("parallel", "arbitrary")),
        name="s5_scan",
    )(u_tm, prm["w_in"], prm["c_out"], prm["ab_re"], prm["ab_im"], prm["d"], h0r, h0i)


def _prep_s5(a_re, a_im, log_dt, b_re, b_im, c_re, c_im, d_skip, glu_v, glu_g):
    dt = jnp.exp(log_dt.astype(F32))[:, None]
    lr, li = a_re.astype(F32), a_im.astype(F32)
    mag = jnp.exp(lr * dt)
    ab_re, ab_im = mag * jnp.cos(li * dt), mag * jnp.sin(li * dt)
    den = lr * lr + li * li
    q_re = ((ab_re - 1.0) * lr + ab_im * li) / den
    q_im = (ab_im * lr - (ab_re - 1.0) * li) / den
    bb_re = q_re[..., None] * b_re - q_im[..., None] * b_im
    bb_im = q_re[..., None] * b_im + q_im[..., None] * b_re
    gl = S5_GROUPS // S5_CH
    eye = jnp.eye(gl, dtype=F32)

    def in_blocks(bb):
        t = bb.reshape(S5_CH, gl, S5_STATE, S5_GROUP)
        return jnp.einsum('cgph,gk->cghkp', t, eye).reshape(S5_CH, LANES, S5_CL)

    def out_blocks(cc):
        t = cc.reshape(S5_CH, gl, S5_GROUP, S5_STATE)
        return jnp.einsum('cghp,gk->ckpgh', t, eye).reshape(S5_CH, S5_CL, LANES)

    return dict(
        w_in=jnp.concatenate([in_blocks(bb_re), in_blocks(bb_im)], axis=-1),
        c_out=jnp.concatenate([out_blocks(c_re.astype(F32)), -out_blocks(c_im.astype(F32))],
                              axis=1),
        ab_re=ab_re.reshape(S5_CH, 1, S5_CL), ab_im=ab_im.reshape(S5_CH, 1, S5_CL),
        d=d_skip.reshape(1, D_MODEL),
        glu_v=glu_v.astype(BF16), glu_g=glu_g.astype(BF16))


def s5_layer(xn, h_re0, h_im0, prm, cfg):
    bsz, L, d = xn.shape
    u_tm = jnp.swapaxes(xn, 0, 1).reshape(L * bsz, d)
    g, hr, hi = s5_scan(u_tm, h_re0.reshape(bsz, S5_LANES), h_im0.reshape(bsz, S5_LANES), prm,
                        bsz=bsz, tr=cfg["s5_tr"])
    y_tm = glu_rows(g, prm["glu_v"], prm["glu_g"], tm=cfg["tm"])
    y = jnp.swapaxes(y_tm.reshape(L, bsz, d), 0, 1)
    return (y, hr.reshape(bsz, S5_GROUPS, S5_STATE), hi.reshape(bsz, S5_GROUPS, S5_STATE))


def _mamba_in_kernel(x_ref, cp_ref, wz_ref, wx_ref, wdt_ref, cw_ref, cb_ref, dtb_ref,
                     z_out, xs_out, b_out, c_out, dt_out, nc_out, carry, *, bb, tl):
    d = x_ref.shape[-1]
    m = bb * tl
    cd = M_CONV_DIM
    l = pl.program_id(1)

    @pl.when(l == 0)
    def _():
        carry[:, :SUBLANES - (M_CONV - 1), :] = jnp.zeros(
            (bb, SUBLANES - (M_CONV - 1), cd), F32)
        carry[:, SUBLANES - (M_CONV - 1):, :] = cp_ref[...]

    x = x_ref[...].reshape(m, d).astype(BF16)
    z_out[...] = _dot(x, wz_ref[...]).reshape(bb, tl, M_INNER)
    dt_raw = _dot(x, wdt_ref[...])[:, :M_HEADS]
    dt_out[...] = _softplus(dt_raw + dtb_ref[...]).reshape(bb, tl, M_HEADS)
    xbc = _dot(x, wx_ref[...]).reshape(bb, tl, cd)
    full = jnp.concatenate([carry[...], xbc], axis=1).reshape(bb * (tl + SUBLANES), cd)
    conv = cb_ref[...] + xbc.reshape(m, cd) * cw_ref[M_CONV - 1:M_CONV, :]
    for j in range(1, M_CONV):
        sh = pltpu.roll(full, j, 0).reshape(bb, tl + SUBLANES, cd)[:, SUBLANES:, :]
        conv = conv + sh.reshape(m, cd) * cw_ref[M_CONV - 1 - j:M_CONV - j, :]
    act = conv * jax.nn.sigmoid(conv)
    xs_out[...] = act[:, :M_INNER].reshape(bb, tl, M_INNER)
    nbc = M_GROUPS * M_STATE
    b_out[...] = act[:, M_INNER:M_INNER + nbc].reshape(bb, tl, nbc)
    c_out[...] = act[:, M_INNER + nbc:].reshape(bb, tl, nbc)
    full3 = full.reshape(bb, tl + SUBLANES, cd)
    carry[...] = full3[:, tl:, :]

    @pl.when(l == pl.num_programs(1) - 1)
    def _():
        nc_out[...] = full3[:, tl + SUBLANES - (M_CONV - 1):, :]


def mamba_in(xn, conv_prev, prm, *, bb, tl):
    bsz, L, d = xn.shape
    cd = M_CONV_DIM
    nbc = M_GROUPS * M_STATE
    seq = lambda w: pl.BlockSpec((bb, tl, w), lambda b, l: (b, l, 0))
    cps = pl.BlockSpec((bb, M_CONV - 1, cd), lambda b, l: (b, 0, 0))
    return pl.pallas_call(
        functools.partial(_mamba_in_kernel, bb=bb, tl=tl),
        grid=(bsz // bb, L // tl),
        in_specs=[seq(d), cps, _full((d, M_INNER)), _full((d, cd)), _full((d, LANES)),
                  _full((M_CONV, cd)), _full((1, cd)), _full((1, M_HEADS))],
        out_specs=[seq(M_INNER), seq(M_INNER), seq(nbc), seq(nbc), seq(M_HEADS), cps],
        out_shape=[jax.ShapeDtypeStruct((bsz, L, M_INNER), F32),
                   jax.ShapeDtypeStruct((bsz, L, M_INNER), F32),
                   jax.ShapeDtypeStruct((bsz, L, nbc), F32),
                   jax.ShapeDtypeStruct((bsz, L, nbc), F32),
                   jax.ShapeDtypeStruct((bsz, L, M_HEADS), F32),
                   jax.ShapeDtypeStruct((bsz, M_CONV - 1, cd), F32)],
        scratch_shapes=[pltpu.VMEM((bb, SUBLANES, cd), F32)],
        compiler_params=_cparams(("parallel", "arbitrary")),
        name="mamba_in",
    )(xn, conv_prev, prm["wz"], prm["wx"], prm["wdt"], prm["conv_w"], prm["conv_b"],
      prm["dt_bias"])


def _ssd_kernel(xs_ref, b_ref, c_ref, z_ref, dt_ref, hv_ref, ex_ref, nw_ref, s0_ref,
                y_out, s_out, s_scr, *, bb, tl, c):
    p, n = M_HEADDIM, M_STATE
    hpg = M_HEADS // M_GROUPS
    gw = M_INNER // M_GROUPS
    nchunk = tl // c

    @pl.when(pl.program_id(1) == 0)
    def _():
        s_scr[...] = s0_ref[...]

    row = lax.broadcasted_iota(jnp.int32, (c, c), 0)
    col = lax.broadcasted_iota(jnp.int32, (c, c), 1)
    incl = row >= col
    tril = jnp.where(incl, 1.0, 0.0).astype(F32)
    triu = jnp.where(row <= col, 1.0, 0.0).astype(F32)
    a_neg = hv_ref[0:1, :]
    d_exp = nw_ref[1:2, :]
    norm_w = nw_ref[0:1, :]
    expand = ex_ref[...]

    def chunk(i, carry):
        s = i // nchunk
        ci = i - s * nchunk
        rows = pl.ds(pl.multiple_of(ci * c, c), c)
        xs = xs_ref[s, rows, :]
        bm = b_ref[s, rows, :]
        cm = c_ref[s, rows, :]
        dt = dt_ref[s, rows, :]
        a = dt * a_neg
        cum = _dot(tril, a, HIGHEST)
        cum_t = _dot_tn(a, triu, HIGHEST)
        cum_end = cum[c - 1:c, :]
        e_end = jnp.exp(cum_end)
        dt_x = _dot(dt, expand, HIGHEST)
        dec_x = _dot(jnp.exp(cum_end - cum), expand, HIGHEST)
        ecum_x = _dot(jnp.exp(cum), expand, HIGHEST)
        xdt = xs * dt_x
        xend = xdt * dec_x
        yd, yo = [], []
        for gi in range(M_GROUPS):
            b_g = bm[:, gi * n:(gi + 1) * n]
            c_g = cm[:, gi * n:(gi + 1) * n]
            cb = _dot_nt(c_g, b_g)
            for hh in range(hpg):
                h = gi * hpg + hh
                sl = slice(h * p, (h + 1) * p)
                seg = cum[:, h:h + 1] - cum_t[h:h + 1, :]
                lmat = jnp.where(incl, jnp.exp(jnp.where(incl, seg, 0.0)), 0.0)
                s_h = s_scr[s, h]
                yd.append(_dot(cb * lmat, xdt[:, sl]))
                yo.append(_dot_nt(c_g, s_h))
                s_scr[s, h] = s_h * e_end[:, h:h + 1] + _dot_tn(xend[:, sl], b_g)
        y = (jnp.concatenate(yd, axis=-1) + jnp.concatenate(yo, axis=-1) * ecum_x
             + d_exp * xs)
        zz = z_ref[s, rows, :]
        y = y * (zz * jax.nn.sigmoid(zz))
        outs = []
        for gi in range(M_GROUPS):
            yg = y[:, gi * gw:(gi + 1) * gw]
            outs.append(yg * lax.rsqrt(jnp.mean(yg * yg, -1, keepdims=True) + EPS))
        y_out[s, rows, :] = jnp.concatenate(outs, axis=-1) * norm_w
        return carry

    lax.fori_loop(0, bb * nchunk, chunk, 0)

    @pl.when(pl.program_id(1) == pl.num_programs(1) - 1)
    def _():
        s_out[...] = s_scr[...]


def ssd_scan(xs, bm, cm, z, dt, prm, s0, *, bb, tl, c):
    bsz, L, _ = xs.shape
    nbc = M_GROUPS * M_STATE
    seq = lambda w: pl.BlockSpec((bb, tl, w), lambda b, l: (b, l, 0))
    st = pl.BlockSpec((bb, M_HEADS, M_HEADDIM, M_STATE), lambda b, l: (b, 0, 0, 0))
    return pl.pallas_call(
        functools.partial(_ssd_kernel, bb=bb, tl=tl, c=c),
        grid=(bsz // bb, L // tl),
        in_specs=[seq(M_INNER), seq(nbc), seq(nbc), seq(M_INNER), seq(M_HEADS),
                  _full((SUBLANES, M_HEADS)), _full((M_HEADS, M_INNER)),
                  _full((SUBLANES, M_INNER)), st],
        out_specs=[seq(M_INNER), st],
        out_shape=[jax.ShapeDtypeStruct((bsz, L, M_INNER), F32),
                   jax.ShapeDtypeStruct(s0.shape, F32)],
        scratch_shapes=[pltpu.VMEM((bb, M_HEADS, M_HEADDIM, M_STATE), F32)],
        compiler_params=_cparams(("parallel", "arbitrary")),
        name="ssd_scan",
    )(xs, bm, cm, z, dt, prm["hv"], prm["expand"], prm["nw"], s0)


def _prep_mamba(in_proj, conv_w, conv_b, dt_bias, a_log, d_skip, norm_w, out_proj):
    d = D_MODEL
    wdt = jnp.zeros((d, LANES), F32).at[:, :M_HEADS].set(in_proj[:, M_INNER + M_CONV_DIM:])
    expand = jnp.repeat(jnp.eye(M_HEADS, dtype=F32), M_HEADDIM, axis=1)
    zrow_h = jnp.zeros((M_HEADS,), F32)
    zrow_i = jnp.zeros((M_INNER,), F32)
    return dict(
        wz=in_proj[:, :M_INNER].astype(BF16),
        wx=in_proj[:, M_INNER:M_INNER + M_CONV_DIM].astype(BF16),
        wdt=wdt.astype(BF16),
        conv_w=conv_w, conv_b=conv_b.reshape(1, M_CONV_DIM),
        dt_bias=dt_bias.reshape(1, M_HEADS),
        hv=jnp.stack([-jnp.exp(a_log.astype(F32))] + [zrow_h] * 7),
        expand=expand,
        nw=jnp.stack([norm_w, jnp.repeat(d_skip, M_HEADDIM)] + [zrow_i] * 6),
        out_proj=out_proj.astype(BF16))


def mamba_layer(xn, conv_prev, ssm_prev, prm, cfg):
    bsz, L, d = xn.shape
    z, xs, bm, cm, dt, new_conv = mamba_in(xn, conv_prev, prm, bb=cfg["mamba_bb"],
                                           tl=cfg["mamba_tl"])
    y, ssm_new = ssd_scan(xs, bm, cm, z, dt, prm, ssm_prev,
                          bb=cfg["ssd_bb"], tl=cfg["ssd_tl"], c=cfg["ssd_c"])
    out = matmul_rows(y.reshape(bsz * L, M_INNER), prm["out_proj"], tm=cfg["tm"])
    return out.reshape(bsz, L, d), new_conv, ssm_new


def _config(bsz, L):
    if L >= 512:
        return dict(tm=512, tf=D_FF // 2, proj_bb=1, proj_tl=256,
                    scan_bb=min(bsz, 8), wkv_tl=512, wkv_c=64, mamba_bb=1, mamba_tl=128,
                    ssd_bb=1, ssd_tl=256, ssd_c=M_CHUNK, s5_tr=512)
    rows = bsz * L
    tm = min(512, rows)
    return dict(tm=tm, tf=D_FF // 2, proj_bb=min(bsz, 256 // L), proj_tl=L,
                scan_bb=min(bsz, 8), wkv_tl=L, wkv_c=L, mamba_bb=min(bsz, 128 // L), mamba_tl=L,
                ssd_bb=min(bsz, 4), ssd_tl=L, ssd_c=min(M_CHUNK, L), s5_tr=rows)


def _trunk(x, p, states, layer_prms, shared, cfg):
    bsz, L, d = x.shape
    n = bsz * L
    tm = cfg["tm"]
    xn = rmsnorm_rows(x.reshape(n, d), shared["norm_mix"][0], tm).reshape(bsz, L, d)
    h = x.reshape(n, d)
    v_first = None
    new_states = []
    for i in range(DEPTH):
        st_a, st_b = states[2 * i], states[2 * i + 1]
        prm = layer_prms[i]
        kind = i % 3
        if kind == 0:
            y, n_a, n_b, v_first = rwkv_layer(xn, st_a, st_b, v_first, prm, cfg)
        elif kind == 1:
            y, n_a, n_b = s5_layer(xn, st_a, st_b, prm, cfg)
        else:
            y, n_a, n_b = mamba_layer(xn, st_a, st_b, prm, cfg)
        new_states += [n_a, n_b]
        nxt = shared["norm_mix"][i + 1] if i + 1 < DEPTH else shared["final_norm"]
        h, xn_flat = ffn_ple(h, y.reshape(n, d), p[i].reshape(n, PLE_DIM),
                             shared["norm_ffn"][i], shared["norm_ple"][i], nxt,
                             shared["ffn_w1"][i], shared["ffn_w3"][i], shared["ffn_w2"][i],
                             shared["ple_gate"][i], shared["ple_proj"][i], tm=tm, tf=cfg["tf"])
        xn = xn_flat.reshape(bsz, L, d)
    return xn, new_states


def kernel(x_prompt, x_sample, p_prompt, p_sample, state_l0_shift, state_l0_wkv, state_l1_s5_re, state_l1_s5_im, state_l2_conv, state_l2_ssm, state_l3_shift, state_l3_wkv, l0_mu, l0_w0, l0_w1, l0_w2, l0_a0, l0_a1, l0_a2, l0_g1, l0_g2, l0_k_k, l0_k_a, l0_r_k, l0_w_rkv, l0_w_o, l0_lnx_w, l0_lnx_b, l1_a_re, l1_a_im, l1_log_dt, l1_b_re, l1_b_im, l1_c_re, l1_c_im, l1_d, l1_glu_v, l1_glu_g, l2_in_proj, l2_conv_w, l2_conv_b, l2_dt_bias, l2_a_log, l2_d, l2_norm_w, l2_out_proj, l3_mu, l3_w0, l3_w1, l3_w2, l3_a0, l3_a1, l3_a2, l3_g1, l3_g2, l3_k_k, l3_k_a, l3_r_k, l3_w_rkv, l3_w_o, l3_lnx_w, l3_lnx_b, l3_v0, l3_v1, l3_v2, norm_mix, norm_ffn, norm_ple, ffn_w1, ffn_w3, ffn_w2, ple_proj, ple_gate, final_norm):
    layer_prms = (
        _prep_rwkv(l0_mu, l0_w0, l0_w1, l0_w2, l0_a0, l0_a1, l0_a2, l0_g1, l0_g2,
                   l0_k_k, l0_k_a, l0_r_k, l0_w_rkv, l0_w_o, l0_lnx_w, l0_lnx_b),
        _prep_s5(l1_a_re, l1_a_im, l1_log_dt, l1_b_re, l1_b_im, l1_c_re, l1_c_im,
                 l1_d, l1_glu_v, l1_glu_g),
        _prep_mamba(l2_in_proj, l2_conv_w, l2_conv_b, l2_dt_bias, l2_a_log, l2_d,
                    l2_norm_w, l2_out_proj),
        _prep_rwkv(l3_mu, l3_w0, l3_w1, l3_w2, l3_a0, l3_a1, l3_a2, l3_g1, l3_g2,
                   l3_k_k, l3_k_a, l3_r_k, l3_w_rkv, l3_w_o, l3_lnx_w, l3_lnx_b,
                   (l3_v0, l3_v1, l3_v2)),
    )
    shared = dict(norm_mix=norm_mix, norm_ffn=norm_ffn, norm_ple=norm_ple,
                  final_norm=final_norm,
                  ffn_w1=ffn_w1.astype(BF16), ffn_w3=ffn_w3.astype(BF16),
                  ffn_w2=ffn_w2.astype(BF16), ple_gate=ple_gate.astype(BF16),
                  ple_proj=ple_proj.astype(BF16))
    sample_states = [state_l0_shift, state_l0_wkv, state_l1_s5_re, state_l1_s5_im,
                     state_l2_conv, state_l2_ssm, state_l3_shift, state_l3_wkv]
    bp = x_prompt.shape[0]
    prompt_states = [jnp.zeros((bp,) + s.shape[1:], F32) for s in sample_states]
    y_prompt, new_p = _trunk(x_prompt, p_prompt, prompt_states, layer_prms, shared,
                             _config(*x_prompt.shape[:2]))
    y_sample, new_s = _trunk(x_sample, p_sample, sample_states, layer_prms, shared,
                             _config(*x_sample.shape[:2]))
    return (y_prompt, y_sample, *new_p, *new_s)
```

```python
import functools
import math

import jax
import jax.numpy as jnp
from jax import lax
from jax.experimental import pallas as pl
from jax.experimental.pallas import tpu as pltpu

F32 = jnp.float32
BF16 = jnp.bfloat16
HIGHEST = lax.Precision.HIGHEST

D_MODEL = 1024
DEPTH = 4
PLE_DIM = 256
EPS = 1e-6
D_FF = 2816
RW_HEAD = 64
RW_HEADS = D_MODEL // RW_HEAD
RW_GN_EPS = 64e-5
S5_GROUP = 16
S5_GROUPS = D_MODEL // S5_GROUP
S5_STATE = 64
S5_LANES = S5_GROUPS * S5_STATE
M_INNER = 2 * D_MODEL
M_HEADDIM = 64
M_HEADS = M_INNER // M_HEADDIM
M_STATE = 128
M_GROUPS = 4
M_CONV = 4
M_CHUNK = 64
M_CONV_DIM = M_INNER + 2 * M_GROUPS * M_STATE

LANES = 128
SUBLANES = 8
VMEM_LIMIT = 56 * 1024 * 1024

SCAN_PREC = None


def _cparams(sem):
    return pltpu.CompilerParams(dimension_semantics=sem, vmem_limit_bytes=VMEM_LIMIT)


def _operands(a, b, prec):
    if prec is None:
        return a.astype(BF16), b.astype(BF16)
    return a, b


def _dot(a, b, prec=None):
    a, b = _operands(a, b, prec)
    return jnp.dot(a, b, preferred_element_type=F32, precision=prec)


def _dot_nt(a, b, prec=None):
    a, b = _operands(a, b, prec)
    return lax.dot_general(a, b, (((1,), (1,)), ((), ())),
                           preferred_element_type=F32, precision=prec)


def _dot_tn(a, b, prec=None):
    a, b = _operands(a, b, prec)
    return lax.dot_general(a, b, (((0,), (0,)), ((), ())),
                           preferred_element_type=F32, precision=prec)


def _rms(x, g):
    return x * lax.rsqrt(jnp.mean(x * x, axis=-1, keepdims=True) + EPS) * g


def _log_sigmoid(z):
    return -(jnp.maximum(-z, 0.0) + jnp.log1p(jnp.exp(-jnp.abs(z))))


def _softplus(z):
    return jnp.maximum(z, 0.0) + jnp.log1p(jnp.exp(-jnp.abs(z)))


def _full(shape):
    n = len(shape)
    return pl.BlockSpec(shape, lambda *_: (0,) * n)


def _rmsnorm_kernel(x_ref, g_ref, o_ref):
    o_ref[...] = _rms(x_ref[...], g_ref[...])


def rmsnorm_rows(x, g, tm):
    n, d = x.shape
    return pl.pallas_call(
        _rmsnorm_kernel,
        grid=(n // tm,),
        in_specs=[pl.BlockSpec((tm, d), lambda i: (i, 0)), _full((1, d))],
        out_specs=pl.BlockSpec((tm, d), lambda i: (i, 0)),
        out_shape=jax.ShapeDtypeStruct((n, d), F32),
        compiler_params=_cparams(("parallel",)),
        name="rmsnorm",
    )(x, g.reshape(1, d))


def _ffn_kernel(h_ref, y_ref, p_ref, nf_ref, np_ref, nn_ref, w1_ref, w3_ref, w2_ref,
                pg_ref, pp_ref, h_out, xn_out, hin_s, hn_s, acc_s, *, n_ff):
    j = pl.program_id(1)

    @pl.when(j == 0)
    def _():
        hin = h_ref[...] + y_ref[...]
        hin_s[...] = hin
        hn_s[...] = _rms(hin, nf_ref[...]).astype(BF16)
        acc_s[...] = jnp.zeros_like(acc_s)

    hn = hn_s[...]
    a = _dot(hn, w1_ref[...])
    b = _dot(hn, w3_ref[...])
    t = (a * jax.nn.sigmoid(a) * b).astype(BF16)
    acc_s[...] += _dot(t, w2_ref[...])

    @pl.when(j == n_ff - 1)
    def _():
        h2 = hin_s[...] + acc_s[...]
        gate = jax.nn.sigmoid(_dot(_rms(h2, np_ref[...]).astype(BF16), pg_ref[...]))
        h3 = h2 + gate * _dot(p_ref[...].astype(BF16), pp_ref[...])
        h_out[...] = h3
        xn_out[...] = _rms(h3, nn_ref[...])


def ffn_ple(h, y, p, nf, npl, nn, w1, w3, w2, pg, pp, *, tm, tf):
    n, d = h.shape
    n_ff = D_FF // tf
    row = lambda i, j: (i, 0)
    return pl.pallas_call(
        functools.partial(_ffn_kernel, n_ff=n_ff),
        grid=(n // tm, n_ff),
        in_specs=[
            pl.BlockSpec((tm, d), row), pl.BlockSpec((tm, d), row),
            pl.BlockSpec((tm, PLE_DIM), row),
            _full((1, d)), _full((1, d)), _full((1, d)),
            pl.BlockSpec((d, tf), lambda i, j: (0, j)),
            pl.BlockSpec((d, tf), lambda i, j: (0, j)),
            pl.BlockSpec((tf, d), lambda i, j: (j, 0)),
            _full((d, d)), _full((PLE_DIM, d)),
        ],
        out_specs=[pl.BlockSpec((tm, d), row), pl.BlockSpec((tm, d), row)],
        out_shape=[jax.ShapeDtypeStruct((n, d), F32)] * 2,
        scratch_shapes=[pltpu.VMEM((tm, d), F32), pltpu.VMEM((tm, d), BF16),
                        pltpu.VMEM((tm, d), F32)],
        compiler_params=_cparams(("parallel", "arbitrary")),
        name="ffn_ple",
    )(h, y, p, nf.reshape(1, d), npl.reshape(1, d), nn.reshape(1, d), w1, w3, w2, pg, pp)


def _mm_kernel(x_ref, w_ref, o_ref):
    o_ref[...] = _dot(x_ref[...].astype(BF16), w_ref[...])


def matmul_rows(x, w, *, tm):
    n, k = x.shape
    m = w.shape[1]
    return pl.pallas_call(
        _mm_kernel,
        grid=(n // tm,),
        in_specs=[pl.BlockSpec((tm, k), lambda i: (i, 0)), _full((k, m))],
        out_specs=pl.BlockSpec((tm, m), lambda i: (i, 0)),
        out_shape=jax.ShapeDtypeStruct((n, m), F32),
        compiler_params=_cparams(("parallel",)),
        name="matmul_rows",
    )(x, w)


def _glu_kernel(x_ref, wv_ref, wg_ref, o_ref):
    g = x_ref[...].astype(BF16)
    o_ref[...] = _dot(g, wv_ref[...]) * jax.nn.sigmoid(_dot(g, wg_ref[...]))


def glu_rows(x, wv, wg, *, tm):
    n, k = x.shape
    m = wv.shape[1]
    return pl.pallas_call(
        _glu_kernel,
        grid=(n // tm,),
        in_specs=[pl.BlockSpec((tm, k), lambda i: (i, 0)), _full((k, m)), _full((k, m))],
        out_specs=pl.BlockSpec((tm, m), lambda i: (i, 0)),
        out_shape=jax.ShapeDtypeStruct((n, m), F32),
        compiler_params=_cparams(("parallel",)),
        name="glu_rows",
    )(x, wv, wg)


def _rwkv_proj_kernel(*refs, bb, tl, with_v):
    if with_v:
        (x_ref, sh_ref, vf_ref, mu_ref, b0_ref, wrkv_ref, w1_ref, w2_ref, a1_ref, a2_ref,
         g1_ref, g2_ref, v1_ref, v2_ref,
         r_out, lw_out, k_out, v_out, a_out, g_out, carry) = refs
    else:
        (x_ref, sh_ref, mu_ref, b0_ref, wrkv_ref, w1_ref, w2_ref, a1_ref, a2_ref,
         g1_ref, g2_ref,
         r_out, lw_out, k_out, v_out, a_out, g_out, carry) = refs
    d = x_ref.shape[-1]
    m = bb * tl

    @pl.when(pl.program_id(1) == 0)
    def _():
        carry[...] = sh_ref[...]

    x3 = x_ref[...]
    x = x3.reshape(m, d)
    prev = jnp.broadcast_to(carry[...], (bb, tl, d)).reshape(m, d)
    rolled = pltpu.roll(x, 1, 0)
    t_in_tile = lax.broadcasted_iota(jnp.int32, (m, d), 0) & (tl - 1)
    x_prev = jnp.where(t_in_tile == 0, prev, rolled)
    carry[...] = x_ref[:, tl - 1:tl, :]

    xx = x_prev - x

    def mix(j):
        return (x + xx * mu_ref[j:j + 1, :]).astype(BF16)

    xr, xw, xk, xv, xa, xg = (mix(j) for j in range(6))
    r = _dot(xr, wrkv_ref[0])
    k = _dot(xk, wrkv_ref[1])
    v = _dot(xv, wrkv_ref[2])
    zw = b0_ref[0:1, :] + _dot(jnp.tanh(_dot(xw, w1_ref[...])).astype(BF16), w2_ref[...])
    lw = -jnp.exp(_log_sigmoid(zw) - 0.5)
    a = jax.nn.sigmoid(b0_ref[1:2, :] + _dot(_dot(xa, a1_ref[...]).astype(BF16), a2_ref[...]))
    g = _dot(jax.nn.sigmoid(_dot(xg, g1_ref[...])).astype(BF16), g2_ref[...])
    if with_v:
        vf = vf_ref[...].reshape(m, d)
        lam = jax.nn.sigmoid(b0_ref[2:3, :]
                             + _dot(_dot(xv, v1_ref[...]).astype(BF16), v2_ref[...]))
        v = v + (vf - v) * lam
    shp = (bb, tl, d)
    r_out[...] = r.reshape(shp)
    lw_out[...] = lw.reshape(shp)
    k_out[...] = k.reshape(shp)
    v_out[...] = v.reshape(shp)
    a_out[...] = a.reshape(shp)
    g_out[...] = g.reshape(shp)


def rwkv_proj(xn, shift, v_first, prm, *, bb, tl):
    bsz, L, d = xn.shape
    with_v = v_first is not None
    seq = pl.BlockSpec((bb, tl, d), lambda b, l: (b, l, 0))
    ins = [xn, shift.reshape(bsz, 1, d)]
    specs = [seq, pl.BlockSpec((bb, 1, d), lambda b, l: (b, 0, 0))]
    if with_v:
        ins.append(v_first)
        specs.append(seq)
    names = ["mu", "b0", "w_rkv", "w1", "w2", "a1", "a2", "g1", "g2"]
    if with_v:
        names += ["v1", "v2"]
    for nme in names:
        ins.append(prm[nme])
        specs.append(_full(prm[nme].shape))
    return pl.pallas_call(
        functools.partial(_rwkv_proj_kernel, bb=bb, tl=tl, with_v=with_v),
        grid=(bsz // bb, L // tl),
        in_specs=specs,
        out_specs=[seq] * 6,
        out_shape=[jax.ShapeDtypeStruct((bsz, L, d), F32)] * 6,
        scratch_shapes=[pltpu.VMEM((bb, 1, d), F32)],
        compiler_params=_cparams(("parallel", "arbitrary")),
        name="rwkv_proj",
    )(*ins)


def _unit_lower_inverse(n_mats, c, prec):
    row = lax.broadcasted_iota(jnp.int32, (c, c), 0)
    col = lax.broadcasted_iota(jnp.int32, (c, c), 1)
    eye = jnp.where(row == col, 1.0, 0.0).astype(F32)

    def same_block(shift):
        return (row >> shift) == (col >> shift)

    blk8 = same_block(3)
    n8 = [jnp.where(blk8, m, 0.0) for m in n_mats]
    n8_2 = [_dot(m, m, prec) for m in n8]
    t = [eye + m for m in n8]
    t = [ti + _dot(ti, m2, prec) for ti, m2 in zip(t, n8_2)]
    n8_4 = [_dot(m2, m2, prec) for m2 in n8_2]
    t = [ti + _dot(ti, m4, prec) for ti, m4 in zip(t, n8_4)]
    shift = 3
    while (1 << shift) < c:
        inner, outer = same_block(shift), same_block(shift + 1)
        off = [jnp.where(inner, 0.0, jnp.where(outer, m, 0.0)) for m in n_mats]
        left = [_dot(ti, oi, prec) for ti, oi in zip(t, off)]
        t = [ti + _dot(li, ti, prec) for ti, li in zip(t, left)]
        shift += 1
    return t


def _wkv_kernel(r_ref, lw_ref, k_ref, v_ref, a_ref, g_ref, hp_ref, s0_ref,
                z_out, s_out, s_scr, *, bb, tl, c, prec):
    n = RW_HEAD
    nheads = LANES // n
    nchunk = tl // c

    @pl.when(pl.program_id(2) == 0)
    def _():
        s_scr[...] = s0_ref[...]

    row = lax.broadcasted_iota(jnp.int32, (c, c), 0)
    col = lax.broadcasted_iota(jnp.int32, (c, c), 1)
    tril = jnp.where(row >= col, 1.0, 0.0).astype(F32)
    row2 = lax.broadcasted_iota(jnp.int32, (2 * c, c), 0)
    col2 = lax.broadcasted_iota(jnp.int32, (2 * c, c), 1)
    mask2 = col2 <= jnp.where(row2 < c, row2 - 1, row2 - c)
    k_k = hp_ref[0:1, :]
    k_a = hp_ref[1:2, :]
    r_k = hp_ref[2:3, :]
    ln_w = hp_ref[3:4, :]
    ln_b = hp_ref[4:5, :]

    chains = [(s, h) for s in range(bb) for h in range(nheads)]

    def chunk(ci, carry):
        rows = pl.ds(pl.multiple_of(ci * c, c), c)
        lw = [lw_ref[s, rows, :] for s in range(bb)]
        cum = [_dot(tril, x, HIGHEST) for x in lw]
        seq = []
        for s in range(bb):
            r = r_ref[s, rows, :]
            k = k_ref[s, rows, :]
            a = a_ref[s, rows, :]
            cum_end = cum[s][c - 1:c, :]
            e_neg = jnp.exp(-cum[s])
            e_end = jnp.exp(cum_end - cum[s])
            k_mod = k * (1.0 + (a - 1.0) * k_a)
            seq.append(dict(
                a=a, v=v_ref[s, rows, :], kk_raw=k * k_k,
                r_t=r * jnp.exp(cum[s]), e_neg=e_neg, e_end=e_end,
                e_prev=jnp.exp(cum[s] - lw[s]), g_end=jnp.exp(cum_end),
                k_t=k_mod * e_neg, k_end=k_mod * e_end, rk=r * k_mod * r_k))
        ch = []
        for s, h in chains:
            q = seq[s]
            sl = slice(h * n, (h + 1) * n)
            kk = q["kk_raw"][:, sl]
            kk = kk * lax.rsqrt(jnp.maximum(jnp.sum(kk * kk, -1, keepdims=True), 1e-24))
            b_h = kk * q["a"][:, sl]
            a_t = -kk * q["e_prev"][:, sl]
            ch.append(dict(
                ar=jnp.concatenate([a_t, q["r_t"][:, sl]], axis=0),
                b_t=b_h * q["e_neg"][:, sl], k_t=q["k_t"][:, sl],
                kb_end=jnp.concatenate([q["k_end"][:, sl], b_h * q["e_end"][:, sl]], axis=0),
                v=q["v"][:, sl], g_end=q["g_end"][:, sl],
                bonus=jnp.sum(q["rk"][:, sl], -1, keepdims=True) * q["v"][:, sl]))
        ab = [jnp.where(mask2, _dot_nt(x["ar"], x["b_t"], prec), 0.0) for x in ch]
        ak = [jnp.where(mask2, _dot_nt(x["ar"], x["k_t"], prec), 0.0) for x in ch]
        t_inv = _unit_lower_inverse([m[:c] for m in ab], c, prec)
        s0 = [s_scr[s, h] for s, h in chains]
        x1 = [_dot_nt(x["ar"], si, prec) for x, si in zip(ch, s0)]
        x1 = [xi + _dot(m, x["v"], prec) for xi, m, x in zip(x1, ak, ch)]
        u = [_dot(ti, xi[:c], prec) for ti, xi in zip(t_inv, x1)]
        o = [xi[c:] + _dot(m[c:], ui, prec) for xi, m, ui in zip(x1, ab, u)]
        s_new = [si * x["g_end"] + _dot_tn(jnp.concatenate([x["v"], ui], axis=0), x["kb_end"], prec)
                 for x, si, ui in zip(ch, s0, u)]
        for (s, h), sn in zip(chains, s_new):
            s_scr[s, h] = sn
        normed = []
        for oi in o:
            mean = jnp.mean(oi, -1, keepdims=True)
            var = jnp.mean(jnp.square(oi - mean), -1, keepdims=True)
            normed.append((oi - mean) * lax.rsqrt(var + RW_GN_EPS))
        for s in range(bb):
            idx = [i for i, (si, _) in enumerate(chains) if si == s]
            o_all = jnp.concatenate([normed[i] for i in idx], axis=-1)
            bonus_all = jnp.concatenate([ch[i]["bonus"] for i in idx], axis=-1)
            z_out[s, rows, :] = (o_all * ln_w + ln_b + bonus_all) * g_ref[s, rows, :]
        return carry

    lax.fori_loop(0, nchunk, chunk, 0)

    @pl.when(pl.program_id(2) == pl.num_programs(2) - 1)
    def _():
        s_out[...] = s_scr[...]


def wkv_scan(r, lw, k, v, a, g, hp, s0, *, bb, tl, c, prec):
    bsz, L, d = r.shape
    nheads = LANES // RW_HEAD
    seq = pl.BlockSpec((bb, tl, LANES), lambda b, h, l: (b, l, h))
    st = pl.BlockSpec((bb, nheads, RW_HEAD, RW_HEAD), lambda b, h, l: (b, h, 0, 0))
    return pl.pallas_call(
        functools.partial(_wkv_kernel, bb=bb, tl=tl, c=c, prec=prec),
        grid=(bsz // bb, d // LANES, L // tl),
        in_specs=[seq] * 6 + [pl.BlockSpec((SUBLANES, LANES), lambda b, h, l: (0, h)), st],
        out_specs=[seq, st],
        out_shape=[jax.ShapeDtypeStruct((bsz, L, d), F32),
                   jax.ShapeDtypeStruct(s0.shape, F32)],
        scratch_shapes=[pltpu.VMEM((bb, nheads, RW_HEAD, RW_HEAD), F32)],
        compiler_params=_cparams(("parallel", "parallel", "arbitrary")),
        name="wkv_scan",
    )(r, lw, k, v, a, g, hp, s0)


def rwkv_layer(xn, shift, wkv, v_first, prm, cfg):
    r, lw, k, v, a, g = rwkv_proj(xn, shift, v_first, prm, bb=cfg["proj_bb"], tl=cfg["proj_tl"])
    z, wkv_new = wkv_scan(r, lw, k, v, a, g, prm["hp"], wkv,
                          bb=cfg["scan_bb"], tl=cfg["wkv_tl"], c=cfg["wkv_c"], prec=SCAN_PREC)
    bsz, L, d = xn.shape
    y = matmul_rows(z.reshape(bsz * L, d), prm["w_o"], tm=cfg["tm"]).reshape(bsz, L, d)
    return y, xn[:, -1], wkv_new, (v if v_first is None else v_first)


def _prep_rwkv(mu, w0, w1, w2, a0, a1, a2, g1, g2, k_k, k_a, r_k, w_rkv, w_o, lnx_w, lnx_b,
               v_lora=None):
    d = D_MODEL
    zero = jnp.zeros((d,), F32)
    prm = dict(mu=mu, w_rkv=w_rkv.astype(BF16), w_o=w_o.astype(BF16),
               w1=w1.astype(BF16), w2=w2.astype(BF16), a1=a1.astype(BF16), a2=a2.astype(BF16),
               g1=g1.astype(BF16), g2=g2.astype(BF16))
    v0 = zero
    if v_lora is not None:
        v0, v1, v2 = v_lora
        prm["v1"] = v1.astype(BF16)
        prm["v2"] = v2.astype(BF16)
    prm["b0"] = jnp.stack([w0, a0, v0] + [zero] * 5)
    prm["hp"] = jnp.stack([k_k, k_a, r_k.reshape(d), lnx_w, lnx_b] + [zero] * 3)
    return prm


S5_CH = 8
S5_CL = S5_LANES // S5_CH


def _s5_kernel(u_ref, w_ref, cm_ref, ar_ref, ai_ref, d_ref, h0r_ref, h0i_ref,
               g_out, hr_out, hi_out, hs, hc, *, bsz, tr):
    i = pl.program_id(1)
    cl = S5_CL

    @pl.when(i == 0)
    def _():
        hc[:, :cl] = h0r_ref[...]
        hc[:, cl:] = h0i_ref[...]

    u = u_ref[...]
    hs[...] = _dot(u, w_ref[...])
    ar = jnp.broadcast_to(ar_ref[...], (bsz, cl))
    ai = jnp.broadcast_to(ai_ref[...], (bsz, cl))

    def step(t, carry):
        hr, hi = carry
        rows = pl.ds(pl.multiple_of(t * bsz, bsz), bsz)
        nr = ar * hr - ai * hi + hs[rows, :cl]
        ni = ar * hi + ai * hr + hs[rows, cl:]
        hs[rows, :cl] = nr
        hs[rows, cl:] = ni
        return nr, ni

    nstep = tr // bsz
    hr, hi = lax.fori_loop(0, nstep, step, (hc[:, :cl], hc[:, cl:]),
                           unroll=min(nstep, SUBLANES))
    hc[:, :cl] = hr
    hc[:, cl:] = hi
    y = _dot(hs[...], cm_ref[...]) + d_ref[...] * u
    g_out[...] = jax.nn.gelu(y)

    @pl.when(i == pl.num_programs(1) - 1)
    def _():
        hr_out[...] = hr
        hi_out[...] = hi


def s5_scan(u_tm, h0r, h0i, prm, *, bsz, tr):
    rows, d = u_tm.shape
    cl = S5_CL
    st = pl.BlockSpec((bsz, cl), lambda c, i: (0, c))
    return pl.pallas_call(
        functools.partial(_s5_kernel, bsz=bsz, tr=tr),
        grid=(S5_CH, rows // tr),
        in_specs=[
            pl.BlockSpec((tr, LANES), lambda c, i: (i, c)),
            pl.BlockSpec((None, LANES, 2 * cl), lambda c, i: (c, 0, 0)),
            pl.BlockSpec((None, 2 * cl, LANES), lambda c, i: (c, 0, 0)),
            pl.BlockSpec((None, 1, cl), lambda c, i: (c, 0, 0)),
            pl.BlockSpec((None, 1, cl), lambda c, i: (c, 0, 0)),
            pl.BlockSpec((1, LANES), lambda c, i: (0, c)),
            st, st,
        ],
        out_specs=[pl.BlockSpec((tr, LANES), lambda c, i: (i, c)), st, st],
        out_shape=[jax.ShapeDtypeStruct((rows, d), F32),
                   jax.ShapeDtypeStruct((bsz, S5_LANES), F32),
                   jax.ShapeDtypeStruct((bsz, S5_LANES), F32)],
        scratch_shapes=[pltpu.VMEM((tr, 2 * cl), F32), pltpu.VMEM((bsz, 2 * cl), F32)],
        compiler_params=_cparams(("parallel", "arbitrary")),
        name="s5_scan",
    )(u_tm, prm["w_in"], prm["c_out"], prm["ab_re"], prm["ab_im"], prm["d"], h0r, h0i)


def _prep_s5(a_re, a_im, log_dt, b_re, b_im, c_re, c_im, d_skip, glu_v, glu_g):
    dt = jnp.exp(log_dt.astype(F32))[:, None]
    lr, li = a_re.astype(F32), a_im.astype(F32)
    mag = jnp.exp(lr * dt)
    ab_re, ab_im = mag * jnp.cos(li * dt), mag * jnp.sin(li * dt)
    den = lr * lr + li * li
    q_re = ((ab_re - 1.0) * lr + ab_im * li) / den
    q_im = (ab_im * lr - (ab_re - 1.0) * li) / den
    bb_re = q_re[..., None] * b_re - q_im[..., None] * b_im
    bb_im = q_re[..., None] * b_im + q_im[..., None] * b_re
    gl = S5_GROUPS // S5_CH
    eye = jnp.eye(gl, dtype=F32)

    def in_blocks(bb):
        t = bb.reshape(S5_CH, gl, S5_STATE, S5_GROUP)
        return jnp.einsum('cgph,gk->cghkp', t, eye).reshape(S5_CH, LANES, S5_CL)

    def out_blocks(cc):
        t = cc.reshape(S5_CH, gl, S5_GROUP, S5_STATE)
        return jnp.einsum('cghp,gk->ckpgh', t, eye).reshape(S5_CH, S5_CL, LANES)

    return dict(
        w_in=jnp.concatenate([in_blocks(bb_re), in_blocks(bb_im)], axis=-1),
        c_out=jnp.concatenate([out_blocks(c_re.astype(F32)), -out_blocks(c_im.astype(F32))],
                              axis=1),
        ab_re=ab_re.reshape(S5_CH, 1, S5_CL), ab_im=ab_im.reshape(S5_CH, 1, S5_CL),
        d=d_skip.reshape(1, D_MODEL),
        glu_v=glu_v.astype(BF16), glu_g=glu_g.astype(BF16))


def s5_layer(xn, h_re0, h_im0, prm, cfg):
    bsz, L, d = xn.shape
    u_tm = jnp.swapaxes(xn, 0, 1).reshape(L * bsz, d)
    g, hr, hi = s5_scan(u_tm, h_re0.reshape(bsz, S5_LANES), h_im0.reshape(bsz, S5_LANES), prm,
                        bsz=bsz, tr=cfg["s5_tr"])
    y_tm = glu_rows(g, prm["glu_v"], prm["glu_g"], tm=cfg["tm"])
    y = jnp.swapaxes(y_tm.reshape(L, bsz, d), 0, 1)
    return (y, hr.reshape(bsz, S5_GROUPS, S5_STATE), hi.reshape(bsz, S5_GROUPS, S5_STATE))


def _mamba_in_kernel(x_ref, cp_ref, wz_ref, wx_ref, wdt_ref, cw_ref, cb_ref, dtb_ref,
                     z_out, xs_out, b_out, c_out, dt_out, nc_out, carry, *, bb, tl):
    d = x_ref.shape[-1]
    m = bb * tl
    cd = M_CONV_DIM
    l = pl.program_id(1)

    @pl.when(l == 0)
    def _():
        carry[:, :SUBLANES - (M_CONV - 1), :] = jnp.zeros(
            (bb, SUBLANES - (M_CONV - 1), cd), F32)
        carry[:, SUBLANES - (M_CONV - 1):, :] = cp_ref[...]

    x = x_ref[...].reshape(m, d).astype(BF16)
    z_out[...] = _dot(x, wz_ref[...]).reshape(bb, tl, M_INNER)
    dt_raw = _dot(x, wdt_ref[...])[:, :M_HEADS]
    dt_out[...] = _softplus(dt_raw + dtb_ref[...]).reshape(bb, tl, M_HEADS)
    xbc = _dot(x, wx_ref[...]).reshape(bb, tl, cd)
    full = jnp.concatenate([carry[...], xbc], axis=1).reshape(bb * (tl + SUBLANES), cd)
    conv = cb_ref[...] + xbc.reshape(m, cd) * cw_ref[M_CONV - 1:M_CONV, :]
    for j in range(1, M_CONV):
        sh = pltpu.roll(full, j, 0).reshape(bb, tl + SUBLANES, cd)[:, SUBLANES:, :]
        conv = conv + sh.reshape(m, cd) * cw_ref[M_CONV - 1 - j:M_CONV - j, :]
    act = conv * jax.nn.sigmoid(conv)
    xs_out[...] = act[:, :M_INNER].reshape(bb, tl, M_INNER)
    nbc = M_GROUPS * M_STATE
    b_out[...] = act[:, M_INNER:M_INNER + nbc].reshape(bb, tl, nbc)
    c_out[...] = act[:, M_INNER + nbc:].reshape(bb, tl, nbc)
    full3 = full.reshape(bb, tl + SUBLANES, cd)
    carry[...] = full3[:, tl:, :]

    @pl.when(l == pl.num_programs(1) - 1)
    def _():
        nc_out[...] = full3[:, tl + SUBLANES - (M_CONV - 1):, :]


def mamba_in(xn, conv_prev, prm, *, bb, tl):
    bsz, L, d = xn.shape
    cd = M_CONV_DIM
    nbc = M_GROUPS * M_STATE
    seq = lambda w: pl.BlockSpec((bb, tl, w), lambda b, l: (b, l, 0))
    cps = pl.BlockSpec((bb, M_CONV - 1, cd), lambda b, l: (b, 0, 0))
    return pl.pallas_call(
        functools.partial(_mamba_in_kernel, bb=bb, tl=tl),
        grid=(bsz // bb, L // tl),
        in_specs=[seq(d), cps, _full((d, M_INNER)), _full((d, cd)), _full((d, LANES)),
                  _full((M_CONV, cd)), _full((1, cd)), _full((1, M_HEADS))],
        out_specs=[seq(M_INNER), seq(M_INNER), seq(nbc), seq(nbc), seq(M_HEADS), cps],
        out_shape=[jax.ShapeDtypeStruct((bsz, L, M_INNER), F32),
                   jax.ShapeDtypeStruct((bsz, L, M_INNER), F32),
                   jax.ShapeDtypeStruct((bsz, L, nbc), F32),
                   jax.ShapeDtypeStruct((bsz, L, nbc), F32),
                   jax.ShapeDtypeStruct((bsz, L, M_HEADS), F32),
                   jax.ShapeDtypeStruct((bsz, M_CONV - 1, cd), F32)],
        scratch_shapes=[pltpu.VMEM((bb, SUBLANES, cd), F32)],
        compiler_params=_cparams(("parallel", "arbitrary")),
        name="mamba_in",
    )(xn, conv_prev, prm["wz"], prm["wx"], prm["wdt"], prm["conv_w"], prm["conv_b"],
      prm["dt_bias"])


def _ssd_kernel(xs_ref, b_ref, c_ref, z_ref, dt_ref, hv_ref, ex_ref, nw_ref, s0_ref,
                y_out, s_out, s_scr, *, bb, tl, c):
    p, n = M_HEADDIM, M_STATE
    hpg = M_HEADS // M_GROUPS
    gw = M_INNER // M_GROUPS
    nchunk = tl // c

    @pl.when(pl.program_id(1) == 0)
    def _():
        s_scr[...] = s0_ref[...]

    row = lax.broadcasted_iota(jnp.int32, (c, c), 0)
    col = lax.broadcasted_iota(jnp.int32, (c, c), 1)
    incl = row >= col
    tril = jnp.where(incl, 1.0, 0.0).astype(F32)
    triu = jnp.where(row <= col, 1.0, 0.0).astype(F32)
    a_neg = hv_ref[0:1, :]
    d_exp = nw_ref[1:2, :]
    norm_w = nw_ref[0:1, :]
    expand = ex_ref[...]

    def chunk(i, carry):
        s = i // nchunk
        ci = i - s * nchunk
        rows = pl.ds(pl.multiple_of(ci * c, c), c)
        xs = xs_ref[s, rows, :]
        bm = b_ref[s, rows, :]
        cm = c_ref[s, rows, :]
        dt = dt_ref[s, rows, :]
        a = dt * a_neg
        cum = _dot(tril, a, HIGHEST)
        cum_t = _dot_tn(a, triu, HIGHEST)
        cum_end = cum[c - 1:c, :]
        e_end = jnp.exp(cum_end)
        dt_x = _dot(dt, expand)
        dec_x = _dot(jnp.exp(cum_end - cum), expand)
        ecum_x = _dot(jnp.exp(cum), expand)
        xdt = xs * dt_x
        xend = xdt * dec_x
        yd, yo = [], []
        for gi in range(M_GROUPS):
            b_g = bm[:, gi * n:(gi + 1) * n]
            c_g = cm[:, gi * n:(gi + 1) * n]
            cb = _dot_nt(c_g, b_g)
            for hh in range(hpg):
                h = gi * hpg + hh
                sl = slice(h * p, (h + 1) * p)
                seg = cum[:, h:h + 1] - cum_t[h:h + 1, :]
                lmat = jnp.where(incl, jnp.exp(jnp.where(incl, seg, 0.0)), 0.0)
                s_h = s_scr[s, h]
                yd.append(_dot(cb * lmat, xdt[:, sl]))
                yo.append(_dot_nt(c_g, s_h))
                s_scr[s, h] = s_h * e_end[:, h:h + 1] + _dot_tn(xend[:, sl], b_g)
        y = (jnp.concatenate(yd, axis=-1) + jnp.concatenate(yo, axis=-1) * ecum_x
             + d_exp * xs)
        zz = z_ref[s, rows, :]
        y = y * (zz * jax.nn.sigmoid(zz))
        outs = []
        for gi in range(M_GROUPS):
            yg = y[:, gi * gw:(gi + 1) * gw]
            outs.append(yg * lax.rsqrt(jnp.mean(yg * yg, -1, keepdims=True) + EPS))
        y_out[s, rows, :] = jnp.concatenate(outs, axis=-1) * norm_w
        return carry

    lax.fori_loop(0, bb * nchunk, chunk, 0)

    @pl.when(pl.program_id(1) == pl.num_programs(1) - 1)
    def _():
        s_out[...] = s_scr[...]


def ssd_scan(xs, bm, cm, z, dt, prm, s0, *, bb, tl, c):
    bsz, L, _ = xs.shape
    nbc = M_GROUPS * M_STATE
    seq = lambda w: pl.BlockSpec((bb, tl, w), lambda b, l: (b, l, 0))
    st = pl.BlockSpec((bb, M_HEADS, M_HEADDIM, M_STATE), lambda b, l: (b, 0, 0, 0))
    return pl.pallas_call(
        functools.partial(_ssd_kernel, bb=bb, tl=tl, c=c),
        grid=(bsz // bb, L // tl),
        in_specs=[seq(M_INNER), seq(nbc), seq(nbc), seq(M_INNER), seq(M_HEADS),
                  _full((SUBLANES, M_HEADS)), _full((M_HEADS, M_INNER)),
                  _full((SUBLANES, M_INNER)), st],
        out_specs=[seq(M_INNER), st],
        out_shape=[jax.ShapeDtypeStruct((bsz, L, M_INNER), F32),
                   jax.ShapeDtypeStruct(s0.shape, F32)],
        scratch_shapes=[pltpu.VMEM((bb, M_HEADS, M_HEADDIM, M_STATE), F32)],
        compiler_params=_cparams(("parallel", "arbitrary")),
        name="ssd_scan",
    )(xs, bm, cm, z, dt, prm["hv"], prm["expand"], prm["nw"], s0)


def _prep_mamba(in_proj, conv_w, conv_b, dt_bias, a_log, d_skip, norm_w, out_proj):
    d = D_MODEL
    wdt = jnp.zeros((d, LANES), F32).at[:, :M_HEADS].set(in_proj[:, M_INNER + M_CONV_DIM:])
    expand = jnp.repeat(jnp.eye(M_HEADS, dtype=F32), M_HEADDIM, axis=1)
    zrow_h = jnp.zeros((M_HEADS,), F32)
    zrow_i = jnp.zeros((M_INNER,), F32)
    return dict(
        wz=in_proj[:, :M_INNER].astype(BF16),
        wx=in_proj[:, M_INNER:M_INNER + M_CONV_DIM].astype(BF16),
        wdt=wdt.astype(BF16),
        conv_w=conv_w, conv_b=conv_b.reshape(1, M_CONV_DIM),
        dt_bias=dt_bias.reshape(1, M_HEADS),
        hv=jnp.stack([-jnp.exp(a_log.astype(F32))] + [zrow_h] * 7),
        expand=expand,
        nw=jnp.stack([norm_w, jnp.repeat(d_skip, M_HEADDIM)] + [zrow_i] * 6),
        out_proj=out_proj.astype(BF16))


def mamba_layer(xn, conv_prev, ssm_prev, prm, cfg):
    bsz, L, d = xn.shape
    z, xs, bm, cm, dt, new_conv = mamba_in(xn, conv_prev, prm, bb=cfg["mamba_bb"],
                                           tl=cfg["mamba_tl"])
    y, ssm_new = ssd_scan(xs, bm, cm, z, dt, prm, ssm_prev,
                          bb=cfg["ssd_bb"], tl=cfg["ssd_tl"], c=cfg["ssd_c"])
    out = matmul_rows(y.reshape(bsz * L, M_INNER), prm["out_proj"], tm=cfg["tm"])
    return out.reshape(bsz, L, d), new_conv, ssm_new


def _config(bsz, L):
    if L >= 512:
        return dict(tm=512, tf=D_FF // 2, proj_bb=1, proj_tl=256,
                    scan_bb=min(bsz, 8), wkv_tl=512, wkv_c=64, mamba_bb=1, mamba_tl=128,
                    ssd_bb=1, ssd_tl=256, ssd_c=M_CHUNK, s5_tr=512)
    rows = bsz * L
    tm = min(512, rows)
    return dict(tm=tm, tf=D_FF // 2, proj_bb=min(bsz, 256 // L), proj_tl=L,
                scan_bb=min(bsz, 16), wkv_tl=L, wkv_c=L, mamba_bb=min(bsz, 128 // L), mamba_tl=L,
                ssd_bb=min(bsz, 4), ssd_tl=L, ssd_c=min(M_CHUNK, L), s5_tr=rows)


def _trunk(x, p, states, layer_prms, shared, cfg):
    bsz, L, d = x.shape
    n = bsz * L
    tm = cfg["tm"]
    xn = rmsnorm_rows(x.reshape(n, d), shared["norm_mix"][0], tm).reshape(bsz, L, d)
    h = x.reshape(n, d)
    v_first = None
    new_states = []
    for i in range(DEPTH):
        st_a, st_b = states[2 * i], states[2 * i + 1]
        prm = layer_prms[i]
        kind = i % 3
        if kind == 0:
            y, n_a, n_b, v_first = rwkv_layer(xn, st_a, st_b, v_first, prm, cfg)
        elif kind == 1:
            y, n_a, n_b = s5_layer(xn, st_a, st_b, prm, cfg)
        else:
            y, n_a, n_b = mamba_layer(xn, st_a, st_b, prm, cfg)
        new_states += [n_a, n_b]
        nxt = shared["norm_mix"][i + 1] if i + 1 < DEPTH else shared["final_norm"]
        h, xn_flat = ffn_ple(h, y.reshape(n, d), p[i].reshape(n, PLE_DIM),
                             shared["norm_ffn"][i], shared["norm_ple"][i], nxt,
                             shared["ffn_w1"][i], shared["ffn_w3"][i], shared["ffn_w2"][i],
                             shared["ple_gate"][i], shared["ple_proj"][i], tm=tm, tf=cfg["tf"])
        xn = xn_flat.reshape(bsz, L, d)
    return xn, new_states


def kernel(x_prompt, x_sample, p_prompt, p_sample, state_l0_shift, state_l0_wkv, state_l1_s5_re, state_l1_s5_im, state_l2_conv, state_l2_ssm, state_l3_shift, state_l3_wkv, l0_mu, l0_w0, l0_w1, l0_w2, l0_a0, l0_a1, l0_a2, l0_g1, l0_g2, l0_k_k, l0_k_a, l0_r_k, l0_w_rkv, l0_w_o, l0_lnx_w, l0_lnx_b, l1_a_re, l1_a_im, l1_log_dt, l1_b_re, l1_b_im, l1_c_re, l1_c_im, l1_d, l1_glu_v, l1_glu_g, l2_in_proj, l2_conv_w, l2_conv_b, l2_dt_bias, l2_a_log, l2_d, l2_norm_w, l2_out_proj, l3_mu, l3_w0, l3_w1, l3_w2, l3_a0, l3_a1, l3_a2, l3_g1, l3_g2, l3_k_k, l3_k_a, l3_r_k, l3_w_rkv, l3_w_o, l3_lnx_w, l3_lnx_b, l3_v0, l3_v1, l3_v2, norm_mix, norm_ffn, norm_ple, ffn_w1, ffn_w3, ffn_w2, ple_proj, ple_gate, final_norm):
    layer_prms = (
        _prep_rwkv(l0_mu, l0_w0, l0_w1, l0_w2, l0_a0, l0_a1, l0_a2, l0_g1, l0_g2,
                   l0_k_k, l0_k_a, l0_r_k, l0_w_rkv, l0_w_o, l0_lnx_w, l0_lnx_b),
        _prep_s5(l1_a_re, l1_a_im, l1_log_dt, l1_b_re, l1_b_im, l1_c_re, l1_c_im,
                 l1_d, l1_glu_v, l1_glu_g),
        _prep_mamba(l2_in_proj, l2_conv_w, l2_conv_b, l2_dt_bias, l2_a_log, l2_d,
                    l2_norm_w, l2_out_proj),
        _prep_rwkv(l3_mu, l3_w0, l3_w1, l3_w2, l3_a0, l3_a1, l3_a2, l3_g1, l3_g2,
                   l3_k_k, l3_k_a, l3_r_k, l3_w_rkv, l3_w_o, l3_lnx_w, l3_lnx_b,
                   (l3_v0, l3_v1, l3_v2)),
    )
    shared = dict(norm_mix=norm_mix, norm_ffn=norm_ffn, norm_ple=norm_ple,
                  final_norm=final_norm,
                  ffn_w1=ffn_w1.astype(BF16), ffn_w3=ffn_w3.astype(BF16),
                  ffn_w2=ffn_w2.astype(BF16), ple_gate=ple_gate.astype(BF16),
                  ple_proj=ple_proj.astype(BF16))
    sample_states = [state_l0_shift, state_l0_wkv, state_l1_s5_re, state_l1_s5_im,
                     state_l2_conv, state_l2_ssm, state_l3_shift, state_l3_wkv]
    bp = x_prompt.shape[0]
    prompt_states = [jnp.zeros((bp,) + s.shape[1:], F32) for s in sample_states]
    y_prompt, new_p = _trunk(x_prompt, p_prompt, prompt_states, layer_prms, shared,
                             _config(*x_prompt.shape[:2]))
    y_sample, new_s = _trunk(x_sample, p_sample, sample_states, layer_prms, shared,
                             _config(*x_sample.shape[:2]))
    return (y_prompt, y_sample, *new_p, *new_s)
```

```python
import functools
import math

import jax
import jax.numpy as jnp
from jax import lax
from jax.experimental import pallas as pl
from jax.experimental.pallas import tpu as pltpu

F32 = jnp.float32
BF16 = jnp.bfloat16
HIGHEST = lax.Precision.HIGHEST

D_MODEL = 1024
DEPTH = 4
PLE_DIM = 256
EPS = 1e-6
D_FF = 2816
RW_HEAD = 64
RW_HEADS = D_MODEL // RW_HEAD
RW_GN_EPS = 64e-5
S5_GROUP = 16
S5_GROUPS = D_MODEL // S5_GROUP
S5_STATE = 64
S5_LANES = S5_GROUPS * S5_STATE
M_INNER = 2 * D_MODEL
M_HEADDIM = 64
M_HEADS = M_INNER // M_HEADDIM
M_STATE = 128
M_GROUPS = 4
M_CONV = 4
M_CHUNK = 64
M_CONV_DIM = M_INNER + 2 * M_GROUPS * M_STATE

LANES = 128
MXU_TILE = 256
SUBLANES = 8
VMEM_LIMIT = 56 * 1024 * 1024

SCAN_PREC = None


def _cparams(sem):
    return pltpu.CompilerParams(dimension_semantics=sem, vmem_limit_bytes=VMEM_LIMIT)


def _operands(a, b, prec):
    if prec is None:
        return a.astype(BF16), b.astype(BF16)
    return a, b


def _dot(a, b, prec=None):
    a, b = _operands(a, b, prec)
    return jnp.dot(a, b, preferred_element_type=F32, precision=prec)


def _dot_nt(a, b, prec=None):
    a, b = _operands(a, b, prec)
    return lax.dot_general(a, b, (((1,), (1,)), ((), ())),
                           preferred_element_type=F32, precision=prec)


def _dot_tn(a, b, prec=None):
    a, b = _operands(a, b, prec)
    return lax.dot_general(a, b, (((0,), (0,)), ((), ())),
                           preferred_element_type=F32, precision=prec)


def _rms(x, g):
    return x * lax.rsqrt(jnp.mean(x * x, axis=-1, keepdims=True) + EPS) * g


def _log_sigmoid(z):
    return -(jnp.maximum(-z, 0.0) + jnp.log1p(jnp.exp(-jnp.abs(z))))


def _softplus(z):
    return jnp.maximum(z, 0.0) + jnp.log1p(jnp.exp(-jnp.abs(z)))


def _full(shape):
    n = len(shape)
    return pl.BlockSpec(shape, lambda *_: (0,) * n)


def _resident(shape):
    n = len(shape)
    return pl.BlockSpec(shape, lambda *_: (0,) * n, pipeline_mode=pl.Buffered(1))


def _rmsnorm_kernel(x_ref, g_ref, o_ref):
    o_ref[...] = _rms(x_ref[...], g_ref[...])


def rmsnorm_rows(x, g, tm):
    n, d = x.shape
    return pl.pallas_call(
        _rmsnorm_kernel,
        grid=(n // tm,),
        in_specs=[pl.BlockSpec((tm, d), lambda i: (i, 0)), _full((1, d))],
        out_specs=pl.BlockSpec((tm, d), lambda i: (i, 0)),
        out_shape=jax.ShapeDtypeStruct((n, d), F32),
        compiler_params=_cparams(("parallel",)),
        name="rmsnorm",
    )(x, g.reshape(1, d))


def _ffn_kernel(h_ref, y_ref, p_ref, nf_ref, np_ref, nn_ref, w1_ref, w3_ref, w2_ref,
                pg_ref, pp_ref, h_out, xn_out):
    hin = h_ref[...] + y_ref[...]
    hn = _rms(hin, nf_ref[...]).astype(BF16)
    acc = None
    for ci in range(D_FF // MXU_TILE):
        cs = slice(ci * MXU_TILE, (ci + 1) * MXU_TILE)
        a = _dot(hn, w1_ref[:, cs])
        b = _dot(hn, w3_ref[:, cs])
        t = (a * jax.nn.sigmoid(a) * b).astype(BF16)
        part = _dot(t, w2_ref[cs, :])
        acc = part if acc is None else acc + part
    h2 = hin + acc
    gate = jax.nn.sigmoid(_dot(_rms(h2, np_ref[...]).astype(BF16), pg_ref[...]))
    h3 = h2 + gate * _dot(p_ref[...].astype(BF16), pp_ref[...])
    h_out[...] = h3
    xn_out[...] = _rms(h3, nn_ref[...])


def ffn_ple(h, y, p, nf, npl, nn, w1, w3, w2, pg, pp, *, tm):
    n, d = h.shape
    row = lambda i: (i, 0)
    return pl.pallas_call(
        _ffn_kernel,
        grid=(n // tm,),
        in_specs=[
            pl.BlockSpec((tm, d), row), pl.BlockSpec((tm, d), row),
            pl.BlockSpec((tm, PLE_DIM), row),
            _full((1, d)), _full((1, d)), _full((1, d)),
            _resident((d, D_FF)), _resident((d, D_FF)), _resident((D_FF, d)),
            _resident((d, d)), _resident((PLE_DIM, d)),
        ],
        out_specs=[pl.BlockSpec((tm, d), row), pl.BlockSpec((tm, d), row)],
        out_shape=[jax.ShapeDtypeStruct((n, d), F32)] * 2,
        compiler_params=_cparams(("parallel",)),
        name="ffn_ple",
    )(h, y, p, nf.reshape(1, d), npl.reshape(1, d), nn.reshape(1, d), w1, w3, w2, pg, pp)


def _mm_kernel(x_ref, w_ref, o_ref):
    o_ref[...] = _dot(x_ref[...].astype(BF16), w_ref[...])


def matmul_rows(x, w, *, tm):
    n, k = x.shape
    m = w.shape[1]
    return pl.pallas_call(
        _mm_kernel,
        grid=(n // tm,),
        in_specs=[pl.BlockSpec((tm, k), lambda i: (i, 0)), _resident((k, m))],
        out_specs=pl.BlockSpec((tm, m), lambda i: (i, 0)),
        out_shape=jax.ShapeDtypeStruct((n, m), F32),
        compiler_params=_cparams(("parallel",)),
        name="matmul_rows",
    )(x, w)


def _glu_kernel(x_ref, wv_ref, wg_ref, o_ref):
    g = x_ref[...].astype(BF16)
    o_ref[...] = _dot(g, wv_ref[...]) * jax.nn.sigmoid(_dot(g, wg_ref[...]))


def glu_rows(x, wv, wg, *, tm):
    n, k = x.shape
    m = wv.shape[1]
    return pl.pallas_call(
        _glu_kernel,
        grid=(n // tm,),
        in_specs=[pl.BlockSpec((tm, k), lambda i: (i, 0)), _resident((k, m)),
                  _resident((k, m))],
        out_specs=pl.BlockSpec((tm, m), lambda i: (i, 0)),
        out_shape=jax.ShapeDtypeStruct((n, m), F32),
        compiler_params=_cparams(("parallel",)),
        name="glu_rows",
    )(x, wv, wg)


def _rwkv_proj_kernel(*refs, bb, tl, with_v):
    if with_v:
        (x_ref, sh_ref, vf_ref, mu_ref, b0_ref, wrkv_ref, w1_ref, w2_ref, a1_ref, a2_ref,
         g1_ref, g2_ref, v1_ref, v2_ref,
         r_out, lw_out, k_out, v_out, a_out, g_out, carry) = refs
    else:
        (x_ref, sh_ref, mu_ref, b0_ref, wrkv_ref, w1_ref, w2_ref, a1_ref, a2_ref,
         g1_ref, g2_ref,
         r_out, lw_out, k_out, v_out, a_out, g_out, carry) = refs
    d = x_ref.shape[-1]
    m = bb * tl

    @pl.when(pl.program_id(1) == 0)
    def _():
        carry[...] = sh_ref[...]

    x3 = x_ref[...]
    x = x3.reshape(m, d)
    prev = jnp.broadcast_to(carry[...], (bb, tl, d)).reshape(m, d)
    rolled = pltpu.roll(x, 1, 0)
    t_in_tile = lax.broadcasted_iota(jnp.int32, (m, d), 0) & (tl - 1)
    x_prev = jnp.where(t_in_tile == 0, prev, rolled)
    carry[...] = x_ref[:, tl - 1:tl, :]

    xx = x_prev - x

    def mix(j):
        return (x + xx * mu_ref[j:j + 1, :]).astype(BF16)

    xr, xw, xk, xv, xa, xg = (mix(j) for j in range(6))
    r = _dot(xr, wrkv_ref[0])
    k = _dot(xk, wrkv_ref[1])
    v = _dot(xv, wrkv_ref[2])
    zw = b0_ref[0:1, :] + _dot(jnp.tanh(_dot(xw, w1_ref[...])).astype(BF16), w2_ref[...])
    lw = -jnp.exp(_log_sigmoid(zw) - 0.5)
    a = jax.nn.sigmoid(b0_ref[1:2, :] + _dot(_dot(xa, a1_ref[...]).astype(BF16), a2_ref[...]))
    g = _dot(jax.nn.sigmoid(_dot(xg, g1_ref[...])).astype(BF16), g2_ref[...])
    if with_v:
        vf = vf_ref[...].reshape(m, d)
        lam = jax.nn.sigmoid(b0_ref[2:3, :]
                             + _dot(_dot(xv, v1_ref[...]).astype(BF16), v2_ref[...]))
        v = v + (vf - v) * lam
    shp = (bb, tl, d)
    r_out[...] = r.reshape(shp)
    lw_out[...] = lw.reshape(shp)
    k_out[...] = k.reshape(shp)
    v_out[...] = v.reshape(shp)
    a_out[...] = a.reshape(shp)
    g_out[...] = g.reshape(shp)


def rwkv_proj(xn, shift, v_first, prm, *, bb, tl):
    bsz, L, d = xn.shape
    with_v = v_first is not None
    seq = pl.BlockSpec((bb, tl, d), lambda b, l: (b, l, 0))
    ins = [xn, shift.reshape(bsz, 1, d)]
    specs = [seq, pl.BlockSpec((bb, 1, d), lambda b, l: (b, 0, 0))]
    if with_v:
        ins.append(v_first)
        specs.append(seq)
    names = ["mu", "b0", "w_rkv", "w1", "w2", "a1", "a2", "g1", "g2"]
    if with_v:
        names += ["v1", "v2"]
    for nme in names:
        ins.append(prm[nme])
        specs.append(_resident(prm[nme].shape))
    return pl.pallas_call(
        functools.partial(_rwkv_proj_kernel, bb=bb, tl=tl, with_v=with_v),
        grid=(bsz // bb, L // tl),
        in_specs=specs,
        out_specs=[seq] * 6,
        out_shape=[jax.ShapeDtypeStruct((bsz, L, d), F32)] * 6,
        scratch_shapes=[pltpu.VMEM((bb, 1, d), F32)],
        compiler_params=_cparams(("parallel", "arbitrary")),
        name="rwkv_proj",
    )(*ins)


def _unit_lower_inverse(n_mats, c, prec):
    row = lax.broadcasted_iota(jnp.int32, (c, c), 0)
    col = lax.broadcasted_iota(jnp.int32, (c, c), 1)
    eye = jnp.where(row == col, 1.0, 0.0).astype(F32)

    def same_block(shift):
        return (row >> shift) == (col >> shift)

    blk8 = same_block(3)
    n8 = [jnp.where(blk8, m, 0.0) for m in n_mats]
    n8_2 = [_dot(m, m, prec) for m in n8]
    t = [eye + m for m in n8]
    t = [ti + _dot(ti, m2, prec) for ti, m2 in zip(t, n8_2)]
    n8_4 = [_dot(m2, m2, prec) for m2 in n8_2]
    t = [ti + _dot(ti, m4, prec) for ti, m4 in zip(t, n8_4)]
    shift = 3
    while (1 << shift) < c:
        inner, outer = same_block(shift), same_block(shift + 1)
        off = [jnp.where(inner, 0.0, jnp.where(outer, m, 0.0)) for m in n_mats]
        left = [_dot(ti, oi, prec) for ti, oi in zip(t, off)]
        t = [ti + _dot(li, ti, prec) for ti, li in zip(t, left)]
        shift += 1
    return t


def _wkv_kernel(r_ref, lw_ref, k_ref, v_ref, a_ref, g_ref, hp_ref, s0_ref,
                z_out, s_out, s_scr, *, bb, tl, c, prec):
    n = RW_HEAD
    nheads = s_scr.shape[1]
    nchunk = tl // c

    @pl.when(pl.program_id(2) == 0)
    def _():
        s_scr[...] = s0_ref[...]

    row = lax.broadcasted_iota(jnp.int32, (c, c), 0)
    col = lax.broadcasted_iota(jnp.int32, (c, c), 1)
    tril = jnp.where(row >= col, 1.0, 0.0).astype(F32)
    row2 = lax.broadcasted_iota(jnp.int32, (2 * c, c), 0)
    col2 = lax.broadcasted_iota(jnp.int32, (2 * c, c), 1)
    mask2 = col2 <= jnp.where(row2 < c, row2 - 1, row2 - c)
    k_k = hp_ref[0:1, :]
    k_a = hp_ref[1:2, :]
    r_k = hp_ref[2:3, :]
    ln_w = hp_ref[3:4, :]
    ln_b = hp_ref[4:5, :]

    chains = [(s, h) for s in range(bb) for h in range(nheads)]

    def chunk(ci, carry):
        rows = pl.ds(pl.multiple_of(ci * c, c), c)
        lw = [lw_ref[s, rows, :] for s in range(bb)]
        cum = [_dot(tril, x, HIGHEST) for x in lw]
        seq = []
        for s in range(bb):
            r = r_ref[s, rows, :]
            k = k_ref[s, rows, :]
            a = a_ref[s, rows, :]
            cum_end = cum[s][c - 1:c, :]
            e_neg = jnp.exp(-cum[s])
            e_end = jnp.exp(cum_end - cum[s])
            k_mod = k * (1.0 + (a - 1.0) * k_a)
            seq.append(dict(
                a=a, v=v_ref[s, rows, :], kk_raw=k * k_k,
                r_t=r * jnp.exp(cum[s]), e_neg=e_neg, e_end=e_end,
                e_prev=jnp.exp(cum[s] - lw[s]), g_end=jnp.exp(cum_end),
                k_t=k_mod * e_neg, k_end=k_mod * e_end, rk=r * k_mod * r_k))
        ch = []
        for s, h in chains:
            q = seq[s]
            sl = slice(h * n, (h + 1) * n)
            kk = q["kk_raw"][:, sl]
            kk = kk * lax.rsqrt(jnp.maximum(jnp.sum(kk * kk, -1, keepdims=True), 1e-24))
            b_h = kk * q["a"][:, sl]
            a_t = -kk * q["e_prev"][:, sl]
            ch.append(dict(
                ar=jnp.concatenate([a_t, q["r_t"][:, sl]], axis=0),
                b_t=b_h * q["e_neg"][:, sl], k_t=q["k_t"][:, sl],
                kb_end=jnp.concatenate([q["k_end"][:, sl], b_h * q["e_end"][:, sl]], axis=0),
                v=q["v"][:, sl], g_end=q["g_end"][:, sl],
                bonus=jnp.sum(q["rk"][:, sl], -1, keepdims=True) * q["v"][:, sl]))
        ab = [jnp.where(mask2, _dot_nt(x["ar"], x["b_t"], prec), 0.0) for x in ch]
        ak = [jnp.where(mask2, _dot_nt(x["ar"], x["k_t"], prec), 0.0) for x in ch]
        t_inv = _unit_lower_inverse([m[:c] for m in ab], c, prec)
        s0 = [s_scr[s, h] for s, h in chains]
        x1 = [_dot_nt(x["ar"], si, prec) for x, si in zip(ch, s0)]
        x1 = [xi + _dot(m, x["v"], prec) for xi, m, x in zip(x1, ak, ch)]
        u = [_dot(ti, xi[:c], prec) for ti, xi in zip(t_inv, x1)]
        o = [xi[c:] + _dot(m[c:], ui, prec) for xi, m, ui in zip(x1, ab, u)]
        s_new = [si * x["g_end"] + _dot_tn(jnp.concatenate([x["v"], ui], axis=0), x["kb_end"], prec)
                 for x, si, ui in zip(ch, s0, u)]
        for (s, h), sn in zip(chains, s_new):
            s_scr[s, h] = sn
        normed = []
        for oi in o:
            mean = jnp.mean(oi, -1, keepdims=True)
            var = jnp.mean(jnp.square(oi - mean), -1, keepdims=True)
            normed.append((oi - mean) * lax.rsqrt(var + RW_GN_EPS))
        for s in range(bb):
            idx = [i for i, (si, _) in enumerate(chains) if si == s]
            o_all = jnp.concatenate([normed[i] for i in idx], axis=-1)
            bonus_all = jnp.concatenate([ch[i]["bonus"] for i in idx], axis=-1)
            z_out[s, rows, :] = (o_all * ln_w + ln_b + bonus_all) * g_ref[s, rows, :]
        return carry

    lax.fori_loop(0, nchunk, chunk, 0)

    @pl.when(pl.program_id(2) == pl.num_programs(2) - 1)
    def _():
        s_out[...] = s_scr[...]


def wkv_scan(r, lw, k, v, a, g, hp, s0, *, bb, tl, c, nheads, prec):
    bsz, L, d = r.shape
    width = nheads * RW_HEAD
    seq = pl.BlockSpec((bb, tl, width), lambda b, h, l: (b, l, h))
    st = pl.BlockSpec((bb, nheads, RW_HEAD, RW_HEAD), lambda b, h, l: (b, h, 0, 0))
    return pl.pallas_call(
        functools.partial(_wkv_kernel, bb=bb, tl=tl, c=c, prec=prec),
        grid=(bsz // bb, d // width, L // tl),
        in_specs=[seq] * 6 + [pl.BlockSpec((SUBLANES, width), lambda b, h, l: (0, h)), st],
        out_specs=[seq, st],
        out_shape=[jax.ShapeDtypeStruct((bsz, L, d), F32),
                   jax.ShapeDtypeStruct(s0.shape, F32)],
        scratch_shapes=[pltpu.VMEM((bb, nheads, RW_HEAD, RW_HEAD), F32)],
        compiler_params=_cparams(("parallel", "parallel", "arbitrary")),
        name="wkv_scan",
    )(r, lw, k, v, a, g, hp, s0)


def rwkv_layer(xn, shift, wkv, v_first, prm, cfg):
    r, lw, k, v, a, g = rwkv_proj(xn, shift, v_first, prm, bb=cfg["proj_bb"], tl=cfg["proj_tl"])
    z, wkv_new = wkv_scan(r, lw, k, v, a, g, prm["hp"], wkv,
                          bb=cfg["scan_bb"], tl=cfg["wkv_tl"], c=cfg["wkv_c"],
                          nheads=cfg["wkv_heads"], prec=SCAN_PREC)
    bsz, L, d = xn.shape
    y = matmul_rows(z.reshape(bsz * L, d), prm["w_o"], tm=cfg["tm"]).reshape(bsz, L, d)
    return y, xn[:, -1], wkv_new, (v if v_first is None else v_first)


def _prep_rwkv(mu, w0, w1, w2, a0, a1, a2, g1, g2, k_k, k_a, r_k, w_rkv, w_o, lnx_w, lnx_b,
               v_lora=None):
    d = D_MODEL
    zero = jnp.zeros((d,), F32)
    prm = dict(mu=mu, w_rkv=w_rkv.astype(BF16), w_o=w_o.astype(BF16),
               w1=w1.astype(BF16), w2=w2.astype(BF16), a1=a1.astype(BF16), a2=a2.astype(BF16),
               g1=g1.astype(BF16), g2=g2.astype(BF16))
    v0 = zero
    if v_lora is not None:
        v0, v1, v2 = v_lora
        prm["v1"] = v1.astype(BF16)
        prm["v2"] = v2.astype(BF16)
    prm["b0"] = jnp.stack([w0, a0, v0] + [zero] * 5)
    prm["hp"] = jnp.stack([k_k, k_a, r_k.reshape(d), lnx_w, lnx_b] + [zero] * 3)
    return prm


S5_CH = 8
S5_CL = S5_LANES // S5_CH


def _s5_kernel(u_ref, w_ref, cm_ref, ar_ref, ai_ref, d_ref, h0r_ref, h0i_ref,
               g_out, hr_out, hi_out, hs, hc, *, bsz, tr):
    i = pl.program_id(1)
    cl = S5_CL

    @pl.when(i == 0)
    def _():
        hc[:, :cl] = h0r_ref[...]
        hc[:, cl:] = h0i_ref[...]

    u = u_ref[...]
    hs[...] = _dot(u, w_ref[...])
    ar = jnp.broadcast_to(ar_ref[...], (bsz, cl))
    ai = jnp.broadcast_to(ai_ref[...], (bsz, cl))

    def step(t, carry):
        hr, hi = carry
        rows = pl.ds(pl.multiple_of(t * bsz, bsz), bsz)
        nr = ar * hr - ai * hi + hs[rows, :cl]
        ni = ar * hi + ai * hr + hs[rows, cl:]
        hs[rows, :cl] = nr
        hs[rows, cl:] = ni
        return nr, ni

    nstep = tr // bsz
    hr, hi = lax.fori_loop(0, nstep, step, (hc[:, :cl], hc[:, cl:]),
                           unroll=min(nstep, SUBLANES))
    hc[:, :cl] = hr
    hc[:, cl:] = hi
    y = _dot(hs[...], cm_ref[...]) + d_ref[...] * u
    g_out[...] = jax.nn.gelu(y)

    @pl.when(i == pl.num_programs(1) - 1)
    def _():
        hr_out[...] = hr
        hi_out[...] = hi


def s5_scan(u_tm, h0r, h0i, prm, *, bsz, tr):
    rows, d = u_tm.shape
    cl = S5_CL
    st = pl.BlockSpec((bsz, cl), lambda c, i: (0, c))
    return pl.pallas_call(
        functools.partial(_s5_kernel, bsz=bsz, tr=tr),
        grid=(S5_CH, rows // tr),
        in_specs=[
            pl.BlockSpec((tr, LANES), lambda c, i: (i, c)),
            pl.BlockSpec((None, LANES, 2 * cl), lambda c, i: (c, 0, 0)),
            pl.BlockSpec((None, 2 * cl, LANES), lambda c, i: (c, 0, 0)),
            pl.BlockSpec((None, 1, cl), lambda c, i: (c, 0, 0)),
            pl.BlockSpec((None, 1, cl), lambda c, i: (c, 0, 0)),
            pl.BlockSpec((1, LANES), lambda c, i: (0, c)),
            st, st,
        ],
        out_specs=[pl.BlockSpec((tr, LANES), lambda c, i: (i, c)), st, st],
        out_shape=[jax.ShapeDtypeStruct((rows, d), F32),
                   jax.ShapeDtypeStruct((bsz, S5_LANES), F32),
                   jax.ShapeDtypeStruct((bsz, S5_LANES), F32)],
        scratch_shapes=[pltpu.VMEM((tr, 2 * cl), F32), pltpu.VMEM((bsz, 2 * cl), F32)],
        compiler_params=_cparams(("parallel", "arbitrary")),
        name="s5_scan",
    )(u_tm, prm["w_in"], prm["c_out"], prm["ab_re"], prm["ab_im"], prm["d"], h0r, h0i)


def _prep_s5(a_re, a_im, log_dt, b_re, b_im, c_re, c_im, d_skip, glu_v, glu_g):
    dt = jnp.exp(log_dt.astype(F32))[:, None]
    lr, li = a_re.astype(F32), a_im.astype(F32)
    mag = jnp.exp(lr * dt)
    ab_re, ab_im = mag * jnp.cos(li * dt), mag * jnp.sin(li * dt)
    den = lr * lr + li * li
    q_re = ((ab_re - 1.0) * lr + ab_im * li) / den
    q_im = (ab_im * lr - (ab_re - 1.0) * li) / den
    bb_re = q_re[..., None] * b_re - q_im[..., None] * b_im
    bb_im = q_re[..., None] * b_im + q_im[..., None] * b_re
    gl = S5_GROUPS // S5_CH
    eye = jnp.eye(gl, dtype=F32)

    def in_blocks(bb):
        t = bb.reshape(S5_CH, gl, S5_STATE, S5_GROUP)
        return jnp.einsum('cgph,gk->cghkp', t, eye).reshape(S5_CH, LANES, S5_CL)

    def out_blocks(cc):
        t = cc.reshape(S5_CH, gl, S5_GROUP, S5_STATE)
        return jnp.einsum('cghp,gk->ckpgh', t, eye).reshape(S5_CH, S5_CL, LANES)

    return dict(
        w_in=jnp.concatenate([in_blocks(bb_re), in_blocks(bb_im)], axis=-1),
        c_out=jnp.concatenate([out_blocks(c_re.astype(F32)), -out_blocks(c_im.astype(F32))],
                              axis=1),
        ab_re=ab_re.reshape(S5_CH, 1, S5_CL), ab_im=ab_im.reshape(S5_CH, 1, S5_CL),
        d=d_skip.reshape(1, D_MODEL),
        glu_v=glu_v.astype(BF16), glu_g=glu_g.astype(BF16))


def s5_layer(xn, h_re0, h_im0, prm, cfg):
    bsz, L, d = xn.shape
    u_tm = jnp.swapaxes(xn, 0, 1).reshape(L * bsz, d)
    g, hr, hi = s5_scan(u_tm, h_re0.reshape(bsz, S5_LANES), h_im0.reshape(bsz, S5_LANES), prm,
                        bsz=bsz, tr=cfg["s5_tr"])
    y_tm = glu_rows(g, prm["glu_v"], prm["glu_g"], tm=cfg["tm"])
    y = jnp.swapaxes(y_tm.reshape(L, bsz, d), 0, 1)
    return (y, hr.reshape(bsz, S5_GROUPS, S5_STATE), hi.reshape(bsz, S5_GROUPS, S5_STATE))


def _mamba_in_kernel(x_ref, cp_ref, wz_ref, wx_ref, wdt_ref, cw_ref, cb_ref, dtb_ref,
                     z_out, xs_out, b_out, c_out, dt_out, nc_out, carry, *, bb, tl):
    d = x_ref.shape[-1]
    m = bb * tl
    cd = M_CONV_DIM
    l = pl.program_id(1)

    @pl.when(l == 0)
    def _():
        carry[:, :SUBLANES - (M_CONV - 1), :] = jnp.zeros(
            (bb, SUBLANES - (M_CONV - 1), cd), F32)
        carry[:, SUBLANES - (M_CONV - 1):, :] = cp_ref[...]

    x = x_ref[...].reshape(m, d).astype(BF16)
    z_out[...] = _dot(x, wz_ref[...]).reshape(bb, tl, M_INNER)
    dt_raw = _dot(x, wdt_ref[...])[:, :M_HEADS]
    dt_out[...] = _softplus(dt_raw + dtb_ref[...]).reshape(bb, tl, M_HEADS)
    xbc = _dot(x, wx_ref[...]).reshape(bb, tl, cd)
    full = jnp.concatenate([carry[...], xbc], axis=1).reshape(bb * (tl + SUBLANES), cd)
    conv = cb_ref[...] + xbc.reshape(m, cd) * cw_ref[M_CONV - 1:M_CONV, :]
    for j in range(1, M_CONV):
        sh = pltpu.roll(full, j, 0).reshape(bb, tl + SUBLANES, cd)[:, SUBLANES:, :]
        conv = conv + sh.reshape(m, cd) * cw_ref[M_CONV - 1 - j:M_CONV - j, :]
    act = conv * jax.nn.sigmoid(conv)
    xs_out[...] = act[:, :M_INNER].reshape(bb, tl, M_INNER)
    nbc = M_GROUPS * M_STATE
    b_out[...] = act[:, M_INNER:M_INNER + nbc].reshape(bb, tl, nbc)
    c_out[...] = act[:, M_INNER + nbc:].reshape(bb, tl, nbc)
    full3 = full.reshape(bb, tl + SUBLANES, cd)
    carry[...] = full3[:, tl:, :]

    @pl.when(l == pl.num_programs(1) - 1)
    def _():
        nc_out[...] = full3[:, tl + SUBLANES - (M_CONV - 1):, :]


def mamba_in(xn, conv_prev, prm, *, bb, tl):
    bsz, L, d = xn.shape
    cd = M_CONV_DIM
    nbc = M_GROUPS * M_STATE
    seq = lambda w: pl.BlockSpec((bb, tl, w), lambda b, l: (b, l, 0))
    cps = pl.BlockSpec((bb, M_CONV - 1, cd), lambda b, l: (b, 0, 0))
    return pl.pallas_call(
        functools.partial(_mamba_in_kernel, bb=bb, tl=tl),
        grid=(bsz // bb, L // tl),
        in_specs=[seq(d), cps, _resident((d, M_INNER)), _resident((d, cd)),
                  _resident((d, LANES)),
                  _full((M_CONV, cd)), _full((1, cd)), _full((1, M_HEADS))],
        out_specs=[seq(M_INNER), seq(M_INNER), seq(nbc), seq(nbc), seq(M_HEADS), cps],
        out_shape=[jax.ShapeDtypeStruct((bsz, L, M_INNER), F32),
                   jax.ShapeDtypeStruct((bsz, L, M_INNER), F32),
                   jax.ShapeDtypeStruct((bsz, L, nbc), F32),
                   jax.ShapeDtypeStruct((bsz, L, nbc), F32),
                   jax.ShapeDtypeStruct((bsz, L, M_HEADS), F32),
                   jax.ShapeDtypeStruct((bsz, M_CONV - 1, cd), F32)],
        scratch_shapes=[pltpu.VMEM((bb, SUBLANES, cd), F32)],
        compiler_params=_cparams(("parallel", "arbitrary")),
        name="mamba_in",
    )(xn, conv_prev, prm["wz"], prm["wx"], prm["wdt"], prm["conv_w"], prm["conv_b"],
      prm["dt_bias"])


def _ssd_kernel(xs_ref, b_ref, c_ref, z_ref, dt_ref, hv_ref, ex_ref, nw_ref, s0_ref,
                y_out, s_out, s_scr, *, bb, tl, c):
    p, n = M_HEADDIM, M_STATE
    hpg = M_HEADS // M_GROUPS
    gw = M_INNER // M_GROUPS
    nchunk = tl // c

    @pl.when(pl.program_id(1) == 0)
    def _():
        s_scr[...] = s0_ref[...]

    row = lax.broadcasted_iota(jnp.int32, (c, c), 0)
    col = lax.broadcasted_iota(jnp.int32, (c, c), 1)
    incl = row >= col
    tril = jnp.where(incl, 1.0, 0.0).astype(F32)
    triu = jnp.where(row <= col, 1.0, 0.0).astype(F32)
    a_neg = hv_ref[0:1, :]
    d_exp = nw_ref[1:2, :]
    norm_w = nw_ref[0:1, :]
    expand = ex_ref[...]

    def chunk(i, carry):
        s = i // nchunk
        ci = i - s * nchunk
        rows = pl.ds(pl.multiple_of(ci * c, c), c)
        xs = xs_ref[s, rows, :]
        bm = b_ref[s, rows, :]
        cm = c_ref[s, rows, :]
        dt = dt_ref[s, rows, :]
        a = dt * a_neg
        cum = _dot(tril, a, HIGHEST)
        cum_t = _dot_tn(a, triu, HIGHEST)
        cum_end = cum[c - 1:c, :]
        e_end = jnp.exp(cum_end)
        dt_x = _dot(dt, expand)
        dec_x = _dot(jnp.exp(cum_end - cum), expand)
        ecum_x = _dot(jnp.exp(cum), expand)
        xdt = xs * dt_x
        xend = xdt * dec_x
        yd, yo = [], []
        for gi in range(M_GROUPS):
            b_g = bm[:, gi * n:(gi + 1) * n]
            c_g = cm[:, gi * n:(gi + 1) * n]
            cb = _dot_nt(c_g, b_g)
            for hh in range(hpg):
                h = gi * hpg + hh
                sl = slice(h * p, (h + 1) * p)
                seg = cum[:, h:h + 1] - cum_t[h:h + 1, :]
                lmat = jnp.where(incl, jnp.exp(jnp.where(incl, seg, 0.0)), 0.0)
                s_h = s_scr[s, h]
                yd.append(_dot(cb * lmat, xdt[:, sl]))
                yo.append(_dot_nt(c_g, s_h))
                s_scr[s, h] = s_h * e_end[:, h:h + 1] + _dot_tn(xend[:, sl], b_g)
        y = (jnp.concatenate(yd, axis=-1) + jnp.concatenate(yo, axis=-1) * ecum_x
             + d_exp * xs)
        zz = z_ref[s, rows, :]
        y = y * (zz * jax.nn.sigmoid(zz))
        outs = []
        for gi in range(M_GROUPS):
            yg = y[:, gi * gw:(gi + 1) * gw]
            outs.append(yg * lax.rsqrt(jnp.mean(yg * yg, -1, keepdims=True) + EPS))
        y_out[s, rows, :] = jnp.concatenate(outs, axis=-1) * norm_w
        return carry

    lax.fori_loop(0, bb * nchunk, chunk, 0)

    @pl.when(pl.program_id(1) == pl.num_programs(1) - 1)
    def _():
        s_out[...] = s_scr[...]


def ssd_scan(xs, bm, cm, z, dt, prm, s0, *, bb, tl, c):
    bsz, L, _ = xs.shape
    nbc = M_GROUPS * M_STATE
    seq = lambda w: pl.BlockSpec((bb, tl, w), lambda b, l: (b, l, 0))
    st = pl.BlockSpec((bb, M_HEADS, M_HEADDIM, M_STATE), lambda b, l: (b, 0, 0, 0))
    return pl.pallas_call(
        functools.partial(_ssd_kernel, bb=bb, tl=tl, c=c),
        grid=(bsz // bb, L // tl),
        in_specs=[seq(M_INNER), seq(nbc), seq(nbc), seq(M_INNER), seq(M_HEADS),
                  _full((SUBLANES, M_HEADS)), _full((M_HEADS, M_INNER)),
                  _full((SUBLANES, M_INNER)), st],
        out_specs=[seq(M_INNER), st],
        out_shape=[jax.ShapeDtypeStruct((bsz, L, M_INNER), F32),
                   jax.ShapeDtypeStruct(s0.shape, F32)],
        scratch_shapes=[pltpu.VMEM((bb, M_HEADS, M_HEADDIM, M_STATE), F32)],
        compiler_params=_cparams(("parallel", "arbitrary")),
        name="ssd_scan",
    )(xs, bm, cm, z, dt, prm["hv"], prm["expand"], prm["nw"], s0)


def _prep_mamba(in_proj, conv_w, conv_b, dt_bias, a_log, d_skip, norm_w, out_proj):
    d = D_MODEL
    wdt = jnp.zeros((d, LANES), F32).at[:, :M_HEADS].set(in_proj[:, M_INNER + M_CONV_DIM:])
    expand = jnp.repeat(jnp.eye(M_HEADS, dtype=F32), M_HEADDIM, axis=1)
    zrow_h = jnp.zeros((M_HEADS,), F32)
    zrow_i = jnp.zeros((M_INNER,), F32)
    return dict(
        wz=in_proj[:, :M_INNER].astype(BF16),
        wx=in_proj[:, M_INNER:M_INNER + M_CONV_DIM].astype(BF16),
        wdt=wdt.astype(BF16),
        conv_w=conv_w, conv_b=conv_b.reshape(1, M_CONV_DIM),
        dt_bias=dt_bias.reshape(1, M_HEADS),
        hv=jnp.stack([-jnp.exp(a_log.astype(F32))] + [zrow_h] * 7),
        expand=expand,
        nw=jnp.stack([norm_w, jnp.repeat(d_skip, M_HEADDIM)] + [zrow_i] * 6),
        out_proj=out_proj.astype(BF16))


def mamba_layer(xn, conv_prev, ssm_prev, prm, cfg):
    bsz, L, d = xn.shape
    z, xs, bm, cm, dt, new_conv = mamba_in(xn, conv_prev, prm, bb=cfg["mamba_bb"],
                                           tl=cfg["mamba_tl"])
    y, ssm_new = ssd_scan(xs, bm, cm, z, dt, prm, ssm_prev,
                          bb=cfg["ssd_bb"], tl=cfg["ssd_tl"], c=cfg["ssd_c"])
    out = matmul_rows(y.reshape(bsz * L, M_INNER), prm["out_proj"], tm=cfg["tm"])
    return out.reshape(bsz, L, d), new_conv, ssm_new


def _config(bsz, L):
    if L >= 512:
        return dict(tm=512, proj_bb=1, proj_tl=256,
                    scan_bb=min(bsz, 8), wkv_tl=256, wkv_c=64, wkv_heads=4,
                    mamba_bb=1, mamba_tl=256,
                    ssd_bb=1, ssd_tl=256, ssd_c=M_CHUNK, s5_tr=512)
    rows = bsz * L
    tm = min(512, rows)
    return dict(tm=tm, proj_bb=min(bsz, 256 // L), proj_tl=L,
                scan_bb=min(bsz, 16), wkv_tl=L, wkv_c=L, wkv_heads=2,
                mamba_bb=min(bsz, 256 // L), mamba_tl=L,
                ssd_bb=min(bsz, 4), ssd_tl=L, ssd_c=min(M_CHUNK, L), s5_tr=rows)


def _trunk(x, p, states, layer_prms, shared, cfg):
    bsz, L, d = x.shape
    n = bsz * L
    tm = cfg["tm"]
    xn = rmsnorm_rows(x.reshape(n, d), shared["norm_mix"][0], tm).reshape(bsz, L, d)
    h = x.reshape(n, d)
    v_first = None
    new_states = []
    for i in range(DEPTH):
        st_a, st_b = states[2 * i], states[2 * i + 1]
        prm = layer_prms[i]
        kind = i % 3
        if kind == 0:
            y, n_a, n_b, v_first = rwkv_layer(xn, st_a, st_b, v_first, prm, cfg)
        elif kind == 1:
            y, n_a, n_b = s5_layer(xn, st_a, st_b, prm, cfg)
        else:
            y, n_a, n_b = mamba_layer(xn, st_a, st_b, prm, cfg)
        new_states += [n_a, n_b]
        nxt = shared["norm_mix"][i + 1] if i + 1 < DEPTH else shared["final_norm"]
        h, xn_flat = ffn_ple(h, y.reshape(n, d), p[i].reshape(n, PLE_DIM),
                             shared["norm_ffn"][i], shared["norm_ple"][i], nxt,
                             shared["ffn_w1"][i], shared["ffn_w3"][i], shared["ffn_w2"][i],
                             shared["ple_gate"][i], shared["ple_proj"][i], tm=tm)
        xn = xn_flat.reshape(bsz, L, d)
    return xn, new_states


def kernel(x_prompt, x_sample, p_prompt, p_sample, state_l0_shift, state_l0_wkv, state_l1_s5_re, state_l1_s5_im, state_l2_conv, state_l2_ssm, state_l3_shift, state_l3_wkv, l0_mu, l0_w0, l0_w1, l0_w2, l0_a0, l0_a1, l0_a2, l0_g1, l0_g2, l0_k_k, l0_k_a, l0_r_k, l0_w_rkv, l0_w_o, l0_lnx_w, l0_lnx_b, l1_a_re, l1_a_im, l1_log_dt, l1_b_re, l1_b_im, l1_c_re, l1_c_im, l1_d, l1_glu_v, l1_glu_g, l2_in_proj, l2_conv_w, l2_conv_b, l2_dt_bias, l2_a_log, l2_d, l2_norm_w, l2_out_proj, l3_mu, l3_w0, l3_w1, l3_w2, l3_a0, l3_a1, l3_a2, l3_g1, l3_g2, l3_k_k, l3_k_a, l3_r_k, l3_w_rkv, l3_w_o, l3_lnx_w, l3_lnx_b, l3_v0, l3_v1, l3_v2, norm_mix, norm_ffn, norm_ple, ffn_w1, ffn_w3, ffn_w2, ple_proj, ple_gate, final_norm):
    layer_prms = (
        _prep_rwkv(l0_mu, l0_w0, l0_w1, l0_w2, l0_a0, l0_a1, l0_a2, l0_g1, l0_g2,
                   l0_k_k, l0_k_a, l0_r_k, l0_w_rkv, l0_w_o, l0_lnx_w, l0_lnx_b),
        _prep_s5(l1_a_re, l1_a_im, l1_log_dt, l1_b_re, l1_b_im, l1_c_re, l1_c_im,
                 l1_d, l1_glu_v, l1_glu_g),
        _prep_mamba(l2_in_proj, l2_conv_w, l2_conv_b, l2_dt_bias, l2_a_log, l2_d,
                    l2_norm_w, l2_out_proj),
        _prep_rwkv(l3_mu, l3_w0, l3_w1, l3_w2, l3_a0, l3_a1, l3_a2, l3_g1, l3_g2,
                   l3_k_k, l3_k_a, l3_r_k, l3_w_rkv, l3_w_o, l3_lnx_w, l3_lnx_b,
                   (l3_v0, l3_v1, l3_v2)),
    )
    shared = dict(norm_mix=norm_mix, norm_ffn=norm_ffn, norm_ple=norm_ple,
                  final_norm=final_norm,
                  ffn_w1=ffn_w1.astype(BF16), ffn_w3=ffn_w3.astype(BF16),
                  ffn_w2=ffn_w2.astype(BF16), ple_gate=ple_gate.astype(BF16),
                  ple_proj=ple_proj.astype(BF16))
    sample_states = [state_l0_shift, state_l0_wkv, state_l1_s5_re, state_l1_s5_im,
                     state_l2_conv, state_l2_ssm, state_l3_shift, state_l3_wkv]
    bp = x_prompt.shape[0]
    prompt_states = [jnp.zeros((bp,) + s.shape[1:], F32) for s in sample_states]
    y_prompt, new_p = _trunk(x_prompt, p_prompt, prompt_states, layer_prms, shared,
                             _config(*x_prompt.shape[:2]))
    y_sample, new_s = _trunk(x_sample, p_sample, sample_states, layer_prms, shared,
                             _config(*x_sample.shape[:2]))
    return (y_prompt, y_sample, *new_p, *new_s)
```

```python
import functools
import math

import jax
import jax.numpy as jnp
from jax import lax
from jax.experimental import pallas as pl
from jax.experimental.pallas import tpu as pltpu

F32 = jnp.float32
BF16 = jnp.bfloat16
HIGHEST = lax.Precision.HIGHEST

D_MODEL = 1024
DEPTH = 4
PLE_DIM = 256
EPS = 1e-6
D_FF = 2816
RW_HEAD = 64
RW_HEADS = D_MODEL // RW_HEAD
RW_GN_EPS = 64e-5
S5_GROUP = 16
S5_GROUPS = D_MODEL // S5_GROUP
S5_STATE = 64
S5_LANES = S5_GROUPS * S5_STATE
M_INNER = 2 * D_MODEL
M_HEADDIM = 64
M_HEADS = M_INNER // M_HEADDIM
M_STATE = 128
M_GROUPS = 4
M_CONV = 4
M_CHUNK = 64
M_CONV_DIM = M_INNER + 2 * M_GROUPS * M_STATE

LANES = 128
MXU_TILE = 256
SUBLANES = 8
VMEM_LIMIT = 56 * 1024 * 1024

SCAN_PREC = None


def _cparams(sem):
    return pltpu.CompilerParams(dimension_semantics=sem, vmem_limit_bytes=VMEM_LIMIT)


def _operands(a, b, prec):
    if prec is None:
        return a.astype(BF16), b.astype(BF16)
    return a, b


def _dot(a, b, prec=None):
    a, b = _operands(a, b, prec)
    return jnp.dot(a, b, preferred_element_type=F32, precision=prec)


def _dot_nt(a, b, prec=None):
    a, b = _operands(a, b, prec)
    return lax.dot_general(a, b, (((1,), (1,)), ((), ())),
                           preferred_element_type=F32, precision=prec)


def _dot_tn(a, b, prec=None):
    a, b = _operands(a, b, prec)
    return lax.dot_general(a, b, (((0,), (0,)), ((), ())),
                           preferred_element_type=F32, precision=prec)


def _rms(x, g):
    return x * lax.rsqrt(jnp.mean(x * x, axis=-1, keepdims=True) + EPS) * g


def _log_sigmoid(z):
    return -(jnp.maximum(-z, 0.0) + jnp.log1p(jnp.exp(-jnp.abs(z))))


def _softplus(z):
    return jnp.maximum(z, 0.0) + jnp.log1p(jnp.exp(-jnp.abs(z)))


def _full(shape):
    n = len(shape)
    return pl.BlockSpec(shape, lambda *_: (0,) * n)


def _resident(shape):
    n = len(shape)
    return pl.BlockSpec(shape, lambda *_: (0,) * n, pipeline_mode=pl.Buffered(1))


def _rmsnorm_kernel(x_ref, g_ref, o_ref):
    o_ref[...] = _rms(x_ref[...], g_ref[...])


def rmsnorm_rows(x, g, tm):
    n, d = x.shape
    return pl.pallas_call(
        _rmsnorm_kernel,
        grid=(n // tm,),
        in_specs=[pl.BlockSpec((tm, d), lambda i: (i, 0)), _full((1, d))],
        out_specs=pl.BlockSpec((tm, d), lambda i: (i, 0)),
        out_shape=jax.ShapeDtypeStruct((n, d), F32),
        compiler_params=_cparams(("parallel",)),
        name="rmsnorm",
    )(x, g.reshape(1, d))


def _ffn_kernel(*refs, project):
    if project:
        (h_ref, mix_ref, wo_ref, p_ref, nf_ref, np_ref, nn_ref, w1_ref, w3_ref, w2_ref,
         pg_ref, pp_ref, h_out, xn_out) = refs
        y = _dot(mix_ref[...], wo_ref[...])
    else:
        (h_ref, mix_ref, p_ref, nf_ref, np_ref, nn_ref, w1_ref, w3_ref, w2_ref,
         pg_ref, pp_ref, h_out, xn_out) = refs
        y = mix_ref[...]
    hin = h_ref[...] + y
    hn = _rms(hin, nf_ref[...]).astype(BF16)
    acc = None
    for ci in range(D_FF // MXU_TILE):
        cs = slice(ci * MXU_TILE, (ci + 1) * MXU_TILE)
        a = _dot(hn, w1_ref[:, cs])
        b = _dot(hn, w3_ref[:, cs])
        t = (a * jax.nn.sigmoid(a) * b).astype(BF16)
        part = _dot(t, w2_ref[cs, :])
        acc = part if acc is None else acc + part
    h2 = hin + acc
    gate = jax.nn.sigmoid(_dot(_rms(h2, np_ref[...]).astype(BF16), pg_ref[...]))
    h3 = h2 + gate * _dot(p_ref[...].astype(BF16), pp_ref[...])
    h_out[...] = h3
    xn_out[...] = _rms(h3, nn_ref[...])


def ffn_ple(h, mix, wo, p, nf, npl, nn, w1, w3, w2, pg, pp, *, tm, seq_len,
            mix_time_major=False, xn_time_major=False):
    n, d = h.shape
    km = mix.shape[-1]
    nl = seq_len // tm
    nseq = n // seq_len
    row = lambda i: (i, 0)
    tmaj = lambda i: (i % nl, i // nl)
    if mix_time_major:
        mix = mix.reshape(seq_len, nseq * km)
    ins = [h, mix]
    specs = [pl.BlockSpec((tm, d), row), pl.BlockSpec((tm, km), tmaj if mix_time_major else row)]
    if wo is not None:
        ins.append(wo)
        specs.append(_resident(wo.shape))
    ins += [p, nf.reshape(1, d), npl.reshape(1, d), nn.reshape(1, d), w1, w3, w2, pg, pp]
    specs += [pl.BlockSpec((tm, PLE_DIM), row), _full((1, d)), _full((1, d)), _full((1, d)),
              _resident((d, D_FF)), _resident((d, D_FF)), _resident((D_FF, d)),
              _resident((d, d)), _resident((PLE_DIM, d))]
    xn_shape = (seq_len, nseq * d) if xn_time_major else (n, d)
    return pl.pallas_call(
        functools.partial(_ffn_kernel, project=wo is not None),
        grid=(n // tm,),
        in_specs=specs,
        out_specs=[pl.BlockSpec((tm, d), row),
                   pl.BlockSpec((tm, d), tmaj if xn_time_major else row)],
        out_shape=[jax.ShapeDtypeStruct((n, d), F32), jax.ShapeDtypeStruct(xn_shape, F32)],
        compiler_params=_cparams(("parallel",)),
        name="ffn_ple",
    )(*ins)


def _glu_kernel(x_ref, wv_ref, wg_ref, o_ref):
    g = x_ref[...].astype(BF16)
    o_ref[...] = _dot(g, wv_ref[...]) * jax.nn.sigmoid(_dot(g, wg_ref[...]))


def glu_rows(x, wv, wg, *, tm):
    n, k = x.shape
    m = wv.shape[1]
    return pl.pallas_call(
        _glu_kernel,
        grid=(n // tm,),
        in_specs=[pl.BlockSpec((tm, k), lambda i: (i, 0)), _resident((k, m)),
                  _resident((k, m))],
        out_specs=pl.BlockSpec((tm, m), lambda i: (i, 0)),
        out_shape=jax.ShapeDtypeStruct((n, m), F32),
        compiler_params=_cparams(("parallel",)),
        name="glu_rows",
    )(x, wv, wg)


def _rwkv_proj_kernel(*refs, bb, tl, with_v):
    if with_v:
        (x_ref, sh_ref, vf_ref, mu_ref, b0_ref, wrkv_ref, w1_ref, w2_ref, a1_ref, a2_ref,
         g1_ref, g2_ref, v1_ref, v2_ref,
         r_out, lw_out, k_out, v_out, a_out, g_out, carry) = refs
    else:
        (x_ref, sh_ref, mu_ref, b0_ref, wrkv_ref, w1_ref, w2_ref, a1_ref, a2_ref,
         g1_ref, g2_ref,
         r_out, lw_out, k_out, v_out, a_out, g_out, carry) = refs
    d = x_ref.shape[-1]
    m = bb * tl

    @pl.when(pl.program_id(1) == 0)
    def _():
        carry[...] = sh_ref[...]

    x3 = x_ref[...]
    x = x3.reshape(m, d)
    prev = jnp.broadcast_to(carry[...], (bb, tl, d)).reshape(m, d)
    rolled = pltpu.roll(x, 1, 0)
    t_in_tile = lax.broadcasted_iota(jnp.int32, (m, d), 0) & (tl - 1)
    x_prev = jnp.where(t_in_tile == 0, prev, rolled)
    carry[...] = x_ref[:, tl - 1:tl, :]

    xx = x_prev - x

    def mix(j):
        return (x + xx * mu_ref[j:j + 1, :]).astype(BF16)

    xr, xw, xk, xv, xa, xg = (mix(j) for j in range(6))
    r = _dot(xr, wrkv_ref[0])
    k = _dot(xk, wrkv_ref[1])
    v = _dot(xv, wrkv_ref[2])
    zw = b0_ref[0:1, :] + _dot(jnp.tanh(_dot(xw, w1_ref[...])).astype(BF16), w2_ref[...])
    lw = -jnp.exp(_log_sigmoid(zw) - 0.5)
    a = jax.nn.sigmoid(b0_ref[1:2, :] + _dot(_dot(xa, a1_ref[...]).astype(BF16), a2_ref[...]))
    g = _dot(jax.nn.sigmoid(_dot(xg, g1_ref[...])).astype(BF16), g2_ref[...])
    if with_v:
        vf = vf_ref[...].reshape(m, d)
        lam = jax.nn.sigmoid(b0_ref[2:3, :]
                             + _dot(_dot(xv, v1_ref[...]).astype(BF16), v2_ref[...]))
        v = v + (vf - v) * lam
    shp = (bb, tl, d)
    r_out[...] = r.reshape(shp)
    lw_out[...] = lw.reshape(shp)
    k_out[...] = k.reshape(shp)
    v_out[...] = v.reshape(shp)
    a_out[...] = a.reshape(shp)
    g_out[...] = g.reshape(shp)


def rwkv_proj(xn, shift, v_first, prm, *, bb, tl):
    bsz, L, d = xn.shape
    with_v = v_first is not None
    seq = pl.BlockSpec((bb, tl, d), lambda b, l: (b, l, 0))
    ins = [xn, shift.reshape(bsz, 1, d)]
    specs = [seq, pl.BlockSpec((bb, 1, d), lambda b, l: (b, 0, 0))]
    if with_v:
        ins.append(v_first)
        specs.append(seq)
    names = ["mu", "b0", "w_rkv", "w1", "w2", "a1", "a2", "g1", "g2"]
    if with_v:
        names += ["v1", "v2"]
    for nme in names:
        ins.append(prm[nme])
        specs.append(_resident(prm[nme].shape))
    return pl.pallas_call(
        functools.partial(_rwkv_proj_kernel, bb=bb, tl=tl, with_v=with_v),
        grid=(bsz // bb, L // tl),
        in_specs=specs,
        out_specs=[seq] * 6,
        out_shape=[jax.ShapeDtypeStruct((bsz, L, d), F32)] * 6,
        scratch_shapes=[pltpu.VMEM((bb, 1, d), F32)],
        compiler_params=_cparams(("parallel", "arbitrary")),
        name="rwkv_proj",
    )(*ins)


def _unit_lower_inverse(n_mats, eye, same_block, size, mm):
    blk8 = same_block(3)
    n8 = [jnp.where(blk8, m, 0.0) for m in n_mats]
    n8_2 = [mm(m, m) for m in n8]
    t = [eye + m for m in n8]
    t = [ti + mm(ti, m2) for ti, m2 in zip(t, n8_2)]
    n8_4 = [mm(m2, m2) for m2 in n8_2]
    t = [ti + mm(ti, m4) for ti, m4 in zip(t, n8_4)]
    shift = 3
    while (1 << shift) < size:
        inner, outer = same_block(shift), same_block(shift + 1)
        off = [jnp.where(inner, 0.0, jnp.where(outer, m, 0.0)) for m in n_mats]
        left = [mm(ti, oi) for ti, oi in zip(t, off)]
        t = [ti + mm(li, ti) for ti, li in zip(t, left)]
        shift += 1
    return t


def _wkv_kernel(r_ref, lw_ref, k_ref, v_ref, a_ref, g_ref, hp_ref, s0_ref,
                z_out, s_out, s_scr, *, bb, tl, c, prec):
    n = RW_HEAD
    nheads = s_scr.shape[1]
    nchunk = tl // c

    @pl.when(pl.program_id(2) == 0)
    def _():
        s_scr[...] = s0_ref[...]

    row = lax.broadcasted_iota(jnp.int32, (c, c), 0)
    col = lax.broadcasted_iota(jnp.int32, (c, c), 1)
    tril = jnp.where(row >= col, 1.0, 0.0).astype(F32)
    eye = jnp.where(row == col, 1.0, 0.0).astype(F32)

    def same_block(shift):
        return (row >> shift) == (col >> shift)

    row2 = lax.broadcasted_iota(jnp.int32, (2 * c, c), 0)
    col2 = lax.broadcasted_iota(jnp.int32, (2 * c, c), 1)
    mask2 = col2 <= jnp.where(row2 < c, row2 - 1, row2 - c)
    k_k = hp_ref[0:1, :]
    k_a = hp_ref[1:2, :]
    r_k = hp_ref[2:3, :]
    ln_w = hp_ref[3:4, :]
    ln_b = hp_ref[4:5, :]

    chains = [(s, h) for s in range(bb) for h in range(nheads)]

    def chunk(ci, carry):
        rows = pl.ds(pl.multiple_of(ci * c, c), c)
        lw = [lw_ref[s, rows, :] for s in range(bb)]
        cum = [_dot(tril, x, HIGHEST) for x in lw]
        seq = []
        for s in range(bb):
            r = r_ref[s, rows, :]
            k = k_ref[s, rows, :]
            a = a_ref[s, rows, :]
            cum_end = cum[s][c - 1:c, :]
            e_neg = jnp.exp(-cum[s])
            e_end = jnp.exp(cum_end - cum[s])
            k_mod = k * (1.0 + (a - 1.0) * k_a)
            seq.append(dict(
                a=a, v=v_ref[s, rows, :], kk_raw=k * k_k,
                r_t=r * jnp.exp(cum[s]), e_neg=e_neg, e_end=e_end,
                e_prev=jnp.exp(cum[s] - lw[s]), g_end=jnp.exp(cum_end),
                k_t=k_mod * e_neg, k_end=k_mod * e_end, rk=r * k_mod * r_k))
        ch = []
        for s, h in chains:
            q = seq[s]
            sl = slice(h * n, (h + 1) * n)
            kk = q["kk_raw"][:, sl]
            kk = kk * lax.rsqrt(jnp.maximum(jnp.sum(kk * kk, -1, keepdims=True), 1e-24))
            b_h = kk * q["a"][:, sl]
            a_t = -kk * q["e_prev"][:, sl]
            ch.append(dict(
                ar=jnp.concatenate([a_t, q["r_t"][:, sl]], axis=0),
                b_t=b_h * q["e_neg"][:, sl], k_t=q["k_t"][:, sl],
                kb_end=jnp.concatenate([q["k_end"][:, sl], b_h * q["e_end"][:, sl]], axis=0),
                v=q["v"][:, sl], g_end=q["g_end"][:, sl],
                bonus=jnp.sum(q["rk"][:, sl], -1, keepdims=True) * q["v"][:, sl]))
        ab = [jnp.where(mask2, _dot_nt(x["ar"], x["b_t"], prec), 0.0) for x in ch]
        ak = [jnp.where(mask2, _dot_nt(x["ar"], x["k_t"], prec), 0.0) for x in ch]
        t_inv = _unit_lower_inverse([m[:c] for m in ab], eye, same_block, c,
                                    lambda x, y: _dot(x, y, prec))
        s0 = [s_scr[s, h] for s, h in chains]
        x1 = [_dot_nt(x["ar"], si, prec) for x, si in zip(ch, s0)]
        x1 = [xi + _dot(m, x["v"], prec) for xi, m, x in zip(x1, ak, ch)]
        u = [_dot(ti, xi[:c], prec) for ti, xi in zip(t_inv, x1)]
        o = [xi[c:] + _dot(m[c:], ui, prec) for xi, m, ui in zip(x1, ab, u)]
        s_new = [si * x["g_end"] + _dot_tn(jnp.concatenate([x["v"], ui], axis=0), x["kb_end"], prec)
                 for x, si, ui in zip(ch, s0, u)]
        for (s, h), sn in zip(chains, s_new):
            s_scr[s, h] = sn
        normed = []
        for oi in o:
            mean = jnp.mean(oi, -1, keepdims=True)
            var = jnp.mean(jnp.square(oi - mean), -1, keepdims=True)
            normed.append((oi - mean) * lax.rsqrt(var + RW_GN_EPS))
        for s in range(bb):
            idx = [i for i, (si, _) in enumerate(chains) if si == s]
            o_all = jnp.concatenate([normed[i] for i in idx], axis=-1)
            bonus_all = jnp.concatenate([ch[i]["bonus"] for i in idx], axis=-1)
            z_out[s, rows, :] = (o_all * ln_w + ln_b + bonus_all) * g_ref[s, rows, :]
        return carry

    lax.fori_loop(0, nchunk, chunk, 0)

    @pl.when(pl.program_id(2) == pl.num_programs(2) - 1)
    def _():
        s_out[...] = s_scr[...]


def _half_sums(x, lo):
    s_lo = jnp.sum(jnp.where(lo, x, 0.0), -1, keepdims=True)
    s_hi = jnp.sum(jnp.where(lo, 0.0, x), -1, keepdims=True)
    return jnp.where(lo, s_lo, s_hi)


def _wkv_pair_kernel(r_ref, lw_ref, k_ref, v_ref, a_ref, g_ref, hp_ref, s0_ref,
                     z_out, s_out, s_scr, *, bb, tl, c, prec):
    n = RW_HEAD
    npair = r_ref.shape[-1] // LANES
    nchunk = tl // c
    units = [(s, q) for s in range(bb) for q in range(npair)]

    @pl.when(pl.program_id(2) == 0)
    def _():
        for s, q in units:
            s_scr[s, q] = jnp.concatenate([s0_ref[s, 2 * q], s0_ref[s, 2 * q + 1]], axis=-1)

    def iota2(shape):
        return (lax.broadcasted_iota(jnp.int32, shape, 0),
                lax.broadcasted_iota(jnp.int32, shape, 1))

    row, lane = iota2((c, LANES))
    col = lane & (n - 1)
    lo_c = lane < n
    eye = jnp.where(row == col, 1.0, 0.0).astype(F32)
    row2, lane2 = iota2((2 * c, LANES))
    mask2 = (lane2 & (n - 1)) <= jnp.where(row2 < c, row2 - 1, row2 - c)
    rowt, colt = iota2((c, c))
    tril = jnp.where(rowt >= colt, 1.0, 0.0).astype(F32)
    lo_s = iota2((n, LANES))[1] < n

    def same_block(shift):
        return (row >> shift) == (col >> shift)

    def expand(y):
        rows = y.shape[0]
        r_e, l_e = iota2((2 * rows, LANES))
        same_head = (r_e >> (rows.bit_length() - 1)) == (l_e >> (n.bit_length() - 1))
        return jnp.where(same_head, jnp.concatenate([y, y], axis=0), 0.0)

    def mm(x, y):
        return _dot(x, expand(y), prec)

    k_k = hp_ref[0:1, :]
    k_a = hp_ref[1:2, :]
    r_k = hp_ref[2:3, :]
    ln_w = hp_ref[3:4, :]
    ln_b = hp_ref[4:5, :]

    def chunk(ci, carry):
        rows = pl.ds(pl.multiple_of(ci * c, c), c)
        lw = [lw_ref[s, rows, :] for s in range(bb)]
        cum = [_dot(tril, x, HIGHEST) for x in lw]
        seq = []
        for s in range(bb):
            r = r_ref[s, rows, :]
            k = k_ref[s, rows, :]
            a = a_ref[s, rows, :]
            cum_end = cum[s][c - 1:c, :]
            e_neg = jnp.exp(-cum[s])
            e_end = jnp.exp(cum_end - cum[s])
            k_mod = k * (1.0 + (a - 1.0) * k_a)
            seq.append(dict(
                a=a, v=v_ref[s, rows, :], kk_raw=k * k_k,
                r_t=r * jnp.exp(cum[s]), e_neg=e_neg, e_end=e_end,
                e_prev=jnp.exp(cum[s] - lw[s]), g_end=jnp.exp(cum_end),
                k_t=k_mod * e_neg, k_end=k_mod * e_end, rk=r * k_mod * r_k))
        un = []
        for s, q in units:
            p = seq[s]
            sl = slice(q * LANES, (q + 1) * LANES)
            kk = p["kk_raw"][:, sl]
            kk = kk * lax.rsqrt(jnp.maximum(_half_sums(kk * kk, lo_c), 1e-24))
            b_h = kk * p["a"][:, sl]
            a_t = -kk * p["e_prev"][:, sl]
            un.append(dict(
                ar=jnp.concatenate([a_t, p["r_t"][:, sl]], axis=0),
                bk=jnp.concatenate([expand(b_h * p["e_neg"][:, sl]), expand(p["k_t"][:, sl])],
                                   axis=0),
                kb_end=jnp.concatenate([p["k_end"][:, sl], b_h * p["e_end"][:, sl]], axis=0),
                v=p["v"][:, sl], g_end=p["g_end"][:, sl],
                bonus=_half_sums(p["rk"][:, sl], lo_c) * p["v"][:, sl]))
        g_all = [_dot_nt(x["ar"], x["bk"], prec) for x in un]
        ab = [jnp.where(mask2, m[:, :2 * c], 0.0) for m in g_all]
        ak = [jnp.where(mask2, m[:, 2 * c:], 0.0) for m in g_all]
        t_inv = _unit_lower_inverse([m[:c] for m in ab], eye, same_block, c, mm)
        s0 = [s_scr[s, q] for s, q in units]
        x1 = [_dot_nt(x["ar"], expand(si), prec) for x, si in zip(un, s0)]
        x1 = [xi + mm(m, x["v"]) for xi, m, x in zip(x1, ak, un)]
        u = [mm(ti, xi[:c]) for ti, xi in zip(t_inv, x1)]
        o = [xi[c:] + mm(m[c:], ui) for xi, m, ui in zip(x1, ab, u)]
        w = [_dot_tn(jnp.concatenate([x["v"], ui], axis=0), x["kb_end"], prec)
             for x, ui in zip(un, u)]
        for (s, q), si, wi, x in zip(units, s0, w, un):
            s_scr[s, q] = si * x["g_end"] + jnp.where(lo_s, wi[:n], wi[n:])
        normed = []
        for oi in o:
            mean = _half_sums(oi, lo_c) * (1.0 / n)
            var = _half_sums(jnp.square(oi - mean), lo_c) * (1.0 / n)
            normed.append((oi - mean) * lax.rsqrt(var + RW_GN_EPS))
        for s in range(bb):
            idx = [i for i, (si, _) in enumerate(units) if si == s]
            o_all = jnp.concatenate([normed[i] for i in idx], axis=-1)
            bonus_all = jnp.concatenate([un[i]["bonus"] for i in idx], axis=-1)
            z_out[s, rows, :] = (o_all * ln_w + ln_b + bonus_all) * g_ref[s, rows, :]
        return carry

    lax.fori_loop(0, nchunk, chunk, 0)

    @pl.when(pl.program_id(2) == pl.num_programs(2) - 1)
    def _():
        for s, q in units:
            s_out[s, 2 * q] = s_scr[s, q][:, :n]
            s_out[s, 2 * q + 1] = s_scr[s, q][:, n:]


def wkv_scan(r, lw, k, v, a, g, hp, s0, *, bb, tl, c, nheads, prec):
    bsz, L, d = r.shape
    width = nheads * RW_HEAD
    seq = pl.BlockSpec((bb, tl, width), lambda b, h, l: (b, l, h))
    st = pl.BlockSpec((bb, nheads, RW_HEAD, RW_HEAD), lambda b, h, l: (b, h, 0, 0))
    paired = 2 * c == LANES and nheads % 2 == 0
    body = _wkv_pair_kernel if paired else _wkv_kernel
    scr = (bb, nheads // 2, RW_HEAD, LANES) if paired else (bb, nheads, RW_HEAD, RW_HEAD)
    return pl.pallas_call(
        functools.partial(body, bb=bb, tl=tl, c=c, prec=prec),
        grid=(bsz // bb, d // width, L // tl),
        in_specs=[seq] * 6 + [pl.BlockSpec((SUBLANES, width), lambda b, h, l: (0, h)), st],
        out_specs=[seq, st],
        out_shape=[jax.ShapeDtypeStruct((bsz, L, d), F32),
                   jax.ShapeDtypeStruct(s0.shape, F32)],
        scratch_shapes=[pltpu.VMEM(scr, F32)],
        compiler_params=_cparams(("parallel", "parallel", "arbitrary")),
        name="wkv_scan",
    )(r, lw, k, v, a, g, hp, s0)


def rwkv_layer(xn, shift, wkv, v_first, prm, cfg):
    r, lw, k, v, a, g = rwkv_proj(xn, shift, v_first, prm, bb=cfg["proj_bb"], tl=cfg["proj_tl"])
    z, wkv_new = wkv_scan(r, lw, k, v, a, g, prm["hp"], wkv,
                          bb=cfg["scan_bb"], tl=cfg["wkv_tl"], c=cfg["wkv_c"],
                          nheads=cfg["wkv_heads"], prec=SCAN_PREC)
    return (z, prm["w_o"]), xn[:, -1], wkv_new, (v if v_first is None else v_first)


def _prep_rwkv(mu, w0, w1, w2, a0, a1, a2, g1, g2, k_k, k_a, r_k, w_rkv, w_o, lnx_w, lnx_b,
               v_lora=None):
    d = D_MODEL
    zero = jnp.zeros((d,), F32)
    prm = dict(mu=mu, w_rkv=w_rkv.astype(BF16), w_o=w_o.astype(BF16),
               w1=w1.astype(BF16), w2=w2.astype(BF16), a1=a1.astype(BF16), a2=a2.astype(BF16),
               g1=g1.astype(BF16), g2=g2.astype(BF16))
    v0 = zero
    if v_lora is not None:
        v0, v1, v2 = v_lora
        prm["v1"] = v1.astype(BF16)
        prm["v2"] = v2.astype(BF16)
    prm["b0"] = jnp.stack([w0, a0, v0] + [zero] * 5)
    prm["hp"] = jnp.stack([k_k, k_a, r_k.reshape(d), lnx_w, lnx_b] + [zero] * 3)
    return prm


S5_CH = 8
S5_CL = S5_LANES // S5_CH


def _s5_kernel(u_ref, w_ref, cm_ref, ar_ref, ai_ref, d_ref, h0r_ref, h0i_ref,
               g_out, hr_out, hi_out, hs, hc, *, bsz, tr):
    i = pl.program_id(1)
    cl = S5_CL

    @pl.when(i == 0)
    def _():
        hc[:, :cl] = h0r_ref[...]
        hc[:, cl:] = h0i_ref[...]

    u = u_ref[...]
    hs[...] = _dot(u, w_ref[...])
    ar = jnp.broadcast_to(ar_ref[...], (bsz, cl))
    ai = jnp.broadcast_to(ai_ref[...], (bsz, cl))

    def step(t, carry):
        hr, hi = carry
        rows = pl.ds(pl.multiple_of(t * bsz, bsz), bsz)
        nr = ar * hr - ai * hi + hs[rows, :cl]
        ni = ar * hi + ai * hr + hs[rows, cl:]
        hs[rows, :cl] = nr
        hs[rows, cl:] = ni
        return nr, ni

    nstep = tr // bsz
    hr, hi = lax.fori_loop(0, nstep, step, (hc[:, :cl], hc[:, cl:]),
                           unroll=min(nstep, SUBLANES))
    hc[:, :cl] = hr
    hc[:, cl:] = hi
    y = _dot(hs[...], cm_ref[...]) + d_ref[...] * u
    g_out[...] = jax.nn.gelu(y)

    @pl.when(i == pl.num_programs(1) - 1)
    def _():
        hr_out[...] = hr
        hi_out[...] = hi


def s5_scan(u_tm, h0r, h0i, prm, *, bsz, tr):
    rows, d = u_tm.shape
    cl = S5_CL
    st = pl.BlockSpec((bsz, cl), lambda c, i: (0, c))
    return pl.pallas_call(
        functools.partial(_s5_kernel, bsz=bsz, tr=tr),
        grid=(S5_CH, rows // tr),
        in_specs=[
            pl.BlockSpec((tr, LANES), lambda c, i: (i, c)),
            pl.BlockSpec((None, LANES, 2 * cl), lambda c, i: (c, 0, 0)),
            pl.BlockSpec((None, 2 * cl, LANES), lambda c, i: (c, 0, 0)),
            pl.BlockSpec((None, 1, cl), lambda c, i: (c, 0, 0)),
            pl.BlockSpec((None, 1, cl), lambda c, i: (c, 0, 0)),
            pl.BlockSpec((1, LANES), lambda c, i: (0, c)),
            st, st,
        ],
        out_specs=[pl.BlockSpec((tr, LANES), lambda c, i: (i, c)), st, st],
        out_shape=[jax.ShapeDtypeStruct((rows, d), F32),
                   jax.ShapeDtypeStruct((bsz, S5_LANES), F32),
                   jax.ShapeDtypeStruct((bsz, S5_LANES), F32)],
        scratch_shapes=[pltpu.VMEM((tr, 2 * cl), F32), pltpu.VMEM((bsz, 2 * cl), F32)],
        compiler_params=_cparams(("parallel", "arbitrary")),
        name="s5_scan",
    )(u_tm, prm["w_in"], prm["c_out"], prm["ab_re"], prm["ab_im"], prm["d"], h0r, h0i)


def _prep_s5(a_re, a_im, log_dt, b_re, b_im, c_re, c_im, d_skip, glu_v, glu_g):
    dt = jnp.exp(log_dt.astype(F32))[:, None]
    lr, li = a_re.astype(F32), a_im.astype(F32)
    mag = jnp.exp(lr * dt)
    ab_re, ab_im = mag * jnp.cos(li * dt), mag * jnp.sin(li * dt)
    den = lr * lr + li * li
    q_re = ((ab_re - 1.0) * lr + ab_im * li) / den
    q_im = (ab_im * lr - (ab_re - 1.0) * li) / den
    bb_re = q_re[..., None] * b_re - q_im[..., None] * b_im
    bb_im = q_re[..., None] * b_im + q_im[..., None] * b_re
    gl = S5_GROUPS // S5_CH
    eye = jnp.eye(gl, dtype=F32)

    def in_blocks(bb):
        t = bb.reshape(S5_CH, gl, S5_STATE, S5_GROUP)
        return jnp.einsum('cgph,gk->cghkp', t, eye).reshape(S5_CH, LANES, S5_CL)

    def out_blocks(cc):
        t = cc.reshape(S5_CH, gl, S5_GROUP, S5_STATE)
        return jnp.einsum('cghp,gk->ckpgh', t, eye).reshape(S5_CH, S5_CL, LANES)

    return dict(
        w_in=jnp.concatenate([in_blocks(bb_re), in_blocks(bb_im)], axis=-1),
        c_out=jnp.concatenate([out_blocks(c_re.astype(F32)), -out_blocks(c_im.astype(F32))],
                              axis=1),
        ab_re=ab_re.reshape(S5_CH, 1, S5_CL), ab_im=ab_im.reshape(S5_CH, 1, S5_CL),
        d=d_skip.reshape(1, D_MODEL),
        glu_v=glu_v.astype(BF16), glu_g=glu_g.astype(BF16))


def s5_layer(xn, h_re0, h_im0, prm, cfg):
    bsz = h_re0.shape[0]
    if cfg["time_major"]:
        u_tm = xn
    else:
        _, L, d = xn.shape
        u_tm = jnp.swapaxes(xn, 0, 1).reshape(L * bsz, d)
    g, hr, hi = s5_scan(u_tm, h_re0.reshape(bsz, S5_LANES), h_im0.reshape(bsz, S5_LANES), prm,
                        bsz=bsz, tr=cfg["s5_tr"])
    y = glu_rows(g, prm["glu_v"], prm["glu_g"], tm=cfg["tm"])
    if not cfg["time_major"]:
        y = jnp.swapaxes(y.reshape(L, bsz, d), 0, 1)
    return ((y, None), hr.reshape(bsz, S5_GROUPS, S5_STATE), hi.reshape(bsz, S5_GROUPS, S5_STATE))


def _mamba_in_kernel(x_ref, cp_ref, wz_ref, wx_ref, wdt_ref, cw_ref, cb_ref, dtb_ref,
                     z_out, xs_out, b_out, c_out, dt_out, nc_out, carry, *, bb, tl):
    d = x_ref.shape[-1]
    m = bb * tl
    cd = M_CONV_DIM
    l = pl.program_id(1)

    @pl.when(l == 0)
    def _():
        carry[:, :SUBLANES - (M_CONV - 1), :] = jnp.zeros(
            (bb, SUBLANES - (M_CONV - 1), cd), F32)
        carry[:, SUBLANES - (M_CONV - 1):, :] = cp_ref[...]

    x = x_ref[...].reshape(m, d).astype(BF16)
    z_out[...] = _dot(x, wz_ref[...]).reshape(bb, tl, M_INNER)
    dt_raw = _dot(x, wdt_ref[...])[:, :M_HEADS]
    dt_out[...] = _softplus(dt_raw + dtb_ref[...]).reshape(bb, tl, M_HEADS)
    xbc = _dot(x, wx_ref[...]).reshape(bb, tl, cd)
    full = jnp.concatenate([carry[...], xbc], axis=1).reshape(bb * (tl + SUBLANES), cd)
    conv = cb_ref[...] + xbc.reshape(m, cd) * cw_ref[M_CONV - 1:M_CONV, :]
    for j in range(1, M_CONV):
        sh = pltpu.roll(full, j, 0).reshape(bb, tl + SUBLANES, cd)[:, SUBLANES:, :]
        conv = conv + sh.reshape(m, cd) * cw_ref[M_CONV - 1 - j:M_CONV - j, :]
    act = conv * jax.nn.sigmoid(conv)
    xs_out[...] = act[:, :M_INNER].reshape(bb, tl, M_INNER)
    nbc = M_GROUPS * M_STATE
    b_out[...] = act[:, M_INNER:M_INNER + nbc].reshape(bb, tl, nbc)
    c_out[...] = act[:, M_INNER + nbc:].reshape(bb, tl, nbc)
    full3 = full.reshape(bb, tl + SUBLANES, cd)
    carry[...] = full3[:, tl:, :]

    @pl.when(l == pl.num_programs(1) - 1)
    def _():
        nc_out[...] = full3[:, tl + SUBLANES - (M_CONV - 1):, :]


def mamba_in(xn, conv_prev, prm, *, bb, tl):
    bsz, L, d = xn.shape
    cd = M_CONV_DIM
    nbc = M_GROUPS * M_STATE
    seq = lambda w: pl.BlockSpec((bb, tl, w), lambda b, l: (b, l, 0))
    cps = pl.BlockSpec((bb, M_CONV - 1, cd), lambda b, l: (b, 0, 0))
    return pl.pallas_call(
        functools.partial(_mamba_in_kernel, bb=bb, tl=tl),
        grid=(bsz // bb, L // tl),
        in_specs=[seq(d), cps, _resident((d, M_INNER)), _resident((d, cd)),
                  _resident((d, LANES)),
                  _full((M_CONV, cd)), _full((1, cd)), _full((1, M_HEADS))],
        out_specs=[seq(M_INNER), seq(M_INNER), seq(nbc), seq(nbc), seq(M_HEADS), cps],
        out_shape=[jax.ShapeDtypeStruct((bsz, L, M_INNER), F32),
                   jax.ShapeDtypeStruct((bsz, L, M_INNER), F32),
                   jax.ShapeDtypeStruct((bsz, L, nbc), F32),
                   jax.ShapeDtypeStruct((bsz, L, nbc), F32),
                   jax.ShapeDtypeStruct((bsz, L, M_HEADS), F32),
                   jax.ShapeDtypeStruct((bsz, M_CONV - 1, cd), F32)],
        scratch_shapes=[pltpu.VMEM((bb, SUBLANES, cd), F32)],
        compiler_params=_cparams(("parallel", "arbitrary")),
        name="mamba_in",
    )(xn, conv_prev, prm["wz"], prm["wx"], prm["wdt"], prm["conv_w"], prm["conv_b"],
      prm["dt_bias"])


def _ssd_kernel(xs_ref, b_ref, c_ref, z_ref, dt_ref, hv_ref, ex_ref, nw_ref, s0_ref,
                y_out, s_out, s_scr, *, bb, tl, c):
    p, n = M_HEADDIM, M_STATE
    hpg = M_HEADS // M_GROUPS
    gw = M_INNER // M_GROUPS
    nchunk = tl // c

    @pl.when(pl.program_id(1) == 0)
    def _():
        s_scr[...] = s0_ref[...]

    row = lax.broadcasted_iota(jnp.int32, (c, c), 0)
    col = lax.broadcasted_iota(jnp.int32, (c, c), 1)
    incl = row >= col
    tril = jnp.where(incl, 1.0, 0.0).astype(F32)
    triu = jnp.where(row <= col, 1.0, 0.0).astype(F32)
    a_neg = hv_ref[0:1, :]
    d_exp = nw_ref[1:2, :]
    norm_w = nw_ref[0:1, :]
    expand = ex_ref[...]

    def chunk(i, carry):
        s = i // nchunk
        ci = i - s * nchunk
        rows = pl.ds(pl.multiple_of(ci * c, c), c)
        xs = xs_ref[s, rows, :]
        bm = b_ref[s, rows, :]
        cm = c_ref[s, rows, :]
        dt = dt_ref[s, rows, :]
        a = dt * a_neg
        cum = _dot(tril, a, HIGHEST)
        cum_t = _dot_tn(a, triu, HIGHEST)
        cum_end = cum[c - 1:c, :]
        e_end = jnp.exp(cum_end)
        dt_x = _dot(dt, expand)
        dec_x = _dot(jnp.exp(cum_end - cum), expand)
        ecum_x = _dot(jnp.exp(cum), expand)
        xdt = xs * dt_x
        xend = xdt * dec_x
        yd, yo = [], []
        for gi in range(M_GROUPS):
            b_g = bm[:, gi * n:(gi + 1) * n]
            c_g = cm[:, gi * n:(gi + 1) * n]
            cb = _dot_nt(c_g, b_g)
            for hh in range(hpg):
                h = gi * hpg + hh
                sl = slice(h * p, (h + 1) * p)
                seg = cum[:, h:h + 1] - cum_t[h:h + 1, :]
                lmat = jnp.where(incl, jnp.exp(jnp.where(incl, seg, 0.0)), 0.0)
                s_h = s_scr[s, h]
                yd.append(_dot(cb * lmat, xdt[:, sl]))
                yo.append(_dot_nt(c_g, s_h))
                s_scr[s, h] = s_h * e_end[:, h:h + 1] + _dot_tn(xend[:, sl], b_g)
        y = (jnp.concatenate(yd, axis=-1) + jnp.concatenate(yo, axis=-1) * ecum_x
             + d_exp * xs)
        zz = z_ref[s, rows, :]
        y = y * (zz * jax.nn.sigmoid(zz))
        outs = []
        for gi in range(M_GROUPS):
            yg = y[:, gi * gw:(gi + 1) * gw]
            outs.append(yg * lax.rsqrt(jnp.mean(yg * yg, -1, keepdims=True) + EPS))
        y_out[s, rows, :] = jnp.concatenate(outs, axis=-1) * norm_w
        return carry

    lax.fori_loop(0, bb * nchunk, chunk, 0)

    @pl.when(pl.program_id(1) == pl.num_programs(1) - 1)
    def _():
        s_out[...] = s_scr[...]


def ssd_scan(xs, bm, cm, z, dt, prm, s0, *, bb, tl, c):
    bsz, L, _ = xs.shape
    nbc = M_GROUPS * M_STATE
    seq = lambda w: pl.BlockSpec((bb, tl, w), lambda b, l: (b, l, 0))
    st = pl.BlockSpec((bb, M_HEADS, M_HEADDIM, M_STATE), lambda b, l: (b, 0, 0, 0))
    return pl.pallas_call(
        functools.partial(_ssd_kernel, bb=bb, tl=tl, c=c),
        grid=(bsz // bb, L // tl),
        in_specs=[seq(M_INNER), seq(nbc), seq(nbc), seq(M_INNER), seq(M_HEADS),
                  _full((SUBLANES, M_HEADS)), _full((M_HEADS, M_INNER)),
                  _full((SUBLANES, M_INNER)), st],
        out_specs=[seq(M_INNER), st],
        out_shape=[jax.ShapeDtypeStruct((bsz, L, M_INNER), F32),
                   jax.ShapeDtypeStruct(s0.shape, F32)],
        scratch_shapes=[pltpu.VMEM((bb, M_HEADS, M_HEADDIM, M_STATE), F32)],
        compiler_params=_cparams(("parallel", "arbitrary")),
        name="ssd_scan",
    )(xs, bm, cm, z, dt, prm["hv"], prm["expand"], prm["nw"], s0)


def _prep_mamba(in_proj, conv_w, conv_b, dt_bias, a_log, d_skip, norm_w, out_proj):
    d = D_MODEL
    wdt = jnp.zeros((d, LANES), F32).at[:, :M_HEADS].set(in_proj[:, M_INNER + M_CONV_DIM:])
    expand = jnp.repeat(jnp.eye(M_HEADS, dtype=F32), M_HEADDIM, axis=1)
    zrow_h = jnp.zeros((M_HEADS,), F32)
    zrow_i = jnp.zeros((M_INNER,), F32)
    return dict(
        wz=in_proj[:, :M_INNER].astype(BF16),
        wx=in_proj[:, M_INNER:M_INNER + M_CONV_DIM].astype(BF16),
        wdt=wdt.astype(BF16),
        conv_w=conv_w, conv_b=conv_b.reshape(1, M_CONV_DIM),
        dt_bias=dt_bias.reshape(1, M_HEADS),
        hv=jnp.stack([-jnp.exp(a_log.astype(F32))] + [zrow_h] * 7),
        expand=expand,
        nw=jnp.stack([norm_w, jnp.repeat(d_skip, M_HEADDIM)] + [zrow_i] * 6),
        out_proj=out_proj.astype(BF16))


def mamba_layer(xn, conv_prev, ssm_prev, prm, cfg):
    bsz, L, d = xn.shape
    z, xs, bm, cm, dt, new_conv = mamba_in(xn, conv_prev, prm, bb=cfg["mamba_bb"],
                                           tl=cfg["mamba_tl"])
    y, ssm_new = ssd_scan(xs, bm, cm, z, dt, prm, ssm_prev,
                          bb=cfg["ssd_bb"], tl=cfg["ssd_tl"], c=cfg["ssd_c"])
    return (y, prm["out_proj"]), new_conv, ssm_new


def _config(bsz, L):
    if L >= 512:
        return dict(tm=512, time_major=True, proj_bb=1, proj_tl=256,
                    scan_bb=min(bsz, 8), wkv_tl=256, wkv_c=64, wkv_heads=4,
                    mamba_bb=1, mamba_tl=256,
                    ssd_bb=1, ssd_tl=256, ssd_c=M_CHUNK, s5_tr=512)
    rows = bsz * L
    tm = min(512, rows)
    return dict(tm=tm, time_major=False, proj_bb=min(bsz, 256 // L), proj_tl=L,
                scan_bb=min(bsz, 16), wkv_tl=L, wkv_c=L, wkv_heads=2,
                mamba_bb=min(bsz, 256 // L), mamba_tl=L,
                ssd_bb=min(bsz, 4), ssd_tl=L, ssd_c=min(M_CHUNK, L), s5_tr=rows)


def _trunk(x, p, states, layer_prms, shared, cfg):
    bsz, L, d = x.shape
    n = bsz * L
    tm = cfg["tm"]
    xn = rmsnorm_rows(x.reshape(n, d), shared["norm_mix"][0], tm).reshape(bsz, L, d)
    h = x.reshape(n, d)
    v_first = None
    new_states = []
    for i in range(DEPTH):
        st_a, st_b = states[2 * i], states[2 * i + 1]
        prm = layer_prms[i]
        kind = i % 3
        if kind == 0:
            (mix, wo), n_a, n_b, v_first = rwkv_layer(xn, st_a, st_b, v_first, prm, cfg)
        elif kind == 1:
            (mix, wo), n_a, n_b = s5_layer(xn, st_a, st_b, prm, cfg)
        else:
            (mix, wo), n_a, n_b = mamba_layer(xn, st_a, st_b, prm, cfg)
        new_states += [n_a, n_b]
        nxt = shared["norm_mix"][i + 1] if i + 1 < DEPTH else shared["final_norm"]
        mix_tm = cfg["time_major"] and kind == 1
        xn_tm = cfg["time_major"] and (i + 1) % 3 == 1 and i + 1 < DEPTH
        h, xn = ffn_ple(h, mix.reshape(n, -1), wo, p[i].reshape(n, PLE_DIM),
                        shared["norm_ffn"][i], shared["norm_ple"][i], nxt,
                        shared["ffn_w1"][i], shared["ffn_w3"][i], shared["ffn_w2"][i],
                        shared["ple_gate"][i], shared["ple_proj"][i], tm=tm, seq_len=L,
                        mix_time_major=mix_tm, xn_time_major=xn_tm)
        xn = xn.reshape(n, d) if xn_tm else xn.reshape(bsz, L, d)
    return xn, new_states


def kernel(x_prompt, x_sample, p_prompt, p_sample, state_l0_shift, state_l0_wkv, state_l1_s5_re, state_l1_s5_im, state_l2_conv, state_l2_ssm, state_l3_shift, state_l3_wkv, l0_mu, l0_w0, l0_w1, l0_w2, l0_a0, l0_a1, l0_a2, l0_g1, l0_g2, l0_k_k, l0_k_a, l0_r_k, l0_w_rkv, l0_w_o, l0_lnx_w, l0_lnx_b, l1_a_re, l1_a_im, l1_log_dt, l1_b_re, l1_b_im, l1_c_re, l1_c_im, l1_d, l1_glu_v, l1_glu_g, l2_in_proj, l2_conv_w, l2_conv_b, l2_dt_bias, l2_a_log, l2_d, l2_norm_w, l2_out_proj, l3_mu, l3_w0, l3_w1, l3_w2, l3_a0, l3_a1, l3_a2, l3_g1, l3_g2, l3_k_k, l3_k_a, l3_r_k, l3_w_rkv, l3_w_o, l3_lnx_w, l3_lnx_b, l3_v0, l3_v1, l3_v2, norm_mix, norm_ffn, norm_ple, ffn_w1, ffn_w3, ffn_w2, ple_proj, ple_gate, final_norm):
    layer_prms = (
        _prep_rwkv(l0_mu, l0_w0, l0_w1, l0_w2, l0_a0, l0_a1, l0_a2, l0_g1, l0_g2,
                   l0_k_k, l0_k_a, l0_r_k, l0_w_rkv, l0_w_o, l0_lnx_w, l0_lnx_b),
        _prep_s5(l1_a_re, l1_a_im, l1_log_dt, l1_b_re, l1_b_im, l1_c_re, l1_c_im,
                 l1_d, l1_glu_v, l1_glu_g),
        _prep_mamba(l2_in_proj, l2_conv_w, l2_conv_b, l2_dt_bias, l2_a_log, l2_d,
                    l2_norm_w, l2_out_proj),
        _prep_rwkv(l3_mu, l3_w0, l3_w1, l3_w2, l3_a0, l3_a1, l3_a2, l3_g1, l3_g2,
                   l3_k_k, l3_k_a, l3_r_k, l3_w_rkv, l3_w_o, l3_lnx_w, l3_lnx_b,
                   (l3_v0, l3_v1, l3_v2)),
    )
    shared = dict(norm_mix=norm_mix, norm_ffn=norm_ffn, norm_ple=norm_ple,
                  final_norm=final_norm,
                  ffn_w1=ffn_w1.astype(BF16), ffn_w3=ffn_w3.astype(BF16),
                  ffn_w2=ffn_w2.astype(BF16), ple_gate=ple_gate.astype(BF16),
                  ple_proj=ple_proj.astype(BF16))
    sample_states = [state_l0_shift, state_l0_wkv, state_l1_s5_re, state_l1_s5_im,
                     state_l2_conv, state_l2_ssm, state_l3_shift, state_l3_wkv]
    bp = x_prompt.shape[0]
    prompt_states = [jnp.zeros((bp,) + s.shape[1:], F32) for s in sample_states]
    y_prompt, new_p = _trunk(x_prompt, p_prompt, prompt_states, layer_prms, shared,
                             _config(*x_prompt.shape[:2]))
    y_sample, new_s = _trunk(x_sample, p_sample, sample_states, layer_prms, shared,
                             _config(*x_sample.shape[:2]))
    return (y_prompt, y_sample, *new_p, *new_s)
```

```python
import functools
import math

import jax
import jax.numpy as jnp
from jax import lax
from jax.experimental import pallas as pl
from jax.experimental.pallas import tpu as pltpu

F32 = jnp.float32
BF16 = jnp.bfloat16
HIGHEST = lax.Precision.HIGHEST

D_MODEL = 1024
DEPTH = 4
PLE_DIM = 256
EPS = 1e-6
D_FF = 2816
RW_HEAD = 64
RW_HEADS = D_MODEL // RW_HEAD
RW_GN_EPS = 64e-5
S5_GROUP = 16
S5_GROUPS = D_MODEL // S5_GROUP
S5_STATE = 64
S5_LANES = S5_GROUPS * S5_STATE
M_INNER = 2 * D_MODEL
M_HEADDIM = 64
M_HEADS = M_INNER // M_HEADDIM
M_STATE = 128
M_GROUPS = 4
M_CONV = 4
M_CHUNK = 64
M_CONV_DIM = M_INNER + 2 * M_GROUPS * M_STATE

LANES = 128
MXU_TILE = 256
SUBLANES = 8
VMEM_LIMIT = 56 * 1024 * 1024

SCAN_PREC = None


def _cparams(sem):
    return pltpu.CompilerParams(dimension_semantics=sem, vmem_limit_bytes=VMEM_LIMIT)


def _operands(a, b, prec):
    if prec is None:
        return a.astype(BF16), b.astype(BF16)
    return a, b


def _dot(a, b, prec=None):
    a, b = _operands(a, b, prec)
    return jnp.dot(a, b, preferred_element_type=F32, precision=prec)


def _dot_nt(a, b, prec=None):
    a, b = _operands(a, b, prec)
    return lax.dot_general(a, b, (((1,), (1,)), ((), ())),
                           preferred_element_type=F32, precision=prec)


def _dot_tn(a, b, prec=None):
    a, b = _operands(a, b, prec)
    return lax.dot_general(a, b, (((0,), (0,)), ((), ())),
                           preferred_element_type=F32, precision=prec)


def _rms(x, g):
    return x * lax.rsqrt(jnp.mean(x * x, axis=-1, keepdims=True) + EPS) * g


def _log_sigmoid(z):
    return -(jnp.maximum(-z, 0.0) + jnp.log1p(jnp.exp(-jnp.abs(z))))


def _softplus(z):
    return jnp.maximum(z, 0.0) + jnp.log1p(jnp.exp(-jnp.abs(z)))


def _full(shape):
    n = len(shape)
    return pl.BlockSpec(shape, lambda *_: (0,) * n)


def _resident(shape):
    n = len(shape)
    return pl.BlockSpec(shape, lambda *_: (0,) * n, pipeline_mode=pl.Buffered(1))


def _rmsnorm_kernel(x_ref, g_ref, o_ref):
    o_ref[...] = _rms(x_ref[...], g_ref[...])


def rmsnorm_rows(x, g, tm):
    n, d = x.shape
    return pl.pallas_call(
        _rmsnorm_kernel,
        grid=(n // tm,),
        in_specs=[pl.BlockSpec((tm, d), lambda i: (i, 0)), _full((1, d))],
        out_specs=pl.BlockSpec((tm, d), lambda i: (i, 0)),
        out_shape=jax.ShapeDtypeStruct((n, d), F32),
        compiler_params=_cparams(("parallel",)),
        name="rmsnorm",
    )(x, g.reshape(1, d))


def _ffn_kernel(*refs, project, mix_tm, xn_tm):
    if project:
        (h_ref, mix_ref, wo_ref, p_ref, nf_ref, np_ref, nn_ref, w1_ref, w3_ref, w2_ref,
         pg_ref, pp_ref, h_out, xn_out) = refs
    else:
        (h_ref, mix_ref, p_ref, nf_ref, np_ref, nn_ref, w1_ref, w3_ref, w2_ref,
         pg_ref, pp_ref, h_out, xn_out) = refs
    fb, ft, d = h_ref.shape
    tm = fb * ft
    mix = mix_ref[...]
    if mix_tm:
        mix = jnp.swapaxes(mix, 0, 1)
    mix = mix.reshape(tm, mix.shape[-1])
    y = _dot(mix, wo_ref[...]) if project else mix
    hin = h_ref[...].reshape(tm, d) + y
    hn = _rms(hin, nf_ref[...]).astype(BF16)
    acc = None
    for ci in range(D_FF // MXU_TILE):
        cs = slice(ci * MXU_TILE, (ci + 1) * MXU_TILE)
        a = _dot(hn, w1_ref[:, cs])
        b = _dot(hn, w3_ref[:, cs])
        t = (a * jax.nn.sigmoid(a) * b).astype(BF16)
        part = _dot(t, w2_ref[cs, :])
        acc = part if acc is None else acc + part
    h2 = hin + acc
    gate = jax.nn.sigmoid(_dot(_rms(h2, np_ref[...]).astype(BF16), pg_ref[...]))
    h3 = h2 + gate * _dot(p_ref[...].reshape(tm, PLE_DIM).astype(BF16), pp_ref[...])
    h_out[...] = h3.reshape(fb, ft, d)
    xn = _rms(h3, nn_ref[...]).reshape(fb, ft, d)
    xn_out[...] = jnp.swapaxes(xn, 0, 1) if xn_tm else xn


def ffn_ple(h, mix, wo, p_all, layer, shared, nn, *, fb, ft, mix_tm=False, xn_tm=False):
    bsz, L, d = h.shape
    km = mix.shape[-1]
    nlt = L // ft
    seq = lambda w: pl.BlockSpec((fb, ft, w), lambda i: (i // nlt, i % nlt, 0))
    tmaj = lambda w: pl.BlockSpec((ft, fb, w), lambda i: (i % nlt, 0, 0))
    lay = lambda *shape: pl.BlockSpec((None,) + shape, lambda i: (layer,) + (0,) * len(shape),
                                      pipeline_mode=pl.Buffered(1))
    ins = [h, mix]
    specs = [seq(d), tmaj(km) if mix_tm else seq(km)]
    if wo is not None:
        ins.append(wo)
        specs.append(_resident(wo.shape))
    ins += [p_all, shared["norm_ffn"], shared["norm_ple"], nn.reshape(1, d),
            shared["ffn_w1"], shared["ffn_w3"], shared["ffn_w2"],
            shared["ple_gate"], shared["ple_proj"]]
    specs += [pl.BlockSpec((None, fb, ft, PLE_DIM), lambda i: (layer, i // nlt, i % nlt, 0)),
              lay(1, d), lay(1, d), _full((1, d)),
              lay(d, D_FF), lay(d, D_FF), lay(D_FF, d), lay(d, d), lay(PLE_DIM, d)]
    xn_shape = (L, bsz, d) if xn_tm else (bsz, L, d)
    return pl.pallas_call(
        functools.partial(_ffn_kernel, project=wo is not None, mix_tm=mix_tm, xn_tm=xn_tm),
        grid=(bsz // fb * nlt,),
        in_specs=specs,
        out_specs=[seq(d), tmaj(d) if xn_tm else seq(d)],
        out_shape=[jax.ShapeDtypeStruct((bsz, L, d), F32), jax.ShapeDtypeStruct(xn_shape, F32)],
        compiler_params=_cparams(("parallel",)),
        name="ffn_ple",
    )(*ins)


def _glu_kernel(x_ref, wv_ref, wg_ref, o_ref):
    g = x_ref[...].astype(BF16)
    o_ref[...] = _dot(g, wv_ref[...]) * jax.nn.sigmoid(_dot(g, wg_ref[...]))


def glu_rows(x, wv, wg, *, tm):
    n, k = x.shape
    m = wv.shape[1]
    return pl.pallas_call(
        _glu_kernel,
        grid=(n // tm,),
        in_specs=[pl.BlockSpec((tm, k), lambda i: (i, 0)), _resident((k, m)),
                  _resident((k, m))],
        out_specs=pl.BlockSpec((tm, m), lambda i: (i, 0)),
        out_shape=jax.ShapeDtypeStruct((n, m), F32),
        compiler_params=_cparams(("parallel",)),
        name="glu_rows",
    )(x, wv, wg)


def _rwkv_proj_kernel(*refs, bb, tl, with_v):
    if with_v:
        (x_ref, sh_ref, vf_ref, mu_ref, b0_ref, wrkv_ref, w1_ref, w2_ref, a1_ref, a2_ref,
         g1_ref, g2_ref, v1_ref, v2_ref,
         r_out, lw_out, k_out, v_out, a_out, g_out, carry) = refs
    else:
        (x_ref, sh_ref, mu_ref, b0_ref, wrkv_ref, w1_ref, w2_ref, a1_ref, a2_ref,
         g1_ref, g2_ref,
         r_out, lw_out, k_out, v_out, a_out, g_out, carry) = refs
    d = x_ref.shape[-1]
    m = bb * tl

    @pl.when(pl.program_id(1) == 0)
    def _():
        carry[...] = sh_ref[...]

    x3 = x_ref[...]
    x = x3.reshape(m, d)
    prev = jnp.broadcast_to(carry[...], (bb, tl, d)).reshape(m, d)
    rolled = pltpu.roll(x, 1, 0)
    t_in_tile = lax.broadcasted_iota(jnp.int32, (m, d), 0) & (tl - 1)
    x_prev = jnp.where(t_in_tile == 0, prev, rolled)
    carry[...] = x_ref[:, tl - 1:tl, :]

    xx = x_prev - x

    def mix(j):
        return (x + xx * mu_ref[j:j + 1, :]).astype(BF16)

    xr, xw, xk, xv, xa, xg = (mix(j) for j in range(6))
    r = _dot(xr, wrkv_ref[0])
    k = _dot(xk, wrkv_ref[1])
    v = _dot(xv, wrkv_ref[2])
    zw = b0_ref[0:1, :] + _dot(jnp.tanh(_dot(xw, w1_ref[...])).astype(BF16), w2_ref[...])
    lw = -jnp.exp(_log_sigmoid(zw) - 0.5)
    a = jax.nn.sigmoid(b0_ref[1:2, :] + _dot(_dot(xa, a1_ref[...]).astype(BF16), a2_ref[...]))
    g = _dot(jax.nn.sigmoid(_dot(xg, g1_ref[...])).astype(BF16), g2_ref[...])
    if with_v:
        vf = vf_ref[...].reshape(m, d)
        lam = jax.nn.sigmoid(b0_ref[2:3, :]
                             + _dot(_dot(xv, v1_ref[...]).astype(BF16), v2_ref[...]))
        v = v + (vf - v) * lam
    shp = (bb, tl, d)
    r_out[...] = r.reshape(shp)
    lw_out[...] = lw.reshape(shp)
    k_out[...] = k.reshape(shp)
    v_out[...] = v.reshape(shp)
    a_out[...] = a.reshape(shp)
    g_out[...] = g.reshape(shp)


def rwkv_proj(xn, shift, v_first, prm, *, bb, tl):
    bsz, L, d = xn.shape
    with_v = v_first is not None
    seq = pl.BlockSpec((bb, tl, d), lambda b, l: (b, l, 0))
    ins = [xn, shift.reshape(bsz, 1, d)]
    specs = [seq, pl.BlockSpec((bb, 1, d), lambda b, l: (b, 0, 0))]
    if with_v:
        ins.append(v_first)
        specs.append(seq)
    names = ["mu", "b0", "w_rkv", "w1", "w2", "a1", "a2", "g1", "g2"]
    if with_v:
        names += ["v1", "v2"]
    for nme in names:
        ins.append(prm[nme])
        specs.append(_resident(prm[nme].shape))
    return pl.pallas_call(
        functools.partial(_rwkv_proj_kernel, bb=bb, tl=tl, with_v=with_v),
        grid=(bsz // bb, L // tl),
        in_specs=specs,
        out_specs=[seq] * 6,
        out_shape=[jax.ShapeDtypeStruct((bsz, L, d), F32)] * 6,
        scratch_shapes=[pltpu.VMEM((bb, 1, d), F32)],
        compiler_params=_cparams(("parallel", "arbitrary")),
        name="rwkv_proj",
    )(*ins)


def _unit_lower_inverse(n_mats, eye, same_block, size, mm):
    blk8 = same_block(3)
    n8 = [jnp.where(blk8, m, 0.0) for m in n_mats]
    n8_2 = [mm(m, m) for m in n8]
    t = [eye + m for m in n8]
    t = [ti + mm(ti, m2) for ti, m2 in zip(t, n8_2)]
    n8_4 = [mm(m2, m2) for m2 in n8_2]
    t = [ti + mm(ti, m4) for ti, m4 in zip(t, n8_4)]
    shift = 3
    while (1 << shift) < size:
        inner, outer = same_block(shift), same_block(shift + 1)
        off = [jnp.where(inner, 0.0, jnp.where(outer, m, 0.0)) for m in n_mats]
        left = [mm(ti, oi) for ti, oi in zip(t, off)]
        t = [ti + mm(li, ti) for ti, li in zip(t, left)]
        shift += 1
    return t


def _wkv_kernel(r_ref, lw_ref, k_ref, v_ref, a_ref, g_ref, hp_ref, s0_ref,
                z_out, s_out, s_scr, *, bb, tl, c, prec):
    n = RW_HEAD
    nheads = s_scr.shape[1]
    nchunk = tl // c

    @pl.when(pl.program_id(2) == 0)
    def _():
        s_scr[...] = s0_ref[...]

    row = lax.broadcasted_iota(jnp.int32, (c, c), 0)
    col = lax.broadcasted_iota(jnp.int32, (c, c), 1)
    tril = jnp.where(row >= col, 1.0, 0.0).astype(F32)
    eye = jnp.where(row == col, 1.0, 0.0).astype(F32)

    def same_block(shift):
        return (row >> shift) == (col >> shift)

    row2 = lax.broadcasted_iota(jnp.int32, (2 * c, c), 0)
    col2 = lax.broadcasted_iota(jnp.int32, (2 * c, c), 1)
    mask2 = col2 <= jnp.where(row2 < c, row2 - 1, row2 - c)
    k_k = hp_ref[0:1, :]
    k_a = hp_ref[1:2, :]
    r_k = hp_ref[2:3, :]
    ln_w = hp_ref[3:4, :]
    ln_b = hp_ref[4:5, :]

    chains = [(s, h) for s in range(bb) for h in range(nheads)]

    def chunk(ci, carry):
        rows = pl.ds(pl.multiple_of(ci * c, c), c)
        lw = [lw_ref[s, rows, :] for s in range(bb)]
        cum = [_dot(tril, x, HIGHEST) for x in lw]
        seq = []
        for s in range(bb):
            r = r_ref[s, rows, :]
            k = k_ref[s, rows, :]
            a = a_ref[s, rows, :]
            cum_end = cum[s][c - 1:c, :]
            e_neg = jnp.exp(-cum[s])
            e_end = jnp.exp(cum_end - cum[s])
            k_mod = k * (1.0 + (a - 1.0) * k_a)
            seq.append(dict(
                a=a, v=v_ref[s, rows, :], kk_raw=k * k_k,
                r_t=r * jnp.exp(cum[s]), e_neg=e_neg, e_end=e_end,
                e_prev=jnp.exp(cum[s] - lw[s]), g_end=jnp.exp(cum_end),
                k_t=k_mod * e_neg, k_end=k_mod * e_end, rk=r * k_mod * r_k))
        ch = []
        for s, h in chains:
            q = seq[s]
            sl = slice(h * n, (h + 1) * n)
            kk = q["kk_raw"][:, sl]
            kk = kk * lax.rsqrt(jnp.maximum(jnp.sum(kk * kk, -1, keepdims=True), 1e-24))
            b_h = kk * q["a"][:, sl]
            a_t = -kk * q["e_prev"][:, sl]
            ch.append(dict(
                ar=jnp.concatenate([a_t, q["r_t"][:, sl]], axis=0),
                b_t=b_h * q["e_neg"][:, sl], k_t=q["k_t"][:, sl],
                kb_end=jnp.concatenate([q["k_end"][:, sl], b_h * q["e_end"][:, sl]], axis=0),
                v=q["v"][:, sl], g_end=q["g_end"][:, sl],
                bonus=jnp.sum(q["rk"][:, sl], -1, keepdims=True) * q["v"][:, sl]))
        ab = [jnp.where(mask2, _dot_nt(x["ar"], x["b_t"], prec), 0.0) for x in ch]
        ak = [jnp.where(mask2, _dot_nt(x["ar"], x["k_t"], prec), 0.0) for x in ch]
        t_inv = _unit_lower_inverse([m[:c] for m in ab], eye, same_block, c,
                                    lambda x, y: _dot(x, y, prec))
        s0 = [s_scr[s, h] for s, h in chains]
        x1 = [_dot_nt(x["ar"], si, prec) for x, si in zip(ch, s0)]
        x1 = [xi + _dot(m, x["v"], prec) for xi, m, x in zip(x1, ak, ch)]
        u = [_dot(ti, xi[:c], prec) for ti, xi in zip(t_inv, x1)]
        o = [xi[c:] + _dot(m[c:], ui, prec) for xi, m, ui in zip(x1, ab, u)]
        s_new = [si * x["g_end"] + _dot_tn(jnp.concatenate([x["v"], ui], axis=0), x["kb_end"], prec)
                 for x, si, ui in zip(ch, s0, u)]
        for (s, h), sn in zip(chains, s_new):
            s_scr[s, h] = sn
        normed = []
        for oi in o:
            mean = jnp.mean(oi, -1, keepdims=True)
            var = jnp.mean(jnp.square(oi - mean), -1, keepdims=True)
            normed.append((oi - mean) * lax.rsqrt(var + RW_GN_EPS))
        for s in range(bb):
            idx = [i for i, (si, _) in enumerate(chains) if si == s]
            o_all = jnp.concatenate([normed[i] for i in idx], axis=-1)
            bonus_all = jnp.concatenate([ch[i]["bonus"] for i in idx], axis=-1)
            z_out[s, rows, :] = (o_all * ln_w + ln_b + bonus_all) * g_ref[s, rows, :]
        return carry

    lax.fori_loop(0, nchunk, chunk, 0)

    @pl.when(pl.program_id(2) == pl.num_programs(2) - 1)
    def _():
        s_out[...] = s_scr[...]


def _half_sums(x, lo):
    s_lo = jnp.sum(jnp.where(lo, x, 0.0), -1, keepdims=True)
    s_hi = jnp.sum(jnp.where(lo, 0.0, x), -1, keepdims=True)
    return jnp.where(lo, s_lo, s_hi)


def _wkv_pair_kernel(r_ref, lw_ref, k_ref, v_ref, a_ref, g_ref, hp_ref, s0_ref,
                     z_out, s_out, s_scr, *, bb, tl, c, prec):
    n = RW_HEAD
    npair = r_ref.shape[-1] // LANES
    nchunk = tl // c
    units = [(s, q) for s in range(bb) for q in range(npair)]

    @pl.when(pl.program_id(2) == 0)
    def _():
        for s, q in units:
            s_scr[s, q] = jnp.concatenate([s0_ref[s, 2 * q], s0_ref[s, 2 * q + 1]], axis=-1)

    def iota2(shape):
        return (lax.broadcasted_iota(jnp.int32, shape, 0),
                lax.broadcasted_iota(jnp.int32, shape, 1))

    row, lane = iota2((c, LANES))
    col = lane & (n - 1)
    lo_c = lane < n
    eye = jnp.where(row == col, 1.0, 0.0).astype(F32)
    row2, lane2 = iota2((2 * c, LANES))
    mask2 = (lane2 & (n - 1)) <= jnp.where(row2 < c, row2 - 1, row2 - c)
    rowt, colt = iota2((c, c))
    tril = jnp.where(rowt >= colt, 1.0, 0.0).astype(F32)
    lo_s = iota2((n, LANES))[1] < n

    def same_block(shift):
        return (row >> shift) == (col >> shift)

    def expand(y):
        rows = y.shape[0]
        r_e, l_e = iota2((2 * rows, LANES))
        same_head = (r_e >> (rows.bit_length() - 1)) == (l_e >> (n.bit_length() - 1))
        return jnp.where(same_head, jnp.concatenate([y, y], axis=0), 0.0)

    def mm(x, y):
        return _dot(x, expand(y), prec)

    k_k = hp_ref[0:1, :]
    k_a = hp_ref[1:2, :]
    r_k = hp_ref[2:3, :]
    ln_w = hp_ref[3:4, :]
    ln_b = hp_ref[4:5, :]

    def chunk(ci, carry):
        rows = pl.ds(pl.multiple_of(ci * c, c), c)
        lw = [lw_ref[s, rows, :] for s in range(bb)]
        cum = [_dot(tril, x, HIGHEST) for x in lw]
        seq = []
        for s in range(bb):
            r = r_ref[s, rows, :]
            k = k_ref[s, rows, :]
            a = a_ref[s, rows, :]
            cum_end = cum[s][c - 1:c, :]
            e_neg = jnp.exp(-cum[s])
            e_end = jnp.exp(cum_end - cum[s])
            k_mod = k * (1.0 + (a - 1.0) * k_a)
            seq.append(dict(
                a=a, v=v_ref[s, rows, :], kk_raw=k * k_k,
                r_t=r * jnp.exp(cum[s]), e_neg=e_neg, e_end=e_end,
                e_prev=jnp.exp(cum[s] - lw[s]), g_end=jnp.exp(cum_end),
                k_t=k_mod * e_neg, k_end=k_mod * e_end, rk=r * k_mod * r_k))
        un = []
        for s, q in units:
            p = seq[s]
            sl = slice(q * LANES, (q + 1) * LANES)
            kk = p["kk_raw"][:, sl]
            kk = kk * lax.rsqrt(jnp.maximum(_half_sums(kk * kk, lo_c), 1e-24))
            b_h = kk * p["a"][:, sl]
            a_t = -kk * p["e_prev"][:, sl]
            un.append(dict(
                ar=jnp.concatenate([a_t, p["r_t"][:, sl]], axis=0),
                bk=jnp.concatenate([expand(b_h * p["e_neg"][:, sl]), expand(p["k_t"][:, sl])],
                                   axis=0),
                kb_end=jnp.concatenate([p["k_end"][:, sl], b_h * p["e_end"][:, sl]], axis=0),
                v=p["v"][:, sl], g_end=p["g_end"][:, sl],
                bonus=_half_sums(p["rk"][:, sl], lo_c) * p["v"][:, sl]))
        g_all = [_dot_nt(x["ar"], x["bk"], prec) for x in un]
        ab = [jnp.where(mask2, m[:, :2 * c], 0.0) for m in g_all]
        ak = [jnp.where(mask2, m[:, 2 * c:], 0.0) for m in g_all]
        t_inv = _unit_lower_inverse([m[:c] for m in ab], eye, same_block, c, mm)
        s0 = [s_scr[s, q] for s, q in units]
        x1 = [_dot_nt(x["ar"], expand(si), prec) for x, si in zip(un, s0)]
        x1 = [xi + mm(m, x["v"]) for xi, m, x in zip(x1, ak, un)]
        u = [mm(ti, xi[:c]) for ti, xi in zip(t_inv, x1)]
        o = [xi[c:] + mm(m[c:], ui) for xi, m, ui in zip(x1, ab, u)]
        w = [_dot_tn(jnp.concatenate([x["v"], ui], axis=0), x["kb_end"], prec)
             for x, ui in zip(un, u)]
        for (s, q), si, wi, x in zip(units, s0, w, un):
            s_scr[s, q] = si * x["g_end"] + jnp.where(lo_s, wi[:n], wi[n:])
        normed = []
        for oi in o:
            mean = _half_sums(oi, lo_c) * (1.0 / n)
            var = _half_sums(jnp.square(oi - mean), lo_c) * (1.0 / n)
            normed.append((oi - mean) * lax.rsqrt(var + RW_GN_EPS))
        for s in range(bb):
            idx = [i for i, (si, _) in enumerate(units) if si == s]
            o_all = jnp.concatenate([normed[i] for i in idx], axis=-1)
            bonus_all = jnp.concatenate([un[i]["bonus"] for i in idx], axis=-1)
            z_out[s, rows, :] = (o_all * ln_w + ln_b + bonus_all) * g_ref[s, rows, :]
        return carry

    lax.fori_loop(0, nchunk, chunk, 0)

    @pl.when(pl.program_id(2) == pl.num_programs(2) - 1)
    def _():
        for s, q in units:
            s_out[s, 2 * q] = s_scr[s, q][:, :n]
            s_out[s, 2 * q + 1] = s_scr[s, q][:, n:]


def wkv_scan(r, lw, k, v, a, g, hp, s0, *, bb, tl, c, nheads, prec):
    bsz, L, d = r.shape
    width = nheads * RW_HEAD
    seq = pl.BlockSpec((bb, tl, width), lambda b, h, l: (b, l, h))
    st = pl.BlockSpec((bb, nheads, RW_HEAD, RW_HEAD), lambda b, h, l: (b, h, 0, 0))
    paired = 2 * c == LANES and nheads % 2 == 0
    body = _wkv_pair_kernel if paired else _wkv_kernel
    scr = (bb, nheads // 2, RW_HEAD, LANES) if paired else (bb, nheads, RW_HEAD, RW_HEAD)
    return pl.pallas_call(
        functools.partial(body, bb=bb, tl=tl, c=c, prec=prec),
        grid=(bsz // bb, d // width, L // tl),
        in_specs=[seq] * 6 + [pl.BlockSpec((SUBLANES, width), lambda b, h, l: (0, h)), st],
        out_specs=[seq, st],
        out_shape=[jax.ShapeDtypeStruct((bsz, L, d), F32),
                   jax.ShapeDtypeStruct(s0.shape, F32)],
        scratch_shapes=[pltpu.VMEM(scr, F32)],
        compiler_params=_cparams(("parallel", "parallel", "arbitrary")),
        name="wkv_scan",
    )(r, lw, k, v, a, g, hp, s0)


def rwkv_layer(xn, shift, wkv, v_first, prm, cfg):
    r, lw, k, v, a, g = rwkv_proj(xn, shift, v_first, prm, bb=cfg["proj_bb"], tl=cfg["proj_tl"])
    z, wkv_new = wkv_scan(r, lw, k, v, a, g, prm["hp"], wkv,
                          bb=cfg["scan_bb"], tl=cfg["wkv_tl"], c=cfg["wkv_c"],
                          nheads=cfg["wkv_heads"], prec=SCAN_PREC)
    return (z, prm["w_o"]), xn[:, -1], wkv_new, (v if v_first is None else v_first)


def _prep_rwkv(mu, w0, w1, w2, a0, a1, a2, g1, g2, k_k, k_a, r_k, w_rkv, w_o, lnx_w, lnx_b,
               v_lora=None):
    d = D_MODEL
    zero = jnp.zeros((d,), F32)
    prm = dict(mu=mu, w_rkv=w_rkv.astype(BF16), w_o=w_o.astype(BF16),
               w1=w1.astype(BF16), w2=w2.astype(BF16), a1=a1.astype(BF16), a2=a2.astype(BF16),
               g1=g1.astype(BF16), g2=g2.astype(BF16))
    v0 = zero
    if v_lora is not None:
        v0, v1, v2 = v_lora
        prm["v1"] = v1.astype(BF16)
        prm["v2"] = v2.astype(BF16)
    prm["b0"] = jnp.stack([w0, a0, v0] + [zero] * 5)
    prm["hp"] = jnp.stack([k_k, k_a, r_k.reshape(d), lnx_w, lnx_b] + [zero] * 3)
    return prm


S5_CH = 8
S5_CL = S5_LANES // S5_CH


def _s5_kernel(u_ref, w_ref, cm_ref, ar_ref, ai_ref, d_ref, h0r_ref, h0i_ref,
               g_out, hr_out, hi_out, hs, hc, *, bsz, tr):
    i = pl.program_id(1)
    cl = S5_CL

    @pl.when(i == 0)
    def _():
        hc[:, :cl] = h0r_ref[...]
        hc[:, cl:] = h0i_ref[...]

    u = u_ref[...]
    hs[...] = _dot(u, w_ref[...])
    ar = jnp.broadcast_to(ar_ref[...], (bsz, cl))
    ai = jnp.broadcast_to(ai_ref[...], (bsz, cl))

    def step(t, carry):
        hr, hi = carry
        rows = pl.ds(pl.multiple_of(t * bsz, bsz), bsz)
        nr = ar * hr - ai * hi + hs[rows, :cl]
        ni = ar * hi + ai * hr + hs[rows, cl:]
        hs[rows, :cl] = nr
        hs[rows, cl:] = ni
        return nr, ni

    nstep = tr // bsz
    hr, hi = lax.fori_loop(0, nstep, step, (hc[:, :cl], hc[:, cl:]),
                           unroll=min(nstep, SUBLANES))
    hc[:, :cl] = hr
    hc[:, cl:] = hi
    y = _dot(hs[...], cm_ref[...]) + d_ref[...] * u
    g_out[...] = jax.nn.gelu(y)

    @pl.when(i == pl.num_programs(1) - 1)
    def _():
        hr_out[...] = hr
        hi_out[...] = hi


def s5_scan(u_tm, h0r, h0i, prm, *, bsz, tr):
    rows, d = u_tm.shape
    cl = S5_CL
    st = pl.BlockSpec((bsz, cl), lambda c, i: (0, c))
    return pl.pallas_call(
        functools.partial(_s5_kernel, bsz=bsz, tr=tr),
        grid=(S5_CH, rows // tr),
        in_specs=[
            pl.BlockSpec((tr, LANES), lambda c, i: (i, c)),
            pl.BlockSpec((None, LANES, 2 * cl), lambda c, i: (c, 0, 0)),
            pl.BlockSpec((None, 2 * cl, LANES), lambda c, i: (c, 0, 0)),
            pl.BlockSpec((None, 1, cl), lambda c, i: (c, 0, 0)),
            pl.BlockSpec((None, 1, cl), lambda c, i: (c, 0, 0)),
            pl.BlockSpec((1, LANES), lambda c, i: (0, c)),
            st, st,
        ],
        out_specs=[pl.BlockSpec((tr, LANES), lambda c, i: (i, c)), st, st],
        out_shape=[jax.ShapeDtypeStruct((rows, d), F32),
                   jax.ShapeDtypeStruct((bsz, S5_LANES), F32),
                   jax.ShapeDtypeStruct((bsz, S5_LANES), F32)],
        scratch_shapes=[pltpu.VMEM((tr, 2 * cl), F32), pltpu.VMEM((bsz, 2 * cl), F32)],
        compiler_params=_cparams(("parallel", "arbitrary")),
        name="s5_scan",
    )(u_tm, prm["w_in"], prm["c_out"], prm["ab_re"], prm["ab_im"], prm["d"], h0r, h0i)


def _prep_s5(a_re, a_im, log_dt, b_re, b_im, c_re, c_im, d_skip, glu_v, glu_g):
    dt = jnp.exp(log_dt.astype(F32))[:, None]
    lr, li = a_re.astype(F32), a_im.astype(F32)
    mag = jnp.exp(lr * dt)
    ab_re, ab_im = mag * jnp.cos(li * dt), mag * jnp.sin(li * dt)
    den = lr * lr + li * li
    q_re = ((ab_re - 1.0) * lr + ab_im * li) / den
    q_im = (ab_im * lr - (ab_re - 1.0) * li) / den
    bb_re = q_re[..., None] * b_re - q_im[..., None] * b_im
    bb_im = q_re[..., None] * b_im + q_im[..., None] * b_re
    gl = S5_GROUPS // S5_CH
    eye = jnp.eye(gl, dtype=F32)

    def in_blocks(bb):
        t = bb.reshape(S5_CH, gl, S5_STATE, S5_GROUP)
        return jnp.einsum('cgph,gk->cghkp', t, eye).reshape(S5_CH, LANES, S5_CL)

    def out_blocks(cc):
        t = cc.reshape(S5_CH, gl, S5_GROUP, S5_STATE)
        return jnp.einsum('cghp,gk->ckpgh', t, eye).reshape(S5_CH, S5_CL, LANES)

    return dict(
        w_in=jnp.concatenate([in_blocks(bb_re), in_blocks(bb_im)], axis=-1),
        c_out=jnp.concatenate([out_blocks(c_re.astype(F32)), -out_blocks(c_im.astype(F32))],
                              axis=1),
        ab_re=ab_re.reshape(S5_CH, 1, S5_CL), ab_im=ab_im.reshape(S5_CH, 1, S5_CL),
        d=d_skip.reshape(1, D_MODEL),
        glu_v=glu_v.astype(BF16), glu_g=glu_g.astype(BF16))


def s5_layer(xn, h_re0, h_im0, prm, cfg):
    bsz = h_re0.shape[0]
    if cfg["time_major"]:
        L, _, d = xn.shape
        u_tm = xn.reshape(L * bsz, d)
    else:
        _, L, d = xn.shape
        u_tm = jnp.swapaxes(xn, 0, 1).reshape(L * bsz, d)
    g, hr, hi = s5_scan(u_tm, h_re0.reshape(bsz, S5_LANES), h_im0.reshape(bsz, S5_LANES), prm,
                        bsz=bsz, tr=cfg["s5_tr"])
    y = glu_rows(g, prm["glu_v"], prm["glu_g"], tm=cfg["tm"]).reshape(L, bsz, d)
    if not cfg["time_major"]:
        y = jnp.swapaxes(y, 0, 1)
    return ((y, None), hr.reshape(bsz, S5_GROUPS, S5_STATE), hi.reshape(bsz, S5_GROUPS, S5_STATE))


def _mamba_in_kernel(x_ref, cp_ref, wz_ref, wx_ref, wdt_ref, cw_ref, cb_ref, dtb_ref,
                     z_out, xs_out, b_out, c_out, dt_out, nc_out, carry, *, bb, tl):
    d = x_ref.shape[-1]
    m = bb * tl
    cd = M_CONV_DIM
    l = pl.program_id(1)

    @pl.when(l == 0)
    def _():
        carry[:, :SUBLANES - (M_CONV - 1), :] = jnp.zeros(
            (bb, SUBLANES - (M_CONV - 1), cd), F32)
        carry[:, SUBLANES - (M_CONV - 1):, :] = cp_ref[...]

    x = x_ref[...].reshape(m, d).astype(BF16)
    z_out[...] = _dot(x, wz_ref[...]).reshape(bb, tl, M_INNER)
    dt_raw = _dot(x, wdt_ref[...])[:, :M_HEADS]
    dt_out[...] = _softplus(dt_raw + dtb_ref[...]).reshape(bb, tl, M_HEADS)
    xbc = _dot(x, wx_ref[...]).reshape(bb, tl, cd)
    full = jnp.concatenate([carry[...], xbc], axis=1).reshape(bb * (tl + SUBLANES), cd)
    conv = cb_ref[...] + xbc.reshape(m, cd) * cw_ref[M_CONV - 1:M_CONV, :]
    for j in range(1, M_CONV):
        sh = pltpu.roll(full, j, 0).reshape(bb, tl + SUBLANES, cd)[:, SUBLANES:, :]
        conv = conv + sh.reshape(m, cd) * cw_ref[M_CONV - 1 - j:M_CONV - j, :]
    act = conv * jax.nn.sigmoid(conv)
    xs_out[...] = act[:, :M_INNER].reshape(bb, tl, M_INNER)
    nbc = M_GROUPS * M_STATE
    b_out[...] = act[:, M_INNER:M_INNER + nbc].reshape(bb, tl, nbc)
    c_out[...] = act[:, M_INNER + nbc:].reshape(bb, tl, nbc)
    full3 = full.reshape(bb, tl + SUBLANES, cd)
    carry[...] = full3[:, tl:, :]

    @pl.when(l == pl.num_programs(1) - 1)
    def _():
        nc_out[...] = full3[:, tl + SUBLANES - (M_CONV - 1):, :]


def mamba_in(xn, conv_prev, prm, *, bb, tl):
    bsz, L, d = xn.shape
    cd = M_CONV_DIM
    nbc = M_GROUPS * M_STATE
    seq = lambda w: pl.BlockSpec((bb, tl, w), lambda b, l: (b, l, 0))
    cps = pl.BlockSpec((bb, M_CONV - 1, cd), lambda b, l: (b, 0, 0))
    return pl.pallas_call(
        functools.partial(_mamba_in_kernel, bb=bb, tl=tl),
        grid=(bsz // bb, L // tl),
        in_specs=[seq(d), cps, _resident((d, M_INNER)), _resident((d, cd)),
                  _resident((d, LANES)),
                  _full((M_CONV, cd)), _full((1, cd)), _full((1, M_HEADS))],
        out_specs=[seq(M_INNER), seq(M_INNER), seq(nbc), seq(nbc), seq(M_HEADS), cps],
        out_shape=[jax.ShapeDtypeStruct((bsz, L, M_INNER), F32),
                   jax.ShapeDtypeStruct((bsz, L, M_INNER), F32),
                   jax.ShapeDtypeStruct((bsz, L, nbc), F32),
                   jax.ShapeDtypeStruct((bsz, L, nbc), F32),
                   jax.ShapeDtypeStruct((bsz, L, M_HEADS), F32),
                   jax.ShapeDtypeStruct((bsz, M_CONV - 1, cd), F32)],
        scratch_shapes=[pltpu.VMEM((bb, SUBLANES, cd), F32)],
        compiler_params=_cparams(("parallel", "arbitrary")),
        name="mamba_in",
    )(xn, conv_prev, prm["wz"], prm["wx"], prm["wdt"], prm["conv_w"], prm["conv_b"],
      prm["dt_bias"])


def _ssd_kernel(xs_ref, b_ref, c_ref, z_ref, dt_ref, hv_ref, ex_ref, nw_ref, s0_ref,
                y_out, s_out, s_scr, *, bb, tl, c):
    p, n = M_HEADDIM, M_STATE
    hpg = M_HEADS // M_GROUPS
    gw = M_INNER // M_GROUPS
    nchunk = tl // c

    @pl.when(pl.program_id(1) == 0)
    def _():
        s_scr[...] = s0_ref[...]

    row = lax.broadcasted_iota(jnp.int32, (c, c), 0)
    col = lax.broadcasted_iota(jnp.int32, (c, c), 1)
    incl = row >= col
    tril = jnp.where(incl, 1.0, 0.0).astype(F32)
    triu = jnp.where(row <= col, 1.0, 0.0).astype(F32)
    a_neg = hv_ref[0:1, :]
    d_exp = nw_ref[1:2, :]
    norm_w = nw_ref[0:1, :]
    expand = ex_ref[...]

    def chunk(i, carry):
        s = i // nchunk
        ci = i - s * nchunk
        rows = pl.ds(pl.multiple_of(ci * c, c), c)
        xs = xs_ref[s, rows, :]
        bm = b_ref[s, rows, :]
        cm = c_ref[s, rows, :]
        dt = dt_ref[s, rows, :]
        a = dt * a_neg
        cum = _dot(tril, a, HIGHEST)
        cum_t = _dot_tn(a, triu, HIGHEST)
        cum_end = cum[c - 1:c, :]
        e_end = jnp.exp(cum_end)
        dt_x = _dot(dt, expand)
        dec_x = _dot(jnp.exp(cum_end - cum), expand)
        ecum_x = _dot(jnp.exp(cum), expand)
        xdt = xs * dt_x
        xend = xdt * dec_x
        yd, yo = [], []
        for gi in range(M_GROUPS):
            b_g = bm[:, gi * n:(gi + 1) * n]
            c_g = cm[:, gi * n:(gi + 1) * n]
            cb = _dot_nt(c_g, b_g)
            for hh in range(hpg):
                h = gi * hpg + hh
                sl = slice(h * p, (h + 1) * p)
                seg = cum[:, h:h + 1] - cum_t[h:h + 1, :]
                lmat = jnp.where(incl, jnp.exp(jnp.where(incl, seg, 0.0)), 0.0)
                s_h = s_scr[s, h]
                yd.append(_dot(cb * lmat, xdt[:, sl]))
                yo.append(_dot_nt(c_g, s_h))
                s_scr[s, h] = s_h * e_end[:, h:h + 1] + _dot_tn(xend[:, sl], b_g)
        y = (jnp.concatenate(yd, axis=-1) + jnp.concatenate(yo, axis=-1) * ecum_x
             + d_exp * xs)
        zz = z_ref[s, rows, :]
        y = y * (zz * jax.nn.sigmoid(zz))
        outs = []
        for gi in range(M_GROUPS):
            yg = y[:, gi * gw:(gi + 1) * gw]
            outs.append(yg * lax.rsqrt(jnp.mean(yg * yg, -1, keepdims=True) + EPS))
        y_out[s, rows, :] = jnp.concatenate(outs, axis=-1) * norm_w
        return carry

    lax.fori_loop(0, bb * nchunk, chunk, 0)

    @pl.when(pl.program_id(1) == pl.num_programs(1) - 1)
    def _():
        s_out[...] = s_scr[...]


def ssd_scan(xs, bm, cm, z, dt, prm, s0, *, bb, tl, c):
    bsz, L, _ = xs.shape
    nbc = M_GROUPS * M_STATE
    seq = lambda w: pl.BlockSpec((bb, tl, w), lambda b, l: (b, l, 0))
    st = pl.BlockSpec((bb, M_HEADS, M_HEADDIM, M_STATE), lambda b, l: (b, 0, 0, 0))
    return pl.pallas_call(
        functools.partial(_ssd_kernel, bb=bb, tl=tl, c=c),
        grid=(bsz // bb, L // tl),
        in_specs=[seq(M_INNER), seq(nbc), seq(nbc), seq(M_INNER), seq(M_HEADS),
                  _full((SUBLANES, M_HEADS)), _full((M_HEADS, M_INNER)),
                  _full((SUBLANES, M_INNER)), st],
        out_specs=[seq(M_INNER), st],
        out_shape=[jax.ShapeDtypeStruct((bsz, L, M_INNER), F32),
                   jax.ShapeDtypeStruct(s0.shape, F32)],
        scratch_shapes=[pltpu.VMEM((bb, M_HEADS, M_HEADDIM, M_STATE), F32)],
        compiler_params=_cparams(("parallel", "arbitrary")),
        name="ssd_scan",
    )(xs, bm, cm, z, dt, prm["hv"], prm["expand"], prm["nw"], s0)


def _prep_mamba(in_proj, conv_w, conv_b, dt_bias, a_log, d_skip, norm_w, out_proj):
    d = D_MODEL
    wdt = jnp.zeros((d, LANES), F32).at[:, :M_HEADS].set(in_proj[:, M_INNER + M_CONV_DIM:])
    expand = jnp.repeat(jnp.eye(M_HEADS, dtype=F32), M_HEADDIM, axis=1)
    zrow_h = jnp.zeros((M_HEADS,), F32)
    zrow_i = jnp.zeros((M_INNER,), F32)
    return dict(
        wz=in_proj[:, :M_INNER].astype(BF16),
        wx=in_proj[:, M_INNER:M_INNER + M_CONV_DIM].astype(BF16),
        wdt=wdt.astype(BF16),
        conv_w=conv_w, conv_b=conv_b.reshape(1, M_CONV_DIM),
        dt_bias=dt_bias.reshape(1, M_HEADS),
        hv=jnp.stack([-jnp.exp(a_log.astype(F32))] + [zrow_h] * 7),
        expand=expand,
        nw=jnp.stack([norm_w, jnp.repeat(d_skip, M_HEADDIM)] + [zrow_i] * 6),
        out_proj=out_proj.astype(BF16))


def mamba_layer(xn, conv_prev, ssm_prev, prm, cfg):
    bsz, L, d = xn.shape
    z, xs, bm, cm, dt, new_conv = mamba_in(xn, conv_prev, prm, bb=cfg["mamba_bb"],
                                           tl=cfg["mamba_tl"])
    y, ssm_new = ssd_scan(xs, bm, cm, z, dt, prm, ssm_prev,
                          bb=cfg["ssd_bb"], tl=cfg["ssd_tl"], c=cfg["ssd_c"])
    return (y, prm["out_proj"]), new_conv, ssm_new


def _config(bsz, L):
    if L >= 512:
        return dict(tm=512, ffn_fb=bsz, ffn_ft=512 // bsz, time_major=bsz == SUBLANES,
                    proj_bb=1, proj_tl=256,
                    scan_bb=min(bsz, 8), wkv_tl=256, wkv_c=64, wkv_heads=4,
                    mamba_bb=1, mamba_tl=256,
                    ssd_bb=1, ssd_tl=256, ssd_c=M_CHUNK, s5_tr=2048)
    rows = bsz * L
    tm = min(512, rows)
    return dict(tm=tm, ffn_fb=tm // L, ffn_ft=L, time_major=False,
                proj_bb=min(bsz, 256 // L), proj_tl=L,
                scan_bb=min(bsz, 16), wkv_tl=L, wkv_c=L, wkv_heads=2,
                mamba_bb=min(bsz, 256 // L), mamba_tl=L,
                ssd_bb=min(bsz, 4), ssd_tl=L, ssd_c=min(M_CHUNK, L), s5_tr=rows)


def _trunk(x, p, states, layer_prms, shared, cfg):
    bsz, L, d = x.shape
    xn = rmsnorm_rows(x.reshape(bsz * L, d), shared["norm_mix"][0], cfg["tm"]).reshape(bsz, L, d)
    h = x
    v_first = None
    new_states = []
    for i in range(DEPTH):
        st_a, st_b = states[2 * i], states[2 * i + 1]
        prm = layer_prms[i]
        kind = i % 3
        if kind == 0:
            (mix, wo), n_a, n_b, v_first = rwkv_layer(xn, st_a, st_b, v_first, prm, cfg)
        elif kind == 1:
            (mix, wo), n_a, n_b = s5_layer(xn, st_a, st_b, prm, cfg)
        else:
            (mix, wo), n_a, n_b = mamba_layer(xn, st_a, st_b, prm, cfg)
        new_states += [n_a, n_b]
        nxt = shared["norm_mix"][i + 1] if i + 1 < DEPTH else shared["final_norm"]
        mix_tm = cfg["time_major"] and kind == 1
        xn_tm = cfg["time_major"] and (i + 1) % 3 == 1 and i + 1 < DEPTH
        h, xn = ffn_ple(h, mix, wo, p, i, shared, nxt, fb=cfg["ffn_fb"], ft=cfg["ffn_ft"],
                        mix_tm=mix_tm, xn_tm=xn_tm)
    return xn, new_states


def kernel(x_prompt, x_sample, p_prompt, p_sample, state_l0_shift, state_l0_wkv, state_l1_s5_re, state_l1_s5_im, state_l2_conv, state_l2_ssm, state_l3_shift, state_l3_wkv, l0_mu, l0_w0, l0_w1, l0_w2, l0_a0, l0_a1, l0_a2, l0_g1, l0_g2, l0_k_k, l0_k_a, l0_r_k, l0_w_rkv, l0_w_o, l0_lnx_w, l0_lnx_b, l1_a_re, l1_a_im, l1_log_dt, l1_b_re, l1_b_im, l1_c_re, l1_c_im, l1_d, l1_glu_v, l1_glu_g, l2_in_proj, l2_conv_w, l2_conv_b, l2_dt_bias, l2_a_log, l2_d, l2_norm_w, l2_out_proj, l3_mu, l3_w0, l3_w1, l3_w2, l3_a0, l3_a1, l3_a2, l3_g1, l3_g2, l3_k_k, l3_k_a, l3_r_k, l3_w_rkv, l3_w_o, l3_lnx_w, l3_lnx_b, l3_v0, l3_v1, l3_v2, norm_mix, norm_ffn, norm_ple, ffn_w1, ffn_w3, ffn_w2, ple_proj, ple_gate, final_norm):
    layer_prms = (
        _prep_rwkv(l0_mu, l0_w0, l0_w1, l0_w2, l0_a0, l0_a1, l0_a2, l0_g1, l0_g2,
                   l0_k_k, l0_k_a, l0_r_k, l0_w_rkv, l0_w_o, l0_lnx_w, l0_lnx_b),
        _prep_s5(l1_a_re, l1_a_im, l1_log_dt, l1_b_re, l1_b_im, l1_c_re, l1_c_im,
                 l1_d, l1_glu_v, l1_glu_g),
        _prep_mamba(l2_in_proj, l2_conv_w, l2_conv_b, l2_dt_bias, l2_a_log, l2_d,
                    l2_norm_w, l2_out_proj),
        _prep_rwkv(l3_mu, l3_w0, l3_w1, l3_w2, l3_a0, l3_a1, l3_a2, l3_g1, l3_g2,
                   l3_k_k, l3_k_a, l3_r_k, l3_w_rkv, l3_w_o, l3_lnx_w, l3_lnx_b,
                   (l3_v0, l3_v1, l3_v2)),
    )
    shared = dict(norm_mix=norm_mix, norm_ffn=norm_ffn.reshape(DEPTH, 1, D_MODEL),
                  norm_ple=norm_ple.reshape(DEPTH, 1, D_MODEL), final_norm=final_norm,
                  ffn_w1=ffn_w1.astype(BF16), ffn_w3=ffn_w3.astype(BF16),
                  ffn_w2=ffn_w2.astype(BF16), ple_gate=ple_gate.astype(BF16),
                  ple_proj=ple_proj.astype(BF16))
    sample_states = [state_l0_shift, state_l0_wkv, state_l1_s5_re, state_l1_s5_im,
                     state_l2_conv, state_l2_ssm, state_l3_shift, state_l3_wkv]
    bp = x_prompt.shape[0]
    prompt_states = [jnp.zeros((bp,) + s.shape[1:], F32) for s in sample_states]
    y_prompt, new_p = _trunk(x_prompt, p_prompt, prompt_states, layer_prms, shared,
                             _config(*x_prompt.shape[:2]))
    y_sample, new_s = _trunk(x_sample, p_sample, sample_states, layer_prms, shared,
                             _config(*x_sample.shape[:2]))
    return (y_prompt, y_sample, *new_p, *new_s)
```

```python
import functools
import math

import jax
import jax.numpy as jnp
from jax import lax
from jax.experimental import pallas as pl
from jax.experimental.pallas import tpu as pltpu

F32 = jnp.float32
BF16 = jnp.bfloat16
HIGHEST = lax.Precision.HIGHEST

D_MODEL = 1024
DEPTH = 4
PLE_DIM = 256
EPS = 1e-6
D_FF = 2816
RW_HEAD = 64
RW_HEADS = D_MODEL // RW_HEAD
RW_GN_EPS = 64e-5
S5_GROUP = 16
S5_GROUPS = D_MODEL // S5_GROUP
S5_STATE = 64
S5_LANES = S5_GROUPS * S5_STATE
M_INNER = 2 * D_MODEL
M_HEADDIM = 64
M_HEADS = M_INNER // M_HEADDIM
M_STATE = 128
M_GROUPS = 4
M_CONV = 4
M_CHUNK = 64
M_CONV_DIM = M_INNER + 2 * M_GROUPS * M_STATE

LANES = 128
MXU_TILE = 256
SUBLANES = 8
VMEM_LIMIT = 56 * 1024 * 1024

SCAN_PREC = None


def _cparams(sem):
    return pltpu.CompilerParams(dimension_semantics=sem, vmem_limit_bytes=VMEM_LIMIT)


def _operands(a, b, prec):
    if prec is None:
        return a.astype(BF16), b.astype(BF16)
    return a, b


def _dot(a, b, prec=None):
    a, b = _operands(a, b, prec)
    return jnp.dot(a, b, preferred_element_type=F32, precision=prec)


def _dot_nt(a, b, prec=None):
    a, b = _operands(a, b, prec)
    return lax.dot_general(a, b, (((1,), (1,)), ((), ())),
                           preferred_element_type=F32, precision=prec)


def _dot_tn(a, b, prec=None):
    a, b = _operands(a, b, prec)
    return lax.dot_general(a, b, (((0,), (0,)), ((), ())),
                           preferred_element_type=F32, precision=prec)


def _rms(x, g):
    return x * lax.rsqrt(jnp.mean(x * x, axis=-1, keepdims=True) + EPS) * g


def _log_sigmoid(z):
    return -(jnp.maximum(-z, 0.0) + jnp.log1p(jnp.exp(-jnp.abs(z))))


def _softplus(z):
    return jnp.maximum(z, 0.0) + jnp.log1p(jnp.exp(-jnp.abs(z)))


def _full(shape):
    n = len(shape)
    return pl.BlockSpec(shape, lambda *_: (0,) * n)


def _resident(shape):
    n = len(shape)
    return pl.BlockSpec(shape, lambda *_: (0,) * n, pipeline_mode=pl.Buffered(1))


def _ffn_kernel(*refs, project, mix_tm, xn_tm):
    if project:
        (h_ref, mix_ref, wo_ref, p_ref, nf_ref, np_ref, nn_ref, w1_ref, w3_ref, w2_ref,
         pg_ref, pp_ref, h_out, xn_out) = refs
    else:
        (h_ref, mix_ref, p_ref, nf_ref, np_ref, nn_ref, w1_ref, w3_ref, w2_ref,
         pg_ref, pp_ref, h_out, xn_out) = refs
    fb, ft, d = h_ref.shape
    tm = fb * ft
    mix = mix_ref[...]
    if mix_tm:
        mix = jnp.swapaxes(mix, 0, 1)
    mix = mix.reshape(tm, mix.shape[-1])
    y = _dot(mix, wo_ref[...]) if project else mix
    hin = h_ref[...].reshape(tm, d) + y
    hn = _rms(hin, nf_ref[...]).astype(BF16)
    acc = None
    for ci in range(D_FF // MXU_TILE):
        cs = slice(ci * MXU_TILE, (ci + 1) * MXU_TILE)
        a = _dot(hn, w1_ref[:, cs])
        b = _dot(hn, w3_ref[:, cs])
        t = (a * jax.nn.sigmoid(a) * b).astype(BF16)
        part = _dot(t, w2_ref[cs, :])
        acc = part if acc is None else acc + part
    h2 = hin + acc
    gate = jax.nn.sigmoid(_dot(_rms(h2, np_ref[...]).astype(BF16), pg_ref[...]))
    h3 = h2 + gate * _dot(p_ref[...].reshape(tm, PLE_DIM).astype(BF16), pp_ref[...])
    h_out[...] = h3.reshape(fb, ft, d)
    xn = _rms(h3, nn_ref[...]).reshape(fb, ft, d)
    xn_out[...] = jnp.swapaxes(xn, 0, 1) if xn_tm else xn


def ffn_ple(h, mix, wo, p_all, layer, shared, nn, *, fb, ft, mix_tm=False, xn_tm=False):
    bsz, L, d = h.shape
    km = mix.shape[-1]
    nlt = L // ft
    seq = lambda w: pl.BlockSpec((fb, ft, w), lambda i: (i // nlt, i % nlt, 0))
    tmaj = lambda w: pl.BlockSpec((ft, fb, w), lambda i: (i % nlt, 0, 0))
    lay = lambda *shape: pl.BlockSpec((None,) + shape, lambda i: (layer,) + (0,) * len(shape),
                                      pipeline_mode=pl.Buffered(1))
    ins = [h, mix]
    specs = [seq(d), tmaj(km) if mix_tm else seq(km)]
    if wo is not None:
        ins.append(wo)
        specs.append(_resident(wo.shape))
    ins += [p_all, shared["norm_ffn"], shared["norm_ple"], nn.reshape(1, d),
            shared["ffn_w1"], shared["ffn_w3"], shared["ffn_w2"],
            shared["ple_gate"], shared["ple_proj"]]
    specs += [pl.BlockSpec((None, fb, ft, PLE_DIM), lambda i: (layer, i // nlt, i % nlt, 0)),
              lay(1, d), lay(1, d), _full((1, d)),
              lay(d, D_FF), lay(d, D_FF), lay(D_FF, d), lay(d, d), lay(PLE_DIM, d)]
    xn_shape = (L, bsz, d) if xn_tm else (bsz, L, d)
    return pl.pallas_call(
        functools.partial(_ffn_kernel, project=wo is not None, mix_tm=mix_tm, xn_tm=xn_tm),
        grid=(bsz // fb * nlt,),
        in_specs=specs,
        out_specs=[seq(d), tmaj(d) if xn_tm else seq(d)],
        out_shape=[jax.ShapeDtypeStruct((bsz, L, d), F32), jax.ShapeDtypeStruct(xn_shape, F32)],
        compiler_params=_cparams(("parallel",)),
        name="ffn_ple",
    )(*ins)


def _glu_kernel(x_ref, wv_ref, wg_ref, o_ref):
    g = x_ref[...].astype(BF16)
    o_ref[...] = _dot(g, wv_ref[...]) * jax.nn.sigmoid(_dot(g, wg_ref[...]))


def glu_rows(x, wv, wg, *, tm):
    n, k = x.shape
    m = wv.shape[1]
    return pl.pallas_call(
        _glu_kernel,
        grid=(n // tm,),
        in_specs=[pl.BlockSpec((tm, k), lambda i: (i, 0)), _resident((k, m)),
                  _resident((k, m))],
        out_specs=pl.BlockSpec((tm, m), lambda i: (i, 0)),
        out_shape=jax.ShapeDtypeStruct((n, m), F32),
        compiler_params=_cparams(("parallel",)),
        name="glu_rows",
    )(x, wv, wg)


def _rwkv_proj_kernel(*refs, bb, tl, with_v, norm_in):
    it = iter(refs)
    x_ref, sh_ref = next(it), next(it)
    vf_ref = next(it) if with_v else None
    nw_ref = next(it) if norm_in else None
    mu_ref, b0_ref, wrkv_ref, w1_ref, w2_ref, a1_ref, a2_ref, g1_ref, g2_ref = (
        next(it) for _ in range(9))
    if with_v:
        v1_ref, v2_ref = next(it), next(it)
    r_out, lw_out, k_out, v_out, a_out, g_out, last_out, carry = it
    d = x_ref.shape[-1]
    m = bb * tl

    @pl.when(pl.program_id(1) == 0)
    def _():
        carry[...] = sh_ref[...]

    x = x_ref[...].reshape(m, d)
    if norm_in:
        x = _rms(x, nw_ref[...])
    prev = jnp.broadcast_to(carry[...], (bb, tl, d)).reshape(m, d)
    rolled = pltpu.roll(x, 1, 0)
    t_in_tile = lax.broadcasted_iota(jnp.int32, (m, d), 0) & (tl - 1)
    x_prev = jnp.where(t_in_tile == 0, prev, rolled)
    last = x.reshape(bb, tl, d)[:, tl - 1:tl, :]
    carry[...] = last
    last_out[...] = last

    xx = x_prev - x

    def mix(j):
        return (x + xx * mu_ref[j:j + 1, :]).astype(BF16)

    shp = (bb, tl, d)
    xw, xa, xg, xv = mix(1), mix(4), mix(5), mix(3)
    zw = b0_ref[0:1, :] + _dot(jnp.tanh(_dot(xw, w1_ref[...])).astype(BF16), w2_ref[...])
    lw_out[...] = (-jnp.exp(_log_sigmoid(zw) - 0.5)).reshape(shp)
    a = jax.nn.sigmoid(b0_ref[1:2, :] + _dot(_dot(xa, a1_ref[...]).astype(BF16), a2_ref[...]))
    a_out[...] = a.reshape(shp)
    g_out[...] = _dot(jax.nn.sigmoid(_dot(xg, g1_ref[...])).astype(BF16),
                      g2_ref[...]).reshape(shp)
    if with_v:
        lam = jax.nn.sigmoid(b0_ref[2:3, :]
                             + _dot(_dot(xv, v1_ref[...]).astype(BF16), v2_ref[...]))
    v = _dot(xv, wrkv_ref[2])
    if with_v:
        v = v + (vf_ref[...].reshape(m, d) - v) * lam
    v_out[...] = v.reshape(shp)
    r_out[...] = _dot(mix(0), wrkv_ref[0]).reshape(shp)
    k_out[...] = _dot(mix(2), wrkv_ref[1]).reshape(shp)


def rwkv_proj(x, shift, v_first, prm, norm_w, *, bb, tl):
    bsz, L, d = x.shape
    with_v = v_first is not None
    norm_in = norm_w is not None
    seq = pl.BlockSpec((bb, tl, d), lambda b, l: (b, l, 0))
    row = pl.BlockSpec((bb, 1, d), lambda b, l: (b, 0, 0))
    ins = [x, shift.reshape(bsz, 1, d)]
    specs = [seq, row]
    if with_v:
        ins.append(v_first)
        specs.append(seq)
    if norm_in:
        ins.append(norm_w.reshape(1, d))
        specs.append(_full((1, d)))
    names = ["mu", "b0", "w_rkv", "w1", "w2", "a1", "a2", "g1", "g2"]
    if with_v:
        names += ["v1", "v2"]
    for nme in names:
        ins.append(prm[nme])
        specs.append(_resident(prm[nme].shape))
    return pl.pallas_call(
        functools.partial(_rwkv_proj_kernel, bb=bb, tl=tl, with_v=with_v, norm_in=norm_in),
        grid=(bsz // bb, L // tl),
        in_specs=specs,
        out_specs=[seq] * 6 + [row],
        out_shape=[jax.ShapeDtypeStruct((bsz, L, d), F32)] * 6
        + [jax.ShapeDtypeStruct((bsz, 1, d), F32)],
        scratch_shapes=[pltpu.VMEM((bb, 1, d), F32)],
        compiler_params=_cparams(("parallel", "arbitrary")),
        name="rwkv_proj",
    )(*ins)


def _unit_lower_inverse(n_mats, eye, same_block, size, mm):
    blk8 = same_block(3)
    n8 = [jnp.where(blk8, m, 0.0) for m in n_mats]
    n8_2 = [mm(m, m) for m in n8]
    t = [eye + m for m in n8]
    t = [ti + mm(ti, m2) for ti, m2 in zip(t, n8_2)]
    n8_4 = [mm(m2, m2) for m2 in n8_2]
    t = [ti + mm(ti, m4) for ti, m4 in zip(t, n8_4)]
    shift = 3
    while (1 << shift) < size:
        inner, outer = same_block(shift), same_block(shift + 1)
        off = [jnp.where(inner, 0.0, jnp.where(outer, m, 0.0)) for m in n_mats]
        left = [mm(ti, oi) for ti, oi in zip(t, off)]
        t = [ti + mm(li, ti) for ti, li in zip(t, left)]
        shift += 1
    return t


def _wkv_kernel(r_ref, lw_ref, k_ref, v_ref, a_ref, g_ref, hp_ref, s0_ref,
                z_out, s_out, s_scr, *, bb, tl, c, prec):
    n = RW_HEAD
    nheads = s_scr.shape[1]
    nchunk = tl // c

    @pl.when(pl.program_id(2) == 0)
    def _():
        s_scr[...] = s0_ref[...]

    row = lax.broadcasted_iota(jnp.int32, (c, c), 0)
    col = lax.broadcasted_iota(jnp.int32, (c, c), 1)
    tril = jnp.where(row >= col, 1.0, 0.0).astype(F32)
    eye = jnp.where(row == col, 1.0, 0.0).astype(F32)

    def same_block(shift):
        return (row >> shift) == (col >> shift)

    row2 = lax.broadcasted_iota(jnp.int32, (2 * c, c), 0)
    col2 = lax.broadcasted_iota(jnp.int32, (2 * c, c), 1)
    mask2 = col2 <= jnp.where(row2 < c, row2 - 1, row2 - c)
    k_k = hp_ref[0:1, :]
    k_a = hp_ref[1:2, :]
    r_k = hp_ref[2:3, :]
    ln_w = hp_ref[3:4, :]
    ln_b = hp_ref[4:5, :]

    chains = [(s, h) for s in range(bb) for h in range(nheads)]

    def chunk(ci, carry):
        rows = pl.ds(pl.multiple_of(ci * c, c), c)
        lw = [lw_ref[s, rows, :] for s in range(bb)]
        cum = [_dot(tril, x, HIGHEST) for x in lw]
        seq = []
        for s in range(bb):
            r = r_ref[s, rows, :]
            k = k_ref[s, rows, :]
            a = a_ref[s, rows, :]
            cum_end = cum[s][c - 1:c, :]
            e_neg = jnp.exp(-cum[s])
            e_end = jnp.exp(cum_end - cum[s])
            k_mod = k * (1.0 + (a - 1.0) * k_a)
            seq.append(dict(
                a=a, v=v_ref[s, rows, :], kk_raw=k * k_k,
                r_t=r * jnp.exp(cum[s]), e_neg=e_neg, e_end=e_end,
                e_prev=jnp.exp(cum[s] - lw[s]), g_end=jnp.exp(cum_end),
                k_t=k_mod * e_neg, k_end=k_mod * e_end, rk=r * k_mod * r_k))
        ch = []
        for s, h in chains:
            q = seq[s]
            sl = slice(h * n, (h + 1) * n)
            kk = q["kk_raw"][:, sl]
            kk = kk * lax.rsqrt(jnp.maximum(jnp.sum(kk * kk, -1, keepdims=True), 1e-24))
            b_h = kk * q["a"][:, sl]
            a_t = -kk * q["e_prev"][:, sl]
            ch.append(dict(
                ar=jnp.concatenate([a_t, q["r_t"][:, sl]], axis=0),
                b_t=b_h * q["e_neg"][:, sl], k_t=q["k_t"][:, sl],
                kb_end=jnp.concatenate([q["k_end"][:, sl], b_h * q["e_end"][:, sl]], axis=0),
                v=q["v"][:, sl], g_end=q["g_end"][:, sl],
                bonus=jnp.sum(q["rk"][:, sl], -1, keepdims=True) * q["v"][:, sl]))
        ab = [jnp.where(mask2, _dot_nt(x["ar"], x["b_t"], prec), 0.0) for x in ch]
        ak = [jnp.where(mask2, _dot_nt(x["ar"], x["k_t"], prec), 0.0) for x in ch]
        t_inv = _unit_lower_inverse([m[:c] for m in ab], eye, same_block, c,
                                    lambda x, y: _dot(x, y, prec))
        s0 = [s_scr[s, h] for s, h in chains]
        x1 = [_dot_nt(x["ar"], si, prec) for x, si in zip(ch, s0)]
        x1 = [xi + _dot(m, x["v"], prec) for xi, m, x in zip(x1, ak, ch)]
        u = [_dot(ti, xi[:c], prec) for ti, xi in zip(t_inv, x1)]
        o = [xi[c:] + _dot(m[c:], ui, prec) for xi, m, ui in zip(x1, ab, u)]
        s_new = [si * x["g_end"] + _dot_tn(jnp.concatenate([x["v"], ui], axis=0), x["kb_end"], prec)
                 for x, si, ui in zip(ch, s0, u)]
        for (s, h), sn in zip(chains, s_new):
            s_scr[s, h] = sn
        normed = []
        for oi in o:
            mean = jnp.mean(oi, -1, keepdims=True)
            var = jnp.mean(jnp.square(oi - mean), -1, keepdims=True)
            normed.append((oi - mean) * lax.rsqrt(var + RW_GN_EPS))
        for s in range(bb):
            idx = [i for i, (si, _) in enumerate(chains) if si == s]
            o_all = jnp.concatenate([normed[i] for i in idx], axis=-1)
            bonus_all = jnp.concatenate([ch[i]["bonus"] for i in idx], axis=-1)
            z_out[s, rows, :] = (o_all * ln_w + ln_b + bonus_all) * g_ref[s, rows, :]
        return carry

    lax.fori_loop(0, nchunk, chunk, 0)

    @pl.when(pl.program_id(2) == pl.num_programs(2) - 1)
    def _():
        s_out[...] = s_scr[...]


def _half_sums(x, lo):
    s_lo = jnp.sum(jnp.where(lo, x, 0.0), -1, keepdims=True)
    s_hi = jnp.sum(jnp.where(lo, 0.0, x), -1, keepdims=True)
    return jnp.where(lo, s_lo, s_hi)


def _wkv_pair_kernel(r_ref, lw_ref, k_ref, v_ref, a_ref, g_ref, hp_ref, s0_ref,
                     z_out, s_out, s_scr, *, bb, tl, c, prec):
    n = RW_HEAD
    npair = r_ref.shape[-1] // LANES
    nchunk = tl // c
    units = [(s, q) for s in range(bb) for q in range(npair)]

    @pl.when(pl.program_id(2) == 0)
    def _():
        for s, q in units:
            s_scr[s, q] = jnp.concatenate([s0_ref[s, 2 * q], s0_ref[s, 2 * q + 1]], axis=-1)

    def iota2(shape):
        return (lax.broadcasted_iota(jnp.int32, shape, 0),
                lax.broadcasted_iota(jnp.int32, shape, 1))

    row, lane = iota2((c, LANES))
    col = lane & (n - 1)
    lo_c = lane < n
    eye = jnp.where(row == col, 1.0, 0.0).astype(F32)
    row2, lane2 = iota2((2 * c, LANES))
    mask2 = (lane2 & (n - 1)) <= jnp.where(row2 < c, row2 - 1, row2 - c)
    rowt, colt = iota2((c, c))
    tril = jnp.where(rowt >= colt, 1.0, 0.0).astype(F32)
    lo_s = iota2((n, LANES))[1] < n

    def same_block(shift):
        return (row >> shift) == (col >> shift)

    def expand(y):
        rows = y.shape[0]
        r_e, l_e = iota2((2 * rows, LANES))
        same_head = (r_e >> (rows.bit_length() - 1)) == (l_e >> (n.bit_length() - 1))
        return jnp.where(same_head, jnp.concatenate([y, y], axis=0), 0.0)

    def mm(x, y):
        return _dot(x, expand(y), prec)

    k_k = hp_ref[0:1, :]
    k_a = hp_ref[1:2, :]
    r_k = hp_ref[2:3, :]
    ln_w = hp_ref[3:4, :]
    ln_b = hp_ref[4:5, :]

    def chunk(ci, carry):
        rows = pl.ds(pl.multiple_of(ci * c, c), c)
        lw = [lw_ref[s, rows, :] for s in range(bb)]
        cum = [_dot(tril, x, HIGHEST) for x in lw]
        seq = []
        for s in range(bb):
            r = r_ref[s, rows, :]
            k = k_ref[s, rows, :]
            a = a_ref[s, rows, :]
            cum_end = cum[s][c - 1:c, :]
            e_neg = jnp.exp(-cum[s])
            e_end = jnp.exp(cum_end - cum[s])
            k_mod = k * (1.0 + (a - 1.0) * k_a)
            seq.append(dict(
                a=a, v=v_ref[s, rows, :], kk_raw=k * k_k,
                r_t=r * jnp.exp(cum[s]), e_neg=e_neg, e_end=e_end,
                e_prev=jnp.exp(cum[s] - lw[s]), g_end=jnp.exp(cum_end),
                k_t=k_mod * e_neg, k_end=k_mod * e_end, rk=r * k_mod * r_k))
        un = []
        for s, q in units:
            p = seq[s]
            sl = slice(q * LANES, (q + 1) * LANES)
            kk = p["kk_raw"][:, sl]
            kk = kk * lax.rsqrt(jnp.maximum(_half_sums(kk * kk, lo_c), 1e-24))
            b_h = kk * p["a"][:, sl]
            a_t = -kk * p["e_prev"][:, sl]
            un.append(dict(
                ar=jnp.concatenate([a_t, p["r_t"][:, sl]], axis=0),
                bk=jnp.concatenate([expand(b_h * p["e_neg"][:, sl]), expand(p["k_t"][:, sl])],
                                   axis=0),
                kb_end=jnp.concatenate([p["k_end"][:, sl], b_h * p["e_end"][:, sl]], axis=0),
                v=p["v"][:, sl], g_end=p["g_end"][:, sl],
                bonus=_half_sums(p["rk"][:, sl], lo_c) * p["v"][:, sl]))
        g_all = [_dot_nt(x["ar"], x["bk"], prec) for x in un]
        ab = [jnp.where(mask2, m[:, :2 * c], 0.0) for m in g_all]
        ak = [jnp.where(mask2, m[:, 2 * c:], 0.0) for m in g_all]
        t_inv = _unit_lower_inverse([m[:c] for m in ab], eye, same_block, c, mm)
        s0 = [s_scr[s, q] for s, q in units]
        x1 = [_dot_nt(x["ar"], expand(si), prec) for x, si in zip(un, s0)]
        x1 = [xi + mm(m, x["v"]) for xi, m, x in zip(x1, ak, un)]
        u = [mm(ti, xi[:c]) for ti, xi in zip(t_inv, x1)]
        o = [xi[c:] + mm(m[c:], ui) for xi, m, ui in zip(x1, ab, u)]
        w = [_dot_tn(jnp.concatenate([x["v"], ui], axis=0), x["kb_end"], prec)
             for x, ui in zip(un, u)]
        for (s, q), si, wi, x in zip(units, s0, w, un):
            s_scr[s, q] = si * x["g_end"] + jnp.where(lo_s, wi[:n], wi[n:])
        normed = []
        for oi in o:
            mean = _half_sums(oi, lo_c) * (1.0 / n)
            var = _half_sums(jnp.square(oi - mean), lo_c) * (1.0 / n)
            normed.append((oi - mean) * lax.rsqrt(var + RW_GN_EPS))
        for s in range(bb):
            idx = [i for i, (si, _) in enumerate(units) if si == s]
            o_all = jnp.concatenate([normed[i] for i in idx], axis=-1)
            bonus_all = jnp.concatenate([un[i]["bonus"] for i in idx], axis=-1)
            z_out[s, rows, :] = (o_all * ln_w + ln_b + bonus_all) * g_ref[s, rows, :]
        return carry

    lax.fori_loop(0, nchunk, chunk, 0)

    @pl.when(pl.program_id(2) == pl.num_programs(2) - 1)
    def _():
        for s, q in units:
            s_out[s, 2 * q] = s_scr[s, q][:, :n]
            s_out[s, 2 * q + 1] = s_scr[s, q][:, n:]


def wkv_scan(r, lw, k, v, a, g, hp, s0, *, bb, tl, c, nheads, prec):
    bsz, L, d = r.shape
    width = nheads * RW_HEAD
    seq = pl.BlockSpec((bb, tl, width), lambda b, h, l: (b, l, h))
    st = pl.BlockSpec((bb, nheads, RW_HEAD, RW_HEAD), lambda b, h, l: (b, h, 0, 0))
    paired = 2 * c == LANES and nheads % 2 == 0
    body = _wkv_pair_kernel if paired else _wkv_kernel
    scr = (bb, nheads // 2, RW_HEAD, LANES) if paired else (bb, nheads, RW_HEAD, RW_HEAD)
    return pl.pallas_call(
        functools.partial(body, bb=bb, tl=tl, c=c, prec=prec),
        grid=(bsz // bb, d // width, L // tl),
        in_specs=[seq] * 6 + [pl.BlockSpec((SUBLANES, width), lambda b, h, l: (0, h)), st],
        out_specs=[seq, st],
        out_shape=[jax.ShapeDtypeStruct((bsz, L, d), F32),
                   jax.ShapeDtypeStruct(s0.shape, F32)],
        scratch_shapes=[pltpu.VMEM(scr, F32)],
        compiler_params=_cparams(("parallel", "parallel", "arbitrary")),
        name="wkv_scan",
    )(r, lw, k, v, a, g, hp, s0)


def rwkv_layer(x, shift, wkv, v_first, prm, cfg, norm_w=None):
    r, lw, k, v, a, g, last = rwkv_proj(x, shift, v_first, prm, norm_w,
                                        bb=cfg["proj_bb"], tl=cfg["proj_tl"])
    z, wkv_new = wkv_scan(r, lw, k, v, a, g, prm["hp"], wkv,
                          bb=cfg["scan_bb"], tl=cfg["wkv_tl"], c=cfg["wkv_c"],
                          nheads=cfg["wkv_heads"], prec=SCAN_PREC)
    return ((z, prm["w_o"]), last.reshape(last.shape[0], -1), wkv_new,
            (v if v_first is None else v_first))


def _prep_rwkv(mu, w0, w1, w2, a0, a1, a2, g1, g2, k_k, k_a, r_k, w_rkv, w_o, lnx_w, lnx_b,
               v_lora=None):
    d = D_MODEL
    zero = jnp.zeros((d,), F32)
    prm = dict(mu=mu, w_rkv=w_rkv.astype(BF16), w_o=w_o.astype(BF16),
               w1=w1.astype(BF16), w2=w2.astype(BF16), a1=a1.astype(BF16), a2=a2.astype(BF16),
               g1=g1.astype(BF16), g2=g2.astype(BF16))
    v0 = zero
    if v_lora is not None:
        v0, v1, v2 = v_lora
        prm["v1"] = v1.astype(BF16)
        prm["v2"] = v2.astype(BF16)
    prm["b0"] = jnp.stack([w0, a0, v0] + [zero] * 5)
    prm["hp"] = jnp.stack([k_k, k_a, r_k.reshape(d), lnx_w, lnx_b] + [zero] * 3)
    return prm


S5_CH = 8
S5_CL = S5_LANES // S5_CH


def _s5_kernel(u_ref, w_ref, cm_ref, ar_ref, ai_ref, d_ref, h0r_ref, h0i_ref,
               g_out, hr_out, hi_out, hs, hc, *, bsz, tr):
    i = pl.program_id(1)
    cl = S5_CL

    @pl.when(i == 0)
    def _():
        hc[:, :cl] = h0r_ref[...]
        hc[:, cl:] = h0i_ref[...]

    u = u_ref[...]
    hs[...] = _dot(u, w_ref[...])
    ar = jnp.broadcast_to(ar_ref[...], (bsz, cl))
    ai = jnp.broadcast_to(ai_ref[...], (bsz, cl))

    def step(t, carry):
        hr, hi = carry
        rows = pl.ds(pl.multiple_of(t * bsz, bsz), bsz)
        nr = ar * hr - ai * hi + hs[rows, :cl]
        ni = ar * hi + ai * hr + hs[rows, cl:]
        hs[rows, :cl] = nr
        hs[rows, cl:] = ni
        return nr, ni

    nstep = tr // bsz
    hr, hi = lax.fori_loop(0, nstep, step, (hc[:, :cl], hc[:, cl:]),
                           unroll=min(nstep, SUBLANES))
    hc[:, :cl] = hr
    hc[:, cl:] = hi
    y = _dot(hs[...], cm_ref[...]) + d_ref[...] * u
    g_out[...] = jax.nn.gelu(y)

    @pl.when(i == pl.num_programs(1) - 1)
    def _():
        hr_out[...] = hr
        hi_out[...] = hi


def s5_scan(u_tm, h0r, h0i, prm, *, bsz, tr):
    rows, d = u_tm.shape
    cl = S5_CL
    st = pl.BlockSpec((bsz, cl), lambda c, i: (0, c))
    return pl.pallas_call(
        functools.partial(_s5_kernel, bsz=bsz, tr=tr),
        grid=(S5_CH, rows // tr),
        in_specs=[
            pl.BlockSpec((tr, LANES), lambda c, i: (i, c)),
            pl.BlockSpec((None, LANES, 2 * cl), lambda c, i: (c, 0, 0)),
            pl.BlockSpec((None, 2 * cl, LANES), lambda c, i: (c, 0, 0)),
            pl.BlockSpec((None, 1, cl), lambda c, i: (c, 0, 0)),
            pl.BlockSpec((None, 1, cl), lambda c, i: (c, 0, 0)),
            pl.BlockSpec((1, LANES), lambda c, i: (0, c)),
            st, st,
        ],
        out_specs=[pl.BlockSpec((tr, LANES), lambda c, i: (i, c)), st, st],
        out_shape=[jax.ShapeDtypeStruct((rows, d), F32),
                   jax.ShapeDtypeStruct((bsz, S5_LANES), F32),
                   jax.ShapeDtypeStruct((bsz, S5_LANES), F32)],
        scratch_shapes=[pltpu.VMEM((tr, 2 * cl), F32), pltpu.VMEM((bsz, 2 * cl), F32)],
        compiler_params=_cparams(("parallel", "arbitrary")),
        name="s5_scan",
    )(u_tm, prm["w_in"], prm["c_out"], prm["ab_re"], prm["ab_im"], prm["d"], h0r, h0i)


def _prep_s5(a_re, a_im, log_dt, b_re, b_im, c_re, c_im, d_skip, glu_v, glu_g):
    dt = jnp.exp(log_dt.astype(F32))[:, None]
    lr, li = a_re.astype(F32), a_im.astype(F32)
    mag = jnp.exp(lr * dt)
    ab_re, ab_im = mag * jnp.cos(li * dt), mag * jnp.sin(li * dt)
    den = lr * lr + li * li
    q_re = ((ab_re - 1.0) * lr + ab_im * li) / den
    q_im = (ab_im * lr - (ab_re - 1.0) * li) / den
    bb_re = q_re[..., None] * b_re - q_im[..., None] * b_im
    bb_im = q_re[..., None] * b_im + q_im[..., None] * b_re
    gl = S5_GROUPS // S5_CH
    eye = jnp.eye(gl, dtype=F32)

    def in_blocks(bb):
        t = bb.reshape(S5_CH, gl, S5_STATE, S5_GROUP)
        return jnp.einsum('cgph,gk->cghkp', t, eye).reshape(S5_CH, LANES, S5_CL)

    def out_blocks(cc):
        t = cc.reshape(S5_CH, gl, S5_GROUP, S5_STATE)
        return jnp.einsum('cghp,gk->ckpgh', t, eye).reshape(S5_CH, S5_CL, LANES)

    return dict(
        w_in=jnp.concatenate([in_blocks(bb_re), in_blocks(bb_im)], axis=-1),
        c_out=jnp.concatenate([out_blocks(c_re.astype(F32)), -out_blocks(c_im.astype(F32))],
                              axis=1),
        ab_re=ab_re.reshape(S5_CH, 1, S5_CL), ab_im=ab_im.reshape(S5_CH, 1, S5_CL),
        d=d_skip.reshape(1, D_MODEL),
        glu_v=glu_v.astype(BF16), glu_g=glu_g.astype(BF16))


def s5_layer(xn, h_re0, h_im0, prm, cfg):
    bsz = h_re0.shape[0]
    if cfg["time_major"]:
        L, _, d = xn.shape
        u_tm = xn.reshape(L * bsz, d)
    else:
        _, L, d = xn.shape
        u_tm = jnp.swapaxes(xn, 0, 1).reshape(L * bsz, d)
    g, hr, hi = s5_scan(u_tm, h_re0.reshape(bsz, S5_LANES), h_im0.reshape(bsz, S5_LANES), prm,
                        bsz=bsz, tr=cfg["s5_tr"])
    y = glu_rows(g, prm["glu_v"], prm["glu_g"], tm=cfg["tm"]).reshape(L, bsz, d)
    if not cfg["time_major"]:
        y = jnp.swapaxes(y, 0, 1)
    return ((y, None), hr.reshape(bsz, S5_GROUPS, S5_STATE), hi.reshape(bsz, S5_GROUPS, S5_STATE))


def _mamba_in_kernel(x_ref, cp_ref, wz_ref, wx_ref, wdt_ref, cw_ref, cb_ref, dtb_ref,
                     z_out, xs_out, b_out, c_out, dt_out, nc_out, buf, *, bb, tl):
    d = x_ref.shape[-1]
    m = bb * tl
    cd = M_CONV_DIM
    l = pl.program_id(1)

    @pl.when(l == 0)
    def _():
        buf[:, :SUBLANES - (M_CONV - 1), :] = jnp.zeros((bb, SUBLANES - (M_CONV - 1), cd), F32)
        buf[:, SUBLANES - (M_CONV - 1):SUBLANES, :] = cp_ref[...]

    @pl.when(l > 0)
    def _():
        buf[:, :SUBLANES, :] = buf[:, tl:, :]

    x = x_ref[...].reshape(m, d).astype(BF16)
    nbc = M_GROUPS * M_STATE
    for ci in range(cd // nbc):
        cs = slice(ci * nbc, (ci + 1) * nbc)
        xbc = _dot(x, wx_ref[:, cs])
        buf[:, SUBLANES:, cs] = xbc.reshape(bb, tl, nbc)
        conv = cb_ref[:, cs] + xbc * cw_ref[M_CONV - 1:M_CONV, cs]
        for j in range(1, M_CONV):
            sh = buf[:, SUBLANES - j:SUBLANES - j + tl, cs]
            conv = conv + sh.reshape(m, nbc) * cw_ref[M_CONV - 1 - j:M_CONV - j, cs]
        act = (conv * jax.nn.sigmoid(conv)).reshape(bb, tl, nbc)
        if ci * nbc < M_INNER:
            xs_out[:, :, cs] = act
            z_out[:, :, cs] = _dot(x, wz_ref[:, cs]).reshape(bb, tl, nbc)
        elif ci * nbc == M_INNER:
            b_out[...] = act
        else:
            c_out[...] = act
    nc_out[...] = buf[:, tl + SUBLANES - (M_CONV - 1):, :]
    dt_raw = _dot(x, wdt_ref[...])[:, :M_HEADS]
    dt_out[...] = _softplus(dt_raw + dtb_ref[...]).reshape(bb, tl, M_HEADS)


def mamba_in(xn, conv_prev, prm, *, bb, tl):
    bsz, L, d = xn.shape
    cd = M_CONV_DIM
    nbc = M_GROUPS * M_STATE
    seq = lambda w: pl.BlockSpec((bb, tl, w), lambda b, l: (b, l, 0))
    cps = pl.BlockSpec((bb, M_CONV - 1, cd), lambda b, l: (b, 0, 0))
    return pl.pallas_call(
        functools.partial(_mamba_in_kernel, bb=bb, tl=tl),
        grid=(bsz // bb, L // tl),
        in_specs=[seq(d), cps, _resident((d, M_INNER)), _resident((d, cd)),
                  _resident((d, LANES)),
                  _full((M_CONV, cd)), _full((1, cd)), _full((1, M_HEADS))],
        out_specs=[seq(M_INNER), seq(M_INNER), seq(nbc), seq(nbc), seq(M_HEADS), cps],
        out_shape=[jax.ShapeDtypeStruct((bsz, L, M_INNER), F32),
                   jax.ShapeDtypeStruct((bsz, L, M_INNER), F32),
                   jax.ShapeDtypeStruct((bsz, L, nbc), F32),
                   jax.ShapeDtypeStruct((bsz, L, nbc), F32),
                   jax.ShapeDtypeStruct((bsz, L, M_HEADS), F32),
                   jax.ShapeDtypeStruct((bsz, M_CONV - 1, cd), F32)],
        scratch_shapes=[pltpu.VMEM((bb, SUBLANES + tl, cd), F32)],
        compiler_params=_cparams(("parallel", "arbitrary")),
        name="mamba_in",
    )(xn, conv_prev, prm["wz"], prm["wx"], prm["wdt"], prm["conv_w"], prm["conv_b"],
      prm["dt_bias"])


def _ssd_kernel(xs_ref, b_ref, c_ref, z_ref, dt_ref, hv_ref, ex_ref, nw_ref, s0_ref,
                y_out, s_out, s_scr, *, bb, tl, c):
    p, n = M_HEADDIM, M_STATE
    hpg = M_HEADS // M_GROUPS
    gw = M_INNER // M_GROUPS
    nchunk = tl // c

    @pl.when(pl.program_id(1) == 0)
    def _():
        s_scr[...] = s0_ref[...]

    row = lax.broadcasted_iota(jnp.int32, (c, c), 0)
    col = lax.broadcasted_iota(jnp.int32, (c, c), 1)
    incl = row >= col
    tril = jnp.where(incl, 1.0, 0.0).astype(F32)
    triu = jnp.where(row <= col, 1.0, 0.0).astype(F32)
    eye = jnp.where(row == col, 1.0, 0.0).astype(F32)
    row_p = lax.broadcasted_iota(jnp.int32, (c, 2 * c), 0)
    lane_p = lax.broadcasted_iota(jnp.int32, (c, 2 * c), 1)
    lo_pair = lane_p < c
    incl_pair = row_p >= (lane_p & (c - 1))
    row_b = lax.broadcasted_iota(jnp.int32, (2 * c, 2 * p), 0)
    lane_b = lax.broadcasted_iota(jnp.int32, (2 * c, 2 * p), 1)
    bd_mask = (row_b >> (c.bit_length() - 1)) == (lane_b >> (p.bit_length() - 1))
    a_neg = hv_ref[0:1, :]
    d_exp = nw_ref[1:2, :]
    norm_w = nw_ref[0:1, :]
    expand = ex_ref[...]

    def chunk(ci, carry):
        rows = pl.ds(pl.multiple_of(ci * c, c), c)
        pre = []
        for s in range(bb):
            xs = xs_ref[s, rows, :]
            dt = dt_ref[s, rows, :]
            a = dt * a_neg
            cum = _dot(tril, a, HIGHEST)
            cum_end = cum[c - 1:c, :]
            pre.append(dict(
                xs=xs, cum=cum,
                cum_t=_dot_tn(a, triu, HIGHEST),
                dt_t=_dot_tn(dt, eye, HIGHEST),
                e_end=jnp.exp(cum_end),
                xend=xs * _dot(dt * jnp.exp(cum_end - cum), expand),
                ecum_x=_dot(jnp.exp(cum), expand)))
        yd = [[] for _ in range(bb)]
        yo = [[] for _ in range(bb)]
        for gi in range(M_GROUPS):
            for s in range(bb):
                q_ = pre[s]
                xs, cum, cum_t, dt_t = q_["xs"], q_["cum"], q_["cum_t"], q_["dt_t"]
                b_g = b_ref[s, rows, gi * n:(gi + 1) * n]
                c_g = c_ref[s, rows, gi * n:(gi + 1) * n]
                cb = _dot_nt(c_g, b_g)
                cb2 = jnp.concatenate([cb, cb], axis=1)
                for q in range(hpg // 2):
                    h0 = gi * hpg + 2 * q
                    col = jnp.where(lo_pair, cum[:, h0:h0 + 1], cum[:, h0 + 1:h0 + 2])
                    rowv = jnp.concatenate([cum_t[h0:h0 + 1, :], cum_t[h0 + 1:h0 + 2, :]], axis=1)
                    dtrow = jnp.concatenate([dt_t[h0:h0 + 1, :], dt_t[h0 + 1:h0 + 2, :]], axis=1)
                    lmat = jnp.where(incl_pair, jnp.exp(jnp.where(incl_pair, col - rowv, 0.0)), 0.0)
                    x_pair = xs[:, h0 * p:(h0 + 2) * p]
                    x_bd = jnp.where(bd_mask, jnp.concatenate([x_pair, x_pair], axis=0), 0.0)
                    yd[s].append(_dot(cb2 * lmat * dtrow, x_bd))
                s_g = s_scr[s, gi * hpg:(gi + 1) * hpg].reshape(hpg * p, n)
                yo[s].append(_dot_nt(c_g, s_g))
                upd = _dot_tn(q_["xend"][:, gi * gw:(gi + 1) * gw], b_g)
                for hh in range(hpg):
                    h = gi * hpg + hh
                    s_scr[s, h] = (s_g[hh * p:(hh + 1) * p] * q_["e_end"][:, h:h + 1]
                                   + upd[hh * p:(hh + 1) * p])
        for s in range(bb):
            q_ = pre[s]
            y = (jnp.concatenate(yd[s], axis=-1) + jnp.concatenate(yo[s], axis=-1) * q_["ecum_x"]
                 + d_exp * q_["xs"])
            zz = z_ref[s, rows, :]
            y = y * (zz * jax.nn.sigmoid(zz))
            outs = []
            for gi in range(M_GROUPS):
                yg = y[:, gi * gw:(gi + 1) * gw]
                outs.append(yg * lax.rsqrt(jnp.mean(yg * yg, -1, keepdims=True) + EPS))
            y_out[s, rows, :] = jnp.concatenate(outs, axis=-1) * norm_w
        return carry

    lax.fori_loop(0, nchunk, chunk, 0)

    @pl.when(pl.program_id(1) == pl.num_programs(1) - 1)
    def _():
        s_out[...] = s_scr[...]


def ssd_scan(xs, bm, cm, z, dt, prm, s0, *, bb, tl, c):
    bsz, L, _ = xs.shape
    nbc = M_GROUPS * M_STATE
    seq = lambda w: pl.BlockSpec((bb, tl, w), lambda b, l: (b, l, 0))
    st = pl.BlockSpec((bb, M_HEADS, M_HEADDIM, M_STATE), lambda b, l: (b, 0, 0, 0))
    return pl.pallas_call(
        functools.partial(_ssd_kernel, bb=bb, tl=tl, c=c),
        grid=(bsz // bb, L // tl),
        in_specs=[seq(M_INNER), seq(nbc), seq(nbc), seq(M_INNER), seq(M_HEADS),
                  _full((SUBLANES, M_HEADS)), _full((M_HEADS, M_INNER)),
                  _full((SUBLANES, M_INNER)), st],
        out_specs=[seq(M_INNER), st],
        out_shape=[jax.ShapeDtypeStruct((bsz, L, M_INNER), F32),
                   jax.ShapeDtypeStruct(s0.shape, F32)],
        scratch_shapes=[pltpu.VMEM((bb, M_HEADS, M_HEADDIM, M_STATE), F32)],
        compiler_params=_cparams(("parallel", "arbitrary")),
        name="ssd_scan",
    )(xs, bm, cm, z, dt, prm["hv"], prm["expand"], prm["nw"], s0)


def _prep_mamba(in_proj, conv_w, conv_b, dt_bias, a_log, d_skip, norm_w, out_proj):
    d = D_MODEL
    wdt = jnp.zeros((d, LANES), F32).at[:, :M_HEADS].set(in_proj[:, M_INNER + M_CONV_DIM:])
    expand = jnp.repeat(jnp.eye(M_HEADS, dtype=F32), M_HEADDIM, axis=1)
    zrow_h = jnp.zeros((M_HEADS,), F32)
    zrow_i = jnp.zeros((M_INNER,), F32)
    return dict(
        wz=in_proj[:, :M_INNER].astype(BF16),
        wx=in_proj[:, M_INNER:M_INNER + M_CONV_DIM].astype(BF16),
        wdt=wdt.astype(BF16),
        conv_w=conv_w, conv_b=conv_b.reshape(1, M_CONV_DIM),
        dt_bias=dt_bias.reshape(1, M_HEADS),
        hv=jnp.stack([-jnp.exp(a_log.astype(F32))] + [zrow_h] * 7),
        expand=expand,
        nw=jnp.stack([norm_w, jnp.repeat(d_skip, M_HEADDIM)] + [zrow_i] * 6),
        out_proj=out_proj.astype(BF16))


def mamba_layer(xn, conv_prev, ssm_prev, prm, cfg):
    bsz, L, d = xn.shape
    z, xs, bm, cm, dt, new_conv = mamba_in(xn, conv_prev, prm, bb=cfg["mamba_bb"],
                                           tl=cfg["mamba_tl"])
    y, ssm_new = ssd_scan(xs, bm, cm, z, dt, prm, ssm_prev,
                          bb=cfg["ssd_bb"], tl=cfg["ssd_tl"], c=cfg["ssd_c"])
    return (y, prm["out_proj"]), new_conv, ssm_new


def _config(bsz, L):
    if L >= 512:
        return dict(tm=512, ffn_fb=bsz, ffn_ft=512 // bsz, time_major=bsz == SUBLANES,
                    proj_bb=1, proj_tl=256,
                    scan_bb=min(bsz, 8), wkv_tl=256, wkv_c=64, wkv_heads=4,
                    mamba_bb=1, mamba_tl=256,
                    ssd_bb=2, ssd_tl=256, ssd_c=M_CHUNK, s5_tr=2048)
    rows = bsz * L
    tm = min(512, rows)
    return dict(tm=tm, ffn_fb=tm // L, ffn_ft=L, time_major=False,
                proj_bb=min(bsz, 256 // L), proj_tl=L,
                scan_bb=min(bsz, 16), wkv_tl=L, wkv_c=L, wkv_heads=2,
                mamba_bb=min(bsz, 256 // L), mamba_tl=L,
                ssd_bb=min(bsz, 4), ssd_tl=L, ssd_c=min(M_CHUNK, L), s5_tr=rows)


def _trunk(x, p, states, layer_prms, shared, cfg):
    bsz, L, d = x.shape
    xn = h = x
    v_first = None
    new_states = []
    for i in range(DEPTH):
        st_a, st_b = states[2 * i], states[2 * i + 1]
        prm = layer_prms[i]
        kind = i % 3
        if kind == 0:
            (mix, wo), n_a, n_b, v_first = rwkv_layer(
                xn, st_a, st_b, v_first, prm, cfg, shared["norm_mix"][0] if i == 0 else None)
        elif kind == 1:
            (mix, wo), n_a, n_b = s5_layer(xn, st_a, st_b, prm, cfg)
        else:
            (mix, wo), n_a, n_b = mamba_layer(xn, st_a, st_b, prm, cfg)
        new_states += [n_a, n_b]
        nxt = shared["norm_mix"][i + 1] if i + 1 < DEPTH else shared["final_norm"]
        mix_tm = cfg["time_major"] and kind == 1
        xn_tm = cfg["time_major"] and (i + 1) % 3 == 1 and i + 1 < DEPTH
        h, xn = ffn_ple(h, mix, wo, p, i, shared, nxt, fb=cfg["ffn_fb"], ft=cfg["ffn_ft"],
                        mix_tm=mix_tm, xn_tm=xn_tm)
    return xn, new_states


def kernel(x_prompt, x_sample, p_prompt, p_sample, state_l0_shift, state_l0_wkv, state_l1_s5_re, state_l1_s5_im, state_l2_conv, state_l2_ssm, state_l3_shift, state_l3_wkv, l0_mu, l0_w0, l0_w1, l0_w2, l0_a0, l0_a1, l0_a2, l0_g1, l0_g2, l0_k_k, l0_k_a, l0_r_k, l0_w_rkv, l0_w_o, l0_lnx_w, l0_lnx_b, l1_a_re, l1_a_im, l1_log_dt, l1_b_re, l1_b_im, l1_c_re, l1_c_im, l1_d, l1_glu_v, l1_glu_g, l2_in_proj, l2_conv_w, l2_conv_b, l2_dt_bias, l2_a_log, l2_d, l2_norm_w, l2_out_proj, l3_mu, l3_w0, l3_w1, l3_w2, l3_a0, l3_a1, l3_a2, l3_g1, l3_g2, l3_k_k, l3_k_a, l3_r_k, l3_w_rkv, l3_w_o, l3_lnx_w, l3_lnx_b, l3_v0, l3_v1, l3_v2, norm_mix, norm_ffn, norm_ple, ffn_w1, ffn_w3, ffn_w2, ple_proj, ple_gate, final_norm):
    layer_prms = (
        _prep_rwkv(l0_mu, l0_w0, l0_w1, l0_w2, l0_a0, l0_a1, l0_a2, l0_g1, l0_g2,
                   l0_k_k, l0_k_a, l0_r_k, l0_w_rkv, l0_w_o, l0_lnx_w, l0_lnx_b),
        _prep_s5(l1_a_re, l1_a_im, l1_log_dt, l1_b_re, l1_b_im, l1_c_re, l1_c_im,
                 l1_d, l1_glu_v, l1_glu_g),
        _prep_mamba(l2_in_proj, l2_conv_w, l2_conv_b, l2_dt_bias, l2_a_log, l2_d,
                    l2_norm_w, l2_out_proj),
        _prep_rwkv(l3_mu, l3_w0, l3_w1, l3_w2, l3_a0, l3_a1, l3_a2, l3_g1, l3_g2,
                   l3_k_k, l3_k_a, l3_r_k, l3_w_rkv, l3_w_o, l3_lnx_w, l3_lnx_b,
                   (l3_v0, l3_v1, l3_v2)),
    )
    shared = dict(norm_mix=norm_mix, norm_ffn=norm_ffn.reshape(DEPTH, 1, D_MODEL),
                  norm_ple=norm_ple.reshape(DEPTH, 1, D_MODEL), final_norm=final_norm,
                  ffn_w1=ffn_w1.astype(BF16), ffn_w3=ffn_w3.astype(BF16),
                  ffn_w2=ffn_w2.astype(BF16), ple_gate=ple_gate.astype(BF16),
                  ple_proj=ple_proj.astype(BF16))
    sample_states = [state_l0_shift, state_l0_wkv, state_l1_s5_re, state_l1_s5_im,
                     state_l2_conv, state_l2_ssm, state_l3_shift, state_l3_wkv]
    bp = x_prompt.shape[0]
    prompt_states = [jnp.zeros((bp,) + s.shape[1:], F32) for s in sample_states]
    y_prompt, new_p = _trunk(x_prompt, p_prompt, prompt_states, layer_prms, shared,
                             _config(*x_prompt.shape[:2]))
    y_sample, new_s = _trunk(x_sample, p_sample, sample_states, layer_prms, shared,
                             _config(*x_sample.shape[:2]))
    return (y_prompt, y_sample, *new_p, *new_s)
```

```python
import functools
import math

import jax
import jax.numpy as jnp
from jax import lax
from jax.experimental import pallas as pl
from jax.experimental.pallas import tpu as pltpu

F32 = jnp.float32
BF16 = jnp.bfloat16
HIGHEST = lax.Precision.HIGHEST

D_MODEL = 1024
DEPTH = 4
PLE_DIM = 256
EPS = 1e-6
D_FF = 2816
RW_HEAD = 64
RW_HEADS = D_MODEL // RW_HEAD
RW_GN_EPS = 64e-5
S5_GROUP = 16
S5_GROUPS = D_MODEL // S5_GROUP
S5_STATE = 64
S5_LANES = S5_GROUPS * S5_STATE
M_INNER = 2 * D_MODEL
M_HEADDIM = 64
M_HEADS = M_INNER // M_HEADDIM
M_STATE = 128
M_GROUPS = 4
M_CONV = 4
M_CHUNK = 64
M_CONV_DIM = M_INNER + 2 * M_GROUPS * M_STATE

LANES = 128
MXU_TILE = 256
SUBLANES = 8
VMEM_LIMIT = 56 * 1024 * 1024

SCAN_PREC = None


def _cparams(sem):
    return pltpu.CompilerParams(dimension_semantics=sem, vmem_limit_bytes=VMEM_LIMIT)


def _operands(a, b, prec):
    if prec is None:
        return a.astype(BF16), b.astype(BF16)
    return a, b


def _dot(a, b, prec=None):
    a, b = _operands(a, b, prec)
    return jnp.dot(a, b, preferred_element_type=F32, precision=prec)


def _dot_nt(a, b, prec=None):
    a, b = _operands(a, b, prec)
    return lax.dot_general(a, b, (((1,), (1,)), ((), ())),
                           preferred_element_type=F32, precision=prec)


def _dot_tn(a, b, prec=None):
    a, b = _operands(a, b, prec)
    return lax.dot_general(a, b, (((0,), (0,)), ((), ())),
                           preferred_element_type=F32, precision=prec)


def _rms(x, g):
    return x * lax.rsqrt(jnp.mean(x * x, axis=-1, keepdims=True) + EPS) * g


def _log_sigmoid(z):
    return -(jnp.maximum(-z, 0.0) + jnp.log1p(jnp.exp(-jnp.abs(z))))


def _softplus(z):
    return jnp.maximum(z, 0.0) + jnp.log1p(jnp.exp(-jnp.abs(z)))


def _full(shape):
    n = len(shape)
    return pl.BlockSpec(shape, lambda *_: (0,) * n)


def _resident(shape):
    n = len(shape)
    return pl.BlockSpec(shape, lambda *_: (0,) * n, pipeline_mode=pl.Buffered(1))


def _ffn_kernel(*refs, project, mix_tm, xn_tm):
    if project:
        (h_ref, mix_ref, wo_ref, p_ref, nf_ref, np_ref, nn_ref, w1_ref, w3_ref, w2_ref,
         pg_ref, pp_ref, h_out, xn_out) = refs
    else:
        (h_ref, mix_ref, p_ref, nf_ref, np_ref, nn_ref, w1_ref, w3_ref, w2_ref,
         pg_ref, pp_ref, h_out, xn_out) = refs
    fb, ft, d = h_ref.shape
    tm = fb * ft
    mix = mix_ref[...]
    if mix_tm:
        mix = jnp.swapaxes(mix, 0, 1)
    mix = mix.reshape(tm, mix.shape[-1])
    y = _dot(mix, wo_ref[...]) if project else mix
    hin = h_ref[...].reshape(tm, d) + y
    hn = _rms(hin, nf_ref[...]).astype(BF16)
    acc = None
    for ci in range(D_FF // MXU_TILE):
        cs = slice(ci * MXU_TILE, (ci + 1) * MXU_TILE)
        a = _dot(hn, w1_ref[:, cs])
        b = _dot(hn, w3_ref[:, cs])
        t = (a * jax.nn.sigmoid(a) * b).astype(BF16)
        part = _dot(t, w2_ref[cs, :])
        acc = part if acc is None else acc + part
    h2 = hin + acc
    gate = jax.nn.sigmoid(_dot(_rms(h2, np_ref[...]).astype(BF16), pg_ref[...]))
    h3 = h2 + gate * _dot(p_ref[...].reshape(tm, PLE_DIM).astype(BF16), pp_ref[...])
    h_out[...] = h3.reshape(fb, ft, d)
    xn = _rms(h3, nn_ref[...]).reshape(fb, ft, d)
    xn_out[...] = jnp.swapaxes(xn, 0, 1) if xn_tm else xn


def ffn_ple(h, mix, wo, p_all, layer, shared, nn, *, fb, ft, mix_tm=False, xn_tm=False):
    bsz, L, d = h.shape
    km = mix.shape[-1]
    nlt = L // ft
    seq = lambda w: pl.BlockSpec((fb, ft, w), lambda i: (i // nlt, i % nlt, 0))
    tmaj = lambda w: pl.BlockSpec((ft, fb, w), lambda i: (i % nlt, 0, 0))
    lay = lambda *shape: pl.BlockSpec((None,) + shape, lambda i: (layer,) + (0,) * len(shape),
                                      pipeline_mode=pl.Buffered(1))
    ins = [h, mix]
    specs = [seq(d), tmaj(km) if mix_tm else seq(km)]
    if wo is not None:
        ins.append(wo)
        specs.append(_resident(wo.shape))
    ins += [p_all, shared["norm_ffn"], shared["norm_ple"], nn.reshape(1, d),
            shared["ffn_w1"], shared["ffn_w3"], shared["ffn_w2"],
            shared["ple_gate"], shared["ple_proj"]]
    specs += [pl.BlockSpec((None, fb, ft, PLE_DIM), lambda i: (layer, i // nlt, i % nlt, 0)),
              lay(1, d), lay(1, d), _full((1, d)),
              lay(d, D_FF), lay(d, D_FF), lay(D_FF, d), lay(d, d), lay(PLE_DIM, d)]
    xn_shape = (L, bsz, d) if xn_tm else (bsz, L, d)
    return pl.pallas_call(
        functools.partial(_ffn_kernel, project=wo is not None, mix_tm=mix_tm, xn_tm=xn_tm),
        grid=(bsz // fb * nlt,),
        in_specs=specs,
        out_specs=[seq(d), tmaj(d) if xn_tm else seq(d)],
        out_shape=[jax.ShapeDtypeStruct((bsz, L, d), F32), jax.ShapeDtypeStruct(xn_shape, F32)],
        compiler_params=_cparams(("parallel",)),
        name="ffn_ple",
    )(*ins)


def _rwkv_proj_kernel(*refs, bb, tl, with_v, norm_in):
    it = iter(refs)
    x_ref, sh_ref = next(it), next(it)
    vf_ref = next(it) if with_v else None
    nw_ref = next(it) if norm_in else None
    mu_ref, b0_ref, wrkv_ref, w1_ref, w2_ref, a1_ref, a2_ref, g1_ref, g2_ref = (
        next(it) for _ in range(9))
    if with_v:
        v1_ref, v2_ref = next(it), next(it)
    r_out, lw_out, k_out, v_out, a_out, g_out, last_out, carry = it
    d = x_ref.shape[-1]
    m = bb * tl

    @pl.when(pl.program_id(1) == 0)
    def _():
        carry[...] = sh_ref[...]

    x = x_ref[...].reshape(m, d)
    if norm_in:
        x = _rms(x, nw_ref[...])
    prev = jnp.broadcast_to(carry[...], (bb, tl, d)).reshape(m, d)
    rolled = pltpu.roll(x, 1, 0)
    t_in_tile = lax.broadcasted_iota(jnp.int32, (m, d), 0) & (tl - 1)
    x_prev = jnp.where(t_in_tile == 0, prev, rolled)
    last = x.reshape(bb, tl, d)[:, tl - 1:tl, :]
    carry[...] = last
    last_out[...] = last

    xx = x_prev - x

    def mix(j):
        return (x + xx * mu_ref[j:j + 1, :]).astype(BF16)

    shp = (bb, tl, d)
    xw, xa, xg, xv = mix(1), mix(4), mix(5), mix(3)
    zw = b0_ref[0:1, :] + _dot(jnp.tanh(_dot(xw, w1_ref[...])).astype(BF16), w2_ref[...])
    lw_out[...] = (-jnp.exp(_log_sigmoid(zw) - 0.5)).reshape(shp)
    a = jax.nn.sigmoid(b0_ref[1:2, :] + _dot(_dot(xa, a1_ref[...]).astype(BF16), a2_ref[...]))
    a_out[...] = a.reshape(shp)
    g_out[...] = _dot(jax.nn.sigmoid(_dot(xg, g1_ref[...])).astype(BF16),
                      g2_ref[...]).reshape(shp)
    if with_v:
        lam = jax.nn.sigmoid(b0_ref[2:3, :]
                             + _dot(_dot(xv, v1_ref[...]).astype(BF16), v2_ref[...]))
    v = _dot(xv, wrkv_ref[2])
    if with_v:
        v = v + (vf_ref[...].reshape(m, d) - v) * lam
    v_out[...] = v.reshape(shp)
    r_out[...] = _dot(mix(0), wrkv_ref[0]).reshape(shp)
    k_out[...] = _dot(mix(2), wrkv_ref[1]).reshape(shp)


def rwkv_proj(x, shift, v_first, prm, norm_w, *, bb, tl):
    bsz, L, d = x.shape
    with_v = v_first is not None
    norm_in = norm_w is not None
    seq = pl.BlockSpec((bb, tl, d), lambda b, l: (b, l, 0))
    row = pl.BlockSpec((bb, 1, d), lambda b, l: (b, 0, 0))
    ins = [x, shift.reshape(bsz, 1, d)]
    specs = [seq, row]
    if with_v:
        ins.append(v_first)
        specs.append(seq)
    if norm_in:
        ins.append(norm_w.reshape(1, d))
        specs.append(_full((1, d)))
    names = ["mu", "b0", "w_rkv", "w1", "w2", "a1", "a2", "g1", "g2"]
    if with_v:
        names += ["v1", "v2"]
    for nme in names:
        ins.append(prm[nme])
        specs.append(_resident(prm[nme].shape))
    return pl.pallas_call(
        functools.partial(_rwkv_proj_kernel, bb=bb, tl=tl, with_v=with_v, norm_in=norm_in),
        grid=(bsz // bb, L // tl),
        in_specs=specs,
        out_specs=[seq] * 6 + [row],
        out_shape=[jax.ShapeDtypeStruct((bsz, L, d), F32)] * 6
        + [jax.ShapeDtypeStruct((bsz, 1, d), F32)],
        scratch_shapes=[pltpu.VMEM((bb, 1, d), F32)],
        compiler_params=_cparams(("parallel", "arbitrary")),
        name="rwkv_proj",
    )(*ins)


def _unit_lower_inverse(n_mats, eye, same_block, size, mm):
    blk8 = same_block(3)
    n8 = [jnp.where(blk8, m, 0.0) for m in n_mats]
    n8_2 = [mm(m, m) for m in n8]
    t = [eye + m for m in n8]
    t = [ti + mm(ti, m2) for ti, m2 in zip(t, n8_2)]
    n8_4 = [mm(m2, m2) for m2 in n8_2]
    t = [ti + mm(ti, m4) for ti, m4 in zip(t, n8_4)]
    shift = 3
    while (1 << shift) < size:
        inner, outer = same_block(shift), same_block(shift + 1)
        off = [jnp.where(inner, 0.0, jnp.where(outer, m, 0.0)) for m in n_mats]
        left = [mm(ti, oi) for ti, oi in zip(t, off)]
        t = [ti + mm(li, ti) for ti, li in zip(t, left)]
        shift += 1
    return t


def _wkv_kernel(r_ref, lw_ref, k_ref, v_ref, a_ref, g_ref, hp_ref, s0_ref,
                z_out, s_out, s_scr, *, bb, tl, c, prec):
    n = RW_HEAD
    nheads = s_scr.shape[1]
    nchunk = tl // c

    @pl.when(pl.program_id(2) == 0)
    def _():
        s_scr[...] = s0_ref[...]

    row = lax.broadcasted_iota(jnp.int32, (c, c), 0)
    col = lax.broadcasted_iota(jnp.int32, (c, c), 1)
    tril = jnp.where(row >= col, 1.0, 0.0).astype(F32)
    eye = jnp.where(row == col, 1.0, 0.0).astype(F32)

    def same_block(shift):
        return (row >> shift) == (col >> shift)

    row2 = lax.broadcasted_iota(jnp.int32, (2 * c, c), 0)
    col2 = lax.broadcasted_iota(jnp.int32, (2 * c, c), 1)
    mask2 = col2 <= jnp.where(row2 < c, row2 - 1, row2 - c)
    k_k = hp_ref[0:1, :]
    k_a = hp_ref[1:2, :]
    r_k = hp_ref[2:3, :]
    ln_w = hp_ref[3:4, :]
    ln_b = hp_ref[4:5, :]

    chains = [(s, h) for s in range(bb) for h in range(nheads)]

    def chunk(ci, carry):
        rows = pl.ds(pl.multiple_of(ci * c, c), c)
        lw = [lw_ref[s, rows, :] for s in range(bb)]
        cum = [_dot(tril, x, HIGHEST) for x in lw]
        seq = []
        for s in range(bb):
            r = r_ref[s, rows, :]
            k = k_ref[s, rows, :]
            a = a_ref[s, rows, :]
            cum_end = cum[s][c - 1:c, :]
            e_neg = jnp.exp(-cum[s])
            e_end = jnp.exp(cum_end - cum[s])
            k_mod = k * (1.0 + (a - 1.0) * k_a)
            seq.append(dict(
                a=a, v=v_ref[s, rows, :], kk_raw=k * k_k,
                r_t=r * jnp.exp(cum[s]), e_neg=e_neg, e_end=e_end,
                e_prev=jnp.exp(cum[s] - lw[s]), g_end=jnp.exp(cum_end),
                k_t=k_mod * e_neg, k_end=k_mod * e_end, rk=r * k_mod * r_k))
        ch = []
        for s, h in chains:
            q = seq[s]
            sl = slice(h * n, (h + 1) * n)
            kk = q["kk_raw"][:, sl]
            kk = kk * lax.rsqrt(jnp.maximum(jnp.sum(kk * kk, -1, keepdims=True), 1e-24))
            b_h = kk * q["a"][:, sl]
            a_t = -kk * q["e_prev"][:, sl]
            ch.append(dict(
                ar=jnp.concatenate([a_t, q["r_t"][:, sl]], axis=0),
                b_t=b_h * q["e_neg"][:, sl], k_t=q["k_t"][:, sl],
                kb_end=jnp.concatenate([q["k_end"][:, sl], b_h * q["e_end"][:, sl]], axis=0),
                v=q["v"][:, sl], g_end=q["g_end"][:, sl],
                bonus=jnp.sum(q["rk"][:, sl], -1, keepdims=True) * q["v"][:, sl]))
        ab = [jnp.where(mask2, _dot_nt(x["ar"], x["b_t"], prec), 0.0) for x in ch]
        ak = [jnp.where(mask2, _dot_nt(x["ar"], x["k_t"], prec), 0.0) for x in ch]
        t_inv = _unit_lower_inverse([m[:c] for m in ab], eye, same_block, c,
                                    lambda x, y: _dot(x, y, prec))
        s0 = [s_scr[s, h] for s, h in chains]
        x1 = [_dot_nt(x["ar"], si, prec) for x, si in zip(ch, s0)]
        x1 = [xi + _dot(m, x["v"], prec) for xi, m, x in zip(x1, ak, ch)]
        u = [_dot(ti, xi[:c], prec) for ti, xi in zip(t_inv, x1)]
        o = [xi[c:] + _dot(m[c:], ui, prec) for xi, m, ui in zip(x1, ab, u)]
        s_new = [si * x["g_end"] + _dot_tn(jnp.concatenate([x["v"], ui], axis=0), x["kb_end"], prec)
                 for x, si, ui in zip(ch, s0, u)]
        for (s, h), sn in zip(chains, s_new):
            s_scr[s, h] = sn
        normed = []
        for oi in o:
            mean = jnp.mean(oi, -1, keepdims=True)
            var = jnp.mean(jnp.square(oi - mean), -1, keepdims=True)
            normed.append((oi - mean) * lax.rsqrt(var + RW_GN_EPS))
        for s in range(bb):
            idx = [i for i, (si, _) in enumerate(chains) if si == s]
            o_all = jnp.concatenate([normed[i] for i in idx], axis=-1)
            bonus_all = jnp.concatenate([ch[i]["bonus"] for i in idx], axis=-1)
            z_out[s, rows, :] = (o_all * ln_w + ln_b + bonus_all) * g_ref[s, rows, :]
        return carry

    lax.fori_loop(0, nchunk, chunk, 0)

    @pl.when(pl.program_id(2) == pl.num_programs(2) - 1)
    def _():
        s_out[...] = s_scr[...]


def _half_sums(x, lo):
    s_lo = jnp.sum(jnp.where(lo, x, 0.0), -1, keepdims=True)
    s_hi = jnp.sum(jnp.where(lo, 0.0, x), -1, keepdims=True)
    return jnp.where(lo, s_lo, s_hi)


def _wkv_pair_kernel(r_ref, lw_ref, k_ref, v_ref, a_ref, g_ref, hp_ref, s0_ref,
                     z_out, s_out, s_scr, *, bb, tl, c, prec):
    n = RW_HEAD
    npair = r_ref.shape[-1] // LANES
    nchunk = tl // c
    units = [(s, q) for s in range(bb) for q in range(npair)]

    @pl.when(pl.program_id(2) == 0)
    def _():
        for s, q in units:
            s_scr[s, q] = jnp.concatenate([s0_ref[s, 2 * q], s0_ref[s, 2 * q + 1]], axis=-1)

    def iota2(shape):
        return (lax.broadcasted_iota(jnp.int32, shape, 0),
                lax.broadcasted_iota(jnp.int32, shape, 1))

    row, lane = iota2((c, LANES))
    col = lane & (n - 1)
    lo_c = lane < n
    eye = jnp.where(row == col, 1.0, 0.0).astype(F32)
    row2, lane2 = iota2((2 * c, LANES))
    mask2 = (lane2 & (n - 1)) <= jnp.where(row2 < c, row2 - 1, row2 - c)
    rowt, colt = iota2((c, c))
    tril = jnp.where(rowt >= colt, 1.0, 0.0).astype(F32)
    lo_s = iota2((n, LANES))[1] < n

    def same_block(shift):
        return (row >> shift) == (col >> shift)

    def expand(y):
        rows = y.shape[0]
        r_e, l_e = iota2((2 * rows, LANES))
        same_head = (r_e >> (rows.bit_length() - 1)) == (l_e >> (n.bit_length() - 1))
        return jnp.where(same_head, jnp.concatenate([y, y], axis=0), 0.0)

    def mm(x, y):
        return _dot(x, expand(y), prec)

    k_k = hp_ref[0:1, :]
    k_a = hp_ref[1:2, :]
    r_k = hp_ref[2:3, :]
    ln_w = hp_ref[3:4, :]
    ln_b = hp_ref[4:5, :]

    def chunk(ci, carry):
        rows = pl.ds(pl.multiple_of(ci * c, c), c)
        lw = [lw_ref[s, rows, :] for s in range(bb)]
        cum = [_dot(tril, x, HIGHEST) for x in lw]
        seq = []
        for s in range(bb):
            r = r_ref[s, rows, :]
            k = k_ref[s, rows, :]
            a = a_ref[s, rows, :]
            cum_end = cum[s][c - 1:c, :]
            e_neg = jnp.exp(-cum[s])
            e_end = jnp.exp(cum_end - cum[s])
            k_mod = k * (1.0 + (a - 1.0) * k_a)
            seq.append(dict(
                a=a, v=v_ref[s, rows, :], kk_raw=k * k_k,
                r_t=r * jnp.exp(cum[s]), e_neg=e_neg, e_end=e_end,
                e_prev=jnp.exp(cum[s] - lw[s]), g_end=jnp.exp(cum_end),
                k_t=k_mod * e_neg, k_end=k_mod * e_end, rk=r * k_mod * r_k))
        un = []
        for s, q in units:
            p = seq[s]
            sl = slice(q * LANES, (q + 1) * LANES)
            kk = p["kk_raw"][:, sl]
            kk = kk * lax.rsqrt(jnp.maximum(_half_sums(kk * kk, lo_c), 1e-24))
            b_h = kk * p["a"][:, sl]
            a_t = -kk * p["e_prev"][:, sl]
            un.append(dict(
                ar=jnp.concatenate([a_t, p["r_t"][:, sl]], axis=0),
                bk=jnp.concatenate([expand(b_h * p["e_neg"][:, sl]), expand(p["k_t"][:, sl])],
                                   axis=0),
                kb_end=jnp.concatenate([p["k_end"][:, sl], b_h * p["e_end"][:, sl]], axis=0),
                v=p["v"][:, sl], g_end=p["g_end"][:, sl],
                bonus=_half_sums(p["rk"][:, sl], lo_c) * p["v"][:, sl]))
        g_all = [_dot_nt(x["ar"], x["bk"], prec) for x in un]
        ab = [jnp.where(mask2, m[:, :2 * c], 0.0) for m in g_all]
        ak = [jnp.where(mask2, m[:, 2 * c:], 0.0) for m in g_all]
        t_inv = _unit_lower_inverse([m[:c] for m in ab], eye, same_block, c, mm)
        s0 = [s_scr[s, q] for s, q in units]
        x1 = [_dot_nt(x["ar"], expand(si), prec) for x, si in zip(un, s0)]
        x1 = [xi + mm(m, x["v"]) for xi, m, x in zip(x1, ak, un)]
        u = [mm(ti, xi[:c]) for ti, xi in zip(t_inv, x1)]
        o = [xi[c:] + mm(m[c:], ui) for xi, m, ui in zip(x1, ab, u)]
        w = [_dot_tn(jnp.concatenate([x["v"], ui], axis=0), x["kb_end"], prec)
             for x, ui in zip(un, u)]
        for (s, q), si, wi, x in zip(units, s0, w, un):
            s_scr[s, q] = si * x["g_end"] + jnp.where(lo_s, wi[:n], wi[n:])
        normed = []
        for oi in o:
            mean = _half_sums(oi, lo_c) * (1.0 / n)
            var = _half_sums(jnp.square(oi - mean), lo_c) * (1.0 / n)
            normed.append((oi - mean) * lax.rsqrt(var + RW_GN_EPS))
        for s in range(bb):
            idx = [i for i, (si, _) in enumerate(units) if si == s]
            o_all = jnp.concatenate([normed[i] for i in idx], axis=-1)
            bonus_all = jnp.concatenate([un[i]["bonus"] for i in idx], axis=-1)
            z_out[s, rows, :] = (o_all * ln_w + ln_b + bonus_all) * g_ref[s, rows, :]
        return carry

    lax.fori_loop(0, nchunk, chunk, 0)

    @pl.when(pl.program_id(2) == pl.num_programs(2) - 1)
    def _():
        for s, q in units:
            s_out[s, 2 * q] = s_scr[s, q][:, :n]
            s_out[s, 2 * q + 1] = s_scr[s, q][:, n:]


def wkv_scan(r, lw, k, v, a, g, hp, s0, *, bb, tl, c, nheads, prec):
    bsz, L, d = r.shape
    width = nheads * RW_HEAD
    seq = pl.BlockSpec((bb, tl, width), lambda b, h, l: (b, l, h))
    st = pl.BlockSpec((bb, nheads, RW_HEAD, RW_HEAD), lambda b, h, l: (b, h, 0, 0))
    paired = 2 * c == LANES and nheads % 2 == 0
    body = _wkv_pair_kernel if paired else _wkv_kernel
    scr = (bb, nheads // 2, RW_HEAD, LANES) if paired else (bb, nheads, RW_HEAD, RW_HEAD)
    return pl.pallas_call(
        functools.partial(body, bb=bb, tl=tl, c=c, prec=prec),
        grid=(bsz // bb, d // width, L // tl),
        in_specs=[seq] * 6 + [pl.BlockSpec((SUBLANES, width), lambda b, h, l: (0, h)), st],
        out_specs=[seq, st],
        out_shape=[jax.ShapeDtypeStruct((bsz, L, d), F32),
                   jax.ShapeDtypeStruct(s0.shape, F32)],
        scratch_shapes=[pltpu.VMEM(scr, F32)],
        compiler_params=_cparams(("parallel", "parallel", "arbitrary")),
        name="wkv_scan",
    )(r, lw, k, v, a, g, hp, s0)


def rwkv_layer(x, shift, wkv, v_first, prm, cfg, norm_w=None):
    r, lw, k, v, a, g, last = rwkv_proj(x, shift, v_first, prm, norm_w,
                                        bb=cfg["proj_bb"], tl=cfg["proj_tl"])
    z, wkv_new = wkv_scan(r, lw, k, v, a, g, prm["hp"], wkv,
                          bb=cfg["scan_bb"], tl=cfg["wkv_tl"], c=cfg["wkv_c"],
                          nheads=cfg["wkv_heads"], prec=SCAN_PREC)
    return ((z, prm["w_o"]), last.reshape(last.shape[0], -1), wkv_new,
            (v if v_first is None else v_first))


def _prep_rwkv(mu, w0, w1, w2, a0, a1, a2, g1, g2, k_k, k_a, r_k, w_rkv, w_o, lnx_w, lnx_b,
               v_lora=None):
    d = D_MODEL
    zero = jnp.zeros((d,), F32)
    prm = dict(mu=mu, w_rkv=w_rkv.astype(BF16), w_o=w_o.astype(BF16),
               w1=w1.astype(BF16), w2=w2.astype(BF16), a1=a1.astype(BF16), a2=a2.astype(BF16),
               g1=g1.astype(BF16), g2=g2.astype(BF16))
    v0 = zero
    if v_lora is not None:
        v0, v1, v2 = v_lora
        prm["v1"] = v1.astype(BF16)
        prm["v2"] = v2.astype(BF16)
    prm["b0"] = jnp.stack([w0, a0, v0] + [zero] * 5)
    prm["hp"] = jnp.stack([k_k, k_a, r_k.reshape(d), lnx_w, lnx_b] + [zero] * 3)
    return prm


S5_CH = 8
S5_CL = S5_LANES // S5_CH


def _s5_kernel(u_ref, w_ref, cm_ref, ar_ref, ai_ref, d_ref, wv_ref, wg_ref, h0r_ref, h0i_ref,
               y_out, hr_out, hi_out, hs, hc, g_s, *, bsz, tr):
    i = pl.program_id(0)
    cl = S5_CL
    nstep = tr // bsz

    @pl.when(i == 0)
    def _():
        hc[:, :S5_LANES] = h0r_ref[...]
        hc[:, S5_LANES:] = h0i_ref[...]

    for ch in range(S5_CH):
        lanes = slice(ch * LANES, (ch + 1) * LANES)
        re = slice(ch * cl, (ch + 1) * cl)
        im = slice(S5_LANES + ch * cl, S5_LANES + (ch + 1) * cl)
        u = u_ref[:, lanes]
        hs[...] = _dot(u, w_ref[ch])
        ar = jnp.broadcast_to(ar_ref[ch], (bsz, cl))
        ai = jnp.broadcast_to(ai_ref[ch], (bsz, cl))

        def step(t, carry, ar=ar, ai=ai):
            hr, hi = carry
            rows = pl.ds(pl.multiple_of(t * bsz, bsz), bsz)
            nr = ar * hr - ai * hi + hs[rows, :cl]
            ni = ar * hi + ai * hr + hs[rows, cl:]
            hs[rows, :cl] = nr
            hs[rows, cl:] = ni
            return nr, ni

        hr, hi = lax.fori_loop(0, nstep, step, (hc[:, re], hc[:, im]),
                               unroll=min(nstep, SUBLANES))
        hc[:, re] = hr
        hc[:, im] = hi
        y = _dot(hs[...], cm_ref[ch]) + d_ref[:, lanes] * u
        g_s[:, lanes] = jax.nn.gelu(y).astype(BF16)

    g = g_s[...]
    y_out[...] = _dot(g, wv_ref[...]) * jax.nn.sigmoid(_dot(g, wg_ref[...]))

    @pl.when(i == pl.num_programs(0) - 1)
    def _():
        hr_out[...] = hc[:, :S5_LANES]
        hi_out[...] = hc[:, S5_LANES:]


def s5_mix(u_tm, h0r, h0i, prm, *, bsz, tr):
    rows, d = u_tm.shape
    cl = S5_CL
    tile = pl.BlockSpec((tr, d), lambda i: (i, 0))
    st = _full((bsz, S5_LANES))
    return pl.pallas_call(
        functools.partial(_s5_kernel, bsz=bsz, tr=tr),
        grid=(rows // tr,),
        in_specs=[tile, _resident((S5_CH, LANES, 2 * cl)), _resident((S5_CH, 2 * cl, LANES)),
                  _resident((S5_CH, 1, cl)), _resident((S5_CH, 1, cl)), _resident((1, d)),
                  _resident((d, d)), _resident((d, d)), st, st],
        out_specs=[tile, st, st],
        out_shape=[jax.ShapeDtypeStruct((rows, d), F32),
                   jax.ShapeDtypeStruct((bsz, S5_LANES), F32),
                   jax.ShapeDtypeStruct((bsz, S5_LANES), F32)],
        scratch_shapes=[pltpu.VMEM((tr, 2 * cl), F32), pltpu.VMEM((bsz, 2 * S5_LANES), F32),
                        pltpu.VMEM((tr, d), BF16)],
        compiler_params=_cparams(("arbitrary",)),
        name="s5_mix",
    )(u_tm, prm["w_in"], prm["c_out"], prm["ab_re"], prm["ab_im"], prm["d"],
      prm["glu_v"], prm["glu_g"], h0r, h0i)


def _prep_s5(a_re, a_im, log_dt, b_re, b_im, c_re, c_im, d_skip, glu_v, glu_g):
    dt = jnp.exp(log_dt.astype(F32))[:, None]
    lr, li = a_re.astype(F32), a_im.astype(F32)
    mag = jnp.exp(lr * dt)
    ab_re, ab_im = mag * jnp.cos(li * dt), mag * jnp.sin(li * dt)
    den = lr * lr + li * li
    q_re = ((ab_re - 1.0) * lr + ab_im * li) / den
    q_im = (ab_im * lr - (ab_re - 1.0) * li) / den
    bb_re = q_re[..., None] * b_re - q_im[..., None] * b_im
    bb_im = q_re[..., None] * b_im + q_im[..., None] * b_re
    gl = S5_GROUPS // S5_CH
    eye = jnp.eye(gl, dtype=F32)

    def in_blocks(bb):
        t = bb.reshape(S5_CH, gl, S5_STATE, S5_GROUP)
        return jnp.einsum('cgph,gk->cghkp', t, eye).reshape(S5_CH, LANES, S5_CL)

    def out_blocks(cc):
        t = cc.reshape(S5_CH, gl, S5_GROUP, S5_STATE)
        return jnp.einsum('cghp,gk->ckpgh', t, eye).reshape(S5_CH, S5_CL, LANES)

    return dict(
        w_in=jnp.concatenate([in_blocks(bb_re), in_blocks(bb_im)], axis=-1).astype(BF16),
        c_out=jnp.concatenate([out_blocks(c_re.astype(F32)), -out_blocks(c_im.astype(F32))],
                              axis=1).astype(BF16),
        ab_re=ab_re.reshape(S5_CH, 1, S5_CL), ab_im=ab_im.reshape(S5_CH, 1, S5_CL),
        d=d_skip.reshape(1, D_MODEL),
        glu_v=glu_v.astype(BF16), glu_g=glu_g.astype(BF16))


def s5_layer(xn, h_re0, h_im0, prm, cfg):
    bsz = h_re0.shape[0]
    if cfg["time_major"]:
        L, _, d = xn.shape
        u_tm = xn.reshape(L * bsz, d)
    else:
        _, L, d = xn.shape
        u_tm = jnp.swapaxes(xn, 0, 1).reshape(L * bsz, d)
    y, hr, hi = s5_mix(u_tm, h_re0.reshape(bsz, S5_LANES), h_im0.reshape(bsz, S5_LANES), prm,
                       bsz=bsz, tr=cfg["s5_tr"])
    y = y.reshape(L, bsz, d)
    if not cfg["time_major"]:
        y = jnp.swapaxes(y, 0, 1)
    return ((y, None), hr.reshape(bsz, S5_GROUPS, S5_STATE), hi.reshape(bsz, S5_GROUPS, S5_STATE))


def _mamba_in_kernel(x_ref, cp_ref, wz_ref, wx_ref, wdt_ref, cw_ref, cb_ref, dtb_ref,
                     z_out, xs_out, b_out, c_out, dt_out, nc_out, buf, *, bb, tl):
    d = x_ref.shape[-1]
    m = bb * tl
    cd = M_CONV_DIM
    l = pl.program_id(1)

    @pl.when(l == 0)
    def _():
        buf[:, :SUBLANES - (M_CONV - 1), :] = jnp.zeros((bb, SUBLANES - (M_CONV - 1), cd), F32)
        buf[:, SUBLANES - (M_CONV - 1):SUBLANES, :] = cp_ref[...]

    @pl.when(l > 0)
    def _():
        buf[:, :SUBLANES, :] = buf[:, tl:, :]

    x = x_ref[...].reshape(m, d).astype(BF16)
    nbc = M_GROUPS * M_STATE
    for ci in range(cd // nbc):
        cs = slice(ci * nbc, (ci + 1) * nbc)
        xbc = _dot(x, wx_ref[:, cs])
        buf[:, SUBLANES:, cs] = xbc.reshape(bb, tl, nbc)
        conv = cb_ref[:, cs] + xbc * cw_ref[M_CONV - 1:M_CONV, cs]
        for j in range(1, M_CONV):
            sh = buf[:, SUBLANES - j:SUBLANES - j + tl, cs]
            conv = conv + sh.reshape(m, nbc) * cw_ref[M_CONV - 1 - j:M_CONV - j, cs]
        act = (conv * jax.nn.sigmoid(conv)).reshape(bb, tl, nbc)
        if ci * nbc < M_INNER:
            xs_out[:, :, cs] = act
            z_out[:, :, cs] = _dot(x, wz_ref[:, cs]).reshape(bb, tl, nbc)
        elif ci * nbc == M_INNER:
            b_out[...] = act
        else:
            c_out[...] = act
    nc_out[...] = buf[:, tl + SUBLANES - (M_CONV - 1):, :]
    dt_raw = _dot(x, wdt_ref[...])[:, :M_HEADS]
    dt_out[...] = _softplus(dt_raw + dtb_ref[...]).reshape(bb, tl, M_HEADS)


def mamba_in(xn, conv_prev, prm, *, bb, tl):
    bsz, L, d = xn.shape
    cd = M_CONV_DIM
    nbc = M_GROUPS * M_STATE
    seq = lambda w: pl.BlockSpec((bb, tl, w), lambda b, l: (b, l, 0))
    cps = pl.BlockSpec((bb, M_CONV - 1, cd), lambda b, l: (b, 0, 0))
    return pl.pallas_call(
        functools.partial(_mamba_in_kernel, bb=bb, tl=tl),
        grid=(bsz // bb, L // tl),
        in_specs=[seq(d), cps, _resident((d, M_INNER)), _resident((d, cd)),
                  _resident((d, LANES)),
                  _full((M_CONV, cd)), _full((1, cd)), _full((1, M_HEADS))],
        out_specs=[seq(M_INNER), seq(M_INNER), seq(nbc), seq(nbc), seq(M_HEADS), cps],
        out_shape=[jax.ShapeDtypeStruct((bsz, L, M_INNER), F32),
                   jax.ShapeDtypeStruct((bsz, L, M_INNER), F32),
                   jax.ShapeDtypeStruct((bsz, L, nbc), F32),
                   jax.ShapeDtypeStruct((bsz, L, nbc), F32),
                   jax.ShapeDtypeStruct((bsz, L, M_HEADS), F32),
                   jax.ShapeDtypeStruct((bsz, M_CONV - 1, cd), F32)],
        scratch_shapes=[pltpu.VMEM((bb, SUBLANES + tl, cd), F32)],
        compiler_params=_cparams(("parallel", "arbitrary")),
        name="mamba_in",
    )(xn, conv_prev, prm["wz"], prm["wx"], prm["wdt"], prm["conv_w"], prm["conv_b"],
      prm["dt_bias"])


def _ssd_kernel(xs_ref, b_ref, c_ref, z_ref, dt_ref, hv_ref, ex_ref, nw_ref, s0_ref,
                y_out, s_out, s_scr, *, bb, tl, c):
    p, n = M_HEADDIM, M_STATE
    hpg = M_HEADS // M_GROUPS
    gw = M_INNER // M_GROUPS
    nchunk = tl // c

    @pl.when(pl.program_id(1) == 0)
    def _():
        s_scr[...] = s0_ref[...]

    row = lax.broadcasted_iota(jnp.int32, (c, c), 0)
    col = lax.broadcasted_iota(jnp.int32, (c, c), 1)
    incl = row >= col
    tril = jnp.where(incl, 1.0, 0.0).astype(F32)
    triu = jnp.where(row <= col, 1.0, 0.0).astype(F32)
    eye = jnp.where(row == col, 1.0, 0.0).astype(F32)
    row_p = lax.broadcasted_iota(jnp.int32, (c, 2 * c), 0)
    lane_p = lax.broadcasted_iota(jnp.int32, (c, 2 * c), 1)
    lo_pair = lane_p < c
    incl_pair = row_p >= (lane_p & (c - 1))
    row_b = lax.broadcasted_iota(jnp.int32, (2 * c, 2 * p), 0)
    lane_b = lax.broadcasted_iota(jnp.int32, (2 * c, 2 * p), 1)
    bd_mask = (row_b >> (c.bit_length() - 1)) == (lane_b >> (p.bit_length() - 1))
    a_neg = hv_ref[0:1, :]
    d_exp = nw_ref[1:2, :]
    norm_w = nw_ref[0:1, :]
    expand = ex_ref[...]

    def chunk(ci, carry):
        rows = pl.ds(pl.multiple_of(ci * c, c), c)
        pre = []
        for s in range(bb):
            xs = xs_ref[s, rows, :]
            dt = dt_ref[s, rows, :]
            a = dt * a_neg
            cum = _dot(tril, a, HIGHEST)
            cum_end = cum[c - 1:c, :]
            pre.append(dict(
                xs=xs, cum=cum,
                cum_t=_dot_tn(a, triu, HIGHEST),
                dt_t=_dot_tn(dt, eye, HIGHEST),
                e_end=jnp.exp(cum_end),
                xend=xs * _dot(dt * jnp.exp(cum_end - cum), expand),
                ecum_x=_dot(jnp.exp(cum), expand)))
        yd = [[] for _ in range(bb)]
        yo = [[] for _ in range(bb)]
        for gi in range(M_GROUPS):
            for s in range(bb):
                q_ = pre[s]
                xs, cum, cum_t, dt_t = q_["xs"], q_["cum"], q_["cum_t"], q_["dt_t"]
                b_g = b_ref[s, rows, gi * n:(gi + 1) * n]
                c_g = c_ref[s, rows, gi * n:(gi + 1) * n]
                cb = _dot_nt(c_g, b_g)
                cb2 = jnp.concatenate([cb, cb], axis=1)
                for q in range(hpg // 2):
                    h0 = gi * hpg + 2 * q
                    col = jnp.where(lo_pair, cum[:, h0:h0 + 1], cum[:, h0 + 1:h0 + 2])
                    rowv = jnp.concatenate([cum_t[h0:h0 + 1, :], cum_t[h0 + 1:h0 + 2, :]], axis=1)
                    dtrow = jnp.concatenate([dt_t[h0:h0 + 1, :], dt_t[h0 + 1:h0 + 2, :]], axis=1)
                    lmat = jnp.where(incl_pair, jnp.exp(jnp.where(incl_pair, col - rowv, 0.0)), 0.0)
                    x_pair = xs[:, h0 * p:(h0 + 2) * p]
                    x_bd = jnp.where(bd_mask, jnp.concatenate([x_pair, x_pair], axis=0), 0.0)
                    yd[s].append(_dot(cb2 * lmat * dtrow, x_bd))
                s_g = s_scr[s, gi * hpg:(gi + 1) * hpg].reshape(hpg * p, n)
                yo[s].append(_dot_nt(c_g, s_g))
                upd = _dot_tn(q_["xend"][:, gi * gw:(gi + 1) * gw], b_g)
                for hh in range(hpg):
                    h = gi * hpg + hh
                    s_scr[s, h] = (s_g[hh * p:(hh + 1) * p] * q_["e_end"][:, h:h + 1]
                                   + upd[hh * p:(hh + 1) * p])
        for s in range(bb):
            q_ = pre[s]
            y = (jnp.concatenate(yd[s], axis=-1) + jnp.concatenate(yo[s], axis=-1) * q_["ecum_x"]
                 + d_exp * q_["xs"])
            zz = z_ref[s, rows, :]
            y = y * (zz * jax.nn.sigmoid(zz))
            outs = []
            for gi in range(M_GROUPS):
                yg = y[:, gi * gw:(gi + 1) * gw]
                outs.append(yg * lax.rsqrt(jnp.mean(yg * yg, -1, keepdims=True) + EPS))
            y_out[s, rows, :] = jnp.concatenate(outs, axis=-1) * norm_w
        return carry

    lax.fori_loop(0, nchunk, chunk, 0)

    @pl.when(pl.program_id(1) == pl.num_programs(1) - 1)
    def _():
        s_out[...] = s_scr[...]


def ssd_scan(xs, bm, cm, z, dt, prm, s0, *, bb, tl, c):
    bsz, L, _ = xs.shape
    nbc = M_GROUPS * M_STATE
    seq = lambda w: pl.BlockSpec((bb, tl, w), lambda b, l: (b, l, 0))
    st = pl.BlockSpec((bb, M_HEADS, M_HEADDIM, M_STATE), lambda b, l: (b, 0, 0, 0))
    return pl.pallas_call(
        functools.partial(_ssd_kernel, bb=bb, tl=tl, c=c),
        grid=(bsz // bb, L // tl),
        in_specs=[seq(M_INNER), seq(nbc), seq(nbc), seq(M_INNER), seq(M_HEADS),
                  _full((SUBLANES, M_HEADS)), _full((M_HEADS, M_INNER)),
                  _full((SUBLANES, M_INNER)), st],
        out_specs=[seq(M_INNER), st],
        out_shape=[jax.ShapeDtypeStruct((bsz, L, M_INNER), F32),
                   jax.ShapeDtypeStruct(s0.shape, F32)],
        scratch_shapes=[pltpu.VMEM((bb, M_HEADS, M_HEADDIM, M_STATE), F32)],
        compiler_params=_cparams(("parallel", "arbitrary")),
        name="ssd_scan",
    )(xs, bm, cm, z, dt, prm["hv"], prm["expand"], prm["nw"], s0)


def _prep_mamba(in_proj, conv_w, conv_b, dt_bias, a_log, d_skip, norm_w, out_proj):
    d = D_MODEL
    wdt = jnp.zeros((d, LANES), F32).at[:, :M_HEADS].set(in_proj[:, M_INNER + M_CONV_DIM:])
    expand = jnp.repeat(jnp.eye(M_HEADS, dtype=F32), M_HEADDIM, axis=1)
    zrow_h = jnp.zeros((M_HEADS,), F32)
    zrow_i = jnp.zeros((M_INNER,), F32)
    return dict(
        wz=in_proj[:, :M_INNER].astype(BF16),
        wx=in_proj[:, M_INNER:M_INNER + M_CONV_DIM].astype(BF16),
        wdt=wdt.astype(BF16),
        conv_w=conv_w, conv_b=conv_b.reshape(1, M_CONV_DIM),
        dt_bias=dt_bias.reshape(1, M_HEADS),
        hv=jnp.stack([-jnp.exp(a_log.astype(F32))] + [zrow_h] * 7),
        expand=expand,
        nw=jnp.stack([norm_w, jnp.repeat(d_skip, M_HEADDIM)] + [zrow_i] * 6),
        out_proj=out_proj.astype(BF16))


def mamba_layer(xn, conv_prev, ssm_prev, prm, cfg):
    bsz, L, d = xn.shape
    z, xs, bm, cm, dt, new_conv = mamba_in(xn, conv_prev, prm, bb=cfg["mamba_bb"],
                                           tl=cfg["mamba_tl"])
    y, ssm_new = ssd_scan(xs, bm, cm, z, dt, prm, ssm_prev,
                          bb=cfg["ssd_bb"], tl=cfg["ssd_tl"], c=cfg["ssd_c"])
    return (y, prm["out_proj"]), new_conv, ssm_new


def _config(bsz, L):
    if L >= 512:
        return dict(tm=512, ffn_fb=bsz, ffn_ft=512 // bsz, time_major=bsz == SUBLANES,
                    proj_bb=1, proj_tl=256,
                    scan_bb=min(bsz, 8), wkv_tl=256, wkv_c=64, wkv_heads=4,
                    mamba_bb=1, mamba_tl=256,
                    ssd_bb=2, ssd_tl=256, ssd_c=M_CHUNK, s5_tr=1024)
    rows = bsz * L
    tm = min(512, rows)
    return dict(tm=tm, ffn_fb=tm // L, ffn_ft=L, time_major=False,
                proj_bb=min(bsz, 256 // L), proj_tl=L,
                scan_bb=min(bsz, 16), wkv_tl=L, wkv_c=L, wkv_heads=2,
                mamba_bb=min(bsz, 256 // L), mamba_tl=L,
                ssd_bb=min(bsz, 4), ssd_tl=L, ssd_c=min(M_CHUNK, L), s5_tr=rows)


def _trunk(x, p, states, layer_prms, shared, cfg):
    bsz, L, d = x.shape
    xn = h = x
    v_first = None
    new_states = []
    for i in range(DEPTH):
        st_a, st_b = states[2 * i], states[2 * i + 1]
        prm = layer_prms[i]
        kind = i % 3
        if kind == 0:
            (mix, wo), n_a, n_b, v_first = rwkv_layer(
                xn, st_a, st_b, v_first, prm, cfg, shared["norm_mix"][0] if i == 0 else None)
        elif kind == 1:
            (mix, wo), n_a, n_b = s5_layer(xn, st_a, st_b, prm, cfg)
        else:
            (mix, wo), n_a, n_b = mamba_layer(xn, st_a, st_b, prm, cfg)
        new_states += [n_a, n_b]
        nxt = shared["norm_mix"][i + 1] if i + 1 < DEPTH else shared["final_norm"]
        mix_tm = cfg["time_major"] and kind == 1
        xn_tm = cfg["time_major"] and (i + 1) % 3 == 1 and i + 1 < DEPTH
        h, xn = ffn_ple(h, mix, wo, p, i, shared, nxt, fb=cfg["ffn_fb"], ft=cfg["ffn_ft"],
                        mix_tm=mix_tm, xn_tm=xn_tm)
    return xn, new_states


def kernel(x_prompt, x_sample, p_prompt, p_sample, state_l0_shift, state_l0_wkv, state_l1_s5_re, state_l1_s5_im, state_l2_conv, state_l2_ssm, state_l3_shift, state_l3_wkv, l0_mu, l0_w0, l0_w1, l0_w2, l0_a0, l0_a1, l0_a2, l0_g1, l0_g2, l0_k_k, l0_k_a, l0_r_k, l0_w_rkv, l0_w_o, l0_lnx_w, l0_lnx_b, l1_a_re, l1_a_im, l1_log_dt, l1_b_re, l1_b_im, l1_c_re, l1_c_im, l1_d, l1_glu_v, l1_glu_g, l2_in_proj, l2_conv_w, l2_conv_b, l2_dt_bias, l2_a_log, l2_d, l2_norm_w, l2_out_proj, l3_mu, l3_w0, l3_w1, l3_w2, l3_a0, l3_a1, l3_a2, l3_g1, l3_g2, l3_k_k, l3_k_a, l3_r_k, l3_w_rkv, l3_w_o, l3_lnx_w, l3_lnx_b, l3_v0, l3_v1, l3_v2, norm_mix, norm_ffn, norm_ple, ffn_w1, ffn_w3, ffn_w2, ple_proj, ple_gate, final_norm):
    layer_prms = (
        _prep_rwkv(l0_mu, l0_w0, l0_w1, l0_w2, l0_a0, l0_a1, l0_a2, l0_g1, l0_g2,
                   l0_k_k, l0_k_a, l0_r_k, l0_w_rkv, l0_w_o, l0_lnx_w, l0_lnx_b),
        _prep_s5(l1_a_re, l1_a_im, l1_log_dt, l1_b_re, l1_b_im, l1_c_re, l1_c_im,
                 l1_d, l1_glu_v, l1_glu_g),
        _prep_mamba(l2_in_proj, l2_conv_w, l2_conv_b, l2_dt_bias, l2_a_log, l2_d,
                    l2_norm_w, l2_out_proj),
        _prep_rwkv(l3_mu, l3_w0, l3_w1, l3_w2, l3_a0, l3_a1, l3_a2, l3_g1, l3_g2,
                   l3_k_k, l3_k_a, l3_r_k, l3_w_rkv, l3_w_o, l3_lnx_w, l3_lnx_b,
                   (l3_v0, l3_v1, l3_v2)),
    )
    shared = dict(norm_mix=norm_mix, norm_ffn=norm_ffn.reshape(DEPTH, 1, D_MODEL),
                  norm_ple=norm_ple.reshape(DEPTH, 1, D_MODEL), final_norm=final_norm,
                  ffn_w1=ffn_w1.astype(BF16), ffn_w3=ffn_w3.astype(BF16),
                  ffn_w2=ffn_w2.astype(BF16), ple_gate=ple_gate.astype(BF16),
                  ple_proj=ple_proj.astype(BF16))
    sample_states = [state_l0_shift, state_l0_wkv, state_l1_s5_re, state_l1_s5_im,
                     state_l2_conv, state_l2_ssm, state_l3_shift, state_l3_wkv]
    bp = x_prompt.shape[0]
    prompt_states = [jnp.zeros((bp,) + s.shape[1:], F32) for s in sample_states]
    y_prompt, new_p = _trunk(x_prompt, p_prompt, prompt_states, layer_prms, shared,
                             _config(*x_prompt.shape[:2]))
    y_sample, new_s = _trunk(x_sample, p_sample, sample_states, layer_prms, shared,
                             _config(*x_sample.shape[:2]))
    return (y_prompt, y_sample, *new_p, *new_s)
```

```python
import functools
import math

import jax
import jax.numpy as jnp
from jax import lax
from jax.experimental import pallas as pl
from jax.experimental.pallas import tpu as pltpu

F32 = jnp.float32
BF16 = jnp.bfloat16
HIGHEST = lax.Precision.HIGHEST

D_MODEL = 1024
DEPTH = 4
PLE_DIM = 256
EPS = 1e-6
D_FF = 2816
RW_HEAD = 64
RW_HEADS = D_MODEL // RW_HEAD
RW_GN_EPS = 64e-5
S5_GROUP = 16
S5_GROUPS = D_MODEL // S5_GROUP
S5_STATE = 64
S5_LANES = S5_GROUPS * S5_STATE
M_INNER = 2 * D_MODEL
M_HEADDIM = 64
M_HEADS = M_INNER // M_HEADDIM
M_STATE = 128
M_GROUPS = 4
M_CONV = 4
M_CHUNK = 64
M_CONV_DIM = M_INNER + 2 * M_GROUPS * M_STATE

LANES = 128
MXU_TILE = 256
SUBLANES = 8
VMEM_LIMIT = 56 * 1024 * 1024

SCAN_PREC = None


def _cparams(sem):
    return pltpu.CompilerParams(dimension_semantics=sem, vmem_limit_bytes=VMEM_LIMIT)


def _operands(a, b, prec):
    if prec is None:
        return a.astype(BF16), b.astype(BF16)
    return a, b


def _dot(a, b, prec=None):
    a, b = _operands(a, b, prec)
    return jnp.dot(a, b, preferred_element_type=F32, precision=prec)


def _dot_nt(a, b, prec=None):
    a, b = _operands(a, b, prec)
    return lax.dot_general(a, b, (((1,), (1,)), ((), ())),
                           preferred_element_type=F32, precision=prec)


def _dot_tn(a, b, prec=None):
    a, b = _operands(a, b, prec)
    return lax.dot_general(a, b, (((0,), (0,)), ((), ())),
                           preferred_element_type=F32, precision=prec)


def _rms(x, g):
    return x * lax.rsqrt(jnp.mean(x * x, axis=-1, keepdims=True) + EPS) * g


def _log_sigmoid(z):
    return -(jnp.maximum(-z, 0.0) + jnp.log1p(jnp.exp(-jnp.abs(z))))


def _softplus(z):
    return jnp.maximum(z, 0.0) + jnp.log1p(jnp.exp(-jnp.abs(z)))


def _full(shape):
    n = len(shape)
    return pl.BlockSpec(shape, lambda *_: (0,) * n)


def _resident(shape):
    n = len(shape)
    return pl.BlockSpec(shape, lambda *_: (0,) * n, pipeline_mode=pl.Buffered(1))


def _ffn_kernel(*refs, project, mix_tm, xn_tm):
    if project:
        (h_ref, mix_ref, wo_ref, p_ref, nf_ref, np_ref, nn_ref, w1_ref, w3_ref, w2_ref,
         pg_ref, pp_ref, h_out, xn_out) = refs
    else:
        (h_ref, mix_ref, p_ref, nf_ref, np_ref, nn_ref, w1_ref, w3_ref, w2_ref,
         pg_ref, pp_ref, h_out, xn_out) = refs
    fb, ft, d = h_ref.shape
    tm = fb * ft
    mix = mix_ref[...]
    if mix_tm:
        mix = jnp.swapaxes(mix, 0, 1)
    mix = mix.reshape(tm, mix.shape[-1])
    y = _dot(mix, wo_ref[...]) if project else mix
    hin = h_ref[...].reshape(tm, d) + y
    hn = _rms(hin, nf_ref[...]).astype(BF16)
    acc = None
    for ci in range(D_FF // MXU_TILE):
        cs = slice(ci * MXU_TILE, (ci + 1) * MXU_TILE)
        a = _dot(hn, w1_ref[:, cs])
        b = _dot(hn, w3_ref[:, cs])
        t = (a * jax.nn.sigmoid(a) * b).astype(BF16)
        part = _dot(t, w2_ref[cs, :])
        acc = part if acc is None else acc + part
    h2 = hin + acc
    gate = jax.nn.sigmoid(_dot(_rms(h2, np_ref[...]).astype(BF16), pg_ref[...]))
    h3 = h2 + gate * _dot(p_ref[...].reshape(tm, PLE_DIM).astype(BF16), pp_ref[...])
    h_out[...] = h3.reshape(fb, ft, d)
    xn = _rms(h3, nn_ref[...]).reshape(fb, ft, d)
    xn_out[...] = jnp.swapaxes(xn, 0, 1) if xn_tm else xn


def ffn_ple(h, mix, wo, p_all, layer, shared, nn, *, fb, ft, mix_tm=False, xn_tm=False):
    bsz, L, d = h.shape
    km = mix.shape[-1]
    nlt = L // ft
    seq = lambda w: pl.BlockSpec((fb, ft, w), lambda i: (i // nlt, i % nlt, 0))
    tmaj = lambda w: pl.BlockSpec((ft, fb, w), lambda i: (i % nlt, 0, 0))
    lay = lambda *shape: pl.BlockSpec((None,) + shape, lambda i: (layer,) + (0,) * len(shape),
                                      pipeline_mode=pl.Buffered(1))
    ins = [h, mix]
    specs = [seq(d), tmaj(km) if mix_tm else seq(km)]
    if wo is not None:
        ins.append(wo)
        specs.append(_resident(wo.shape))
    ins += [p_all, shared["norm_ffn"], shared["norm_ple"], nn.reshape(1, d),
            shared["ffn_w1"], shared["ffn_w3"], shared["ffn_w2"],
            shared["ple_gate"], shared["ple_proj"]]
    specs += [pl.BlockSpec((None, fb, ft, PLE_DIM), lambda i: (layer, i // nlt, i % nlt, 0)),
              lay(1, d), lay(1, d), _full((1, d)),
              lay(d, D_FF), lay(d, D_FF), lay(D_FF, d), lay(d, d), lay(PLE_DIM, d)]
    xn_shape = (L, bsz, d) if xn_tm else (bsz, L, d)
    return pl.pallas_call(
        functools.partial(_ffn_kernel, project=wo is not None, mix_tm=mix_tm, xn_tm=xn_tm),
        grid=(bsz // fb * nlt,),
        in_specs=specs,
        out_specs=[seq(d), tmaj(d) if xn_tm else seq(d)],
        out_shape=[jax.ShapeDtypeStruct((bsz, L, d), F32), jax.ShapeDtypeStruct(xn_shape, F32)],
        compiler_params=_cparams(("parallel",)),
        name="ffn_ple",
    )(*ins)


def _rwkv_proj_kernel(*refs, bb, tl, with_v, norm_in):
    it = iter(refs)
    x_ref, sh_ref = next(it), next(it)
    vf_ref = next(it) if with_v else None
    nw_ref = next(it) if norm_in else None
    mu_ref, b0_ref, wrkv_ref, w1_ref, w2_ref, a1_ref, a2_ref, g1_ref, g2_ref = (
        next(it) for _ in range(9))
    if with_v:
        v1_ref, v2_ref = next(it), next(it)
    r_out, lw_out, k_out, v_out, a_out, g_out, last_out, carry = it
    d = x_ref.shape[-1]
    m = bb * tl

    @pl.when(pl.program_id(1) == 0)
    def _():
        carry[...] = sh_ref[...]

    x = x_ref[...].reshape(m, d)
    if norm_in:
        x = _rms(x, nw_ref[...])
    prev = jnp.broadcast_to(carry[...], (bb, tl, d)).reshape(m, d)
    rolled = pltpu.roll(x, 1, 0)
    t_in_tile = lax.broadcasted_iota(jnp.int32, (m, d), 0) & (tl - 1)
    x_prev = jnp.where(t_in_tile == 0, prev, rolled)
    last = x.reshape(bb, tl, d)[:, tl - 1:tl, :]
    carry[...] = last
    last_out[...] = last

    xx = x_prev - x

    def mix(j):
        return (x + xx * mu_ref[j:j + 1, :]).astype(BF16)

    shp = (bb, tl, d)
    xw, xa, xg, xv = mix(1), mix(4), mix(5), mix(3)
    zw = b0_ref[0:1, :] + _dot(jnp.tanh(_dot(xw, w1_ref[...])).astype(BF16), w2_ref[...])
    lw_out[...] = (-jnp.exp(_log_sigmoid(zw) - 0.5)).reshape(shp)
    a = jax.nn.sigmoid(b0_ref[1:2, :] + _dot(_dot(xa, a1_ref[...]).astype(BF16), a2_ref[...]))
    a_out[...] = a.reshape(shp)
    g_out[...] = _dot(jax.nn.sigmoid(_dot(xg, g1_ref[...])).astype(BF16),
                      g2_ref[...]).reshape(shp)
    if with_v:
        lam = jax.nn.sigmoid(b0_ref[2:3, :]
                             + _dot(_dot(xv, v1_ref[...]).astype(BF16), v2_ref[...]))
    v = _dot(xv, wrkv_ref[2])
    if with_v:
        v = v + (vf_ref[...].reshape(m, d) - v) * lam
    v_out[...] = v.reshape(shp)
    r_out[...] = _dot(mix(0), wrkv_ref[0]).reshape(shp)
    k_out[...] = _dot(mix(2), wrkv_ref[1]).reshape(shp)


def rwkv_proj(x, shift, v_first, prm, norm_w, *, bb, tl):
    bsz, L, d = x.shape
    with_v = v_first is not None
    norm_in = norm_w is not None
    seq = pl.BlockSpec((bb, tl, d), lambda b, l: (b, l, 0))
    row = pl.BlockSpec((bb, 1, d), lambda b, l: (b, 0, 0))
    ins = [x, shift.reshape(bsz, 1, d)]
    specs = [seq, row]
    if with_v:
        ins.append(v_first)
        specs.append(seq)
    if norm_in:
        ins.append(norm_w.reshape(1, d))
        specs.append(_full((1, d)))
    names = ["mu", "b0", "w_rkv", "w1", "w2", "a1", "a2", "g1", "g2"]
    if with_v:
        names += ["v1", "v2"]
    for nme in names:
        ins.append(prm[nme])
        specs.append(_resident(prm[nme].shape))
    return pl.pallas_call(
        functools.partial(_rwkv_proj_kernel, bb=bb, tl=tl, with_v=with_v, norm_in=norm_in),
        grid=(bsz // bb, L // tl),
        in_specs=specs,
        out_specs=[seq] * 6 + [row],
        out_shape=[jax.ShapeDtypeStruct((bsz, L, d), F32)] * 6
        + [jax.ShapeDtypeStruct((bsz, 1, d), F32)],
        scratch_shapes=[pltpu.VMEM((bb, 1, d), F32)],
        compiler_params=_cparams(("parallel", "arbitrary")),
        name="rwkv_proj",
    )(*ins)


def _unit_lower_inverse(n_mats, eye, same_block, size, mm):
    blk8 = same_block(3)
    n8 = [jnp.where(blk8, m, 0.0) for m in n_mats]
    n8_2 = [mm(m, m) for m in n8]
    t = [eye + m for m in n8]
    t = [ti + mm(ti, m2) for ti, m2 in zip(t, n8_2)]
    n8_4 = [mm(m2, m2) for m2 in n8_2]
    t = [ti + mm(ti, m4) for ti, m4 in zip(t, n8_4)]
    shift = 3
    while (1 << shift) < size:
        inner, outer = same_block(shift), same_block(shift + 1)
        off = [jnp.where(inner, 0.0, jnp.where(outer, m, 0.0)) for m in n_mats]
        left = [mm(ti, oi) for ti, oi in zip(t, off)]
        t = [ti + mm(li, ti) for ti, li in zip(t, left)]
        shift += 1
    return t


def _wkv_kernel(r_ref, lw_ref, k_ref, v_ref, a_ref, g_ref, hp_ref, s0_ref,
                z_out, s_out, s_scr, *, bb, tl, c, prec):
    n = RW_HEAD
    nheads = s_scr.shape[1]
    nchunk = tl // c

    @pl.when(pl.program_id(2) == 0)
    def _():
        s_scr[...] = s0_ref[...]

    row = lax.broadcasted_iota(jnp.int32, (c, c), 0)
    col = lax.broadcasted_iota(jnp.int32, (c, c), 1)
    tril = jnp.where(row >= col, 1.0, 0.0).astype(F32)
    eye = jnp.where(row == col, 1.0, 0.0).astype(F32)

    def same_block(shift):
        return (row >> shift) == (col >> shift)

    row2 = lax.broadcasted_iota(jnp.int32, (2 * c, c), 0)
    col2 = lax.broadcasted_iota(jnp.int32, (2 * c, c), 1)
    mask2 = col2 <= jnp.where(row2 < c, row2 - 1, row2 - c)
    k_k = hp_ref[0:1, :]
    k_a = hp_ref[1:2, :]
    r_k = hp_ref[2:3, :]
    ln_w = hp_ref[3:4, :]
    ln_b = hp_ref[4:5, :]

    chains = [(s, h) for s in range(bb) for h in range(nheads)]

    def chunk(ci, carry):
        rows = pl.ds(pl.multiple_of(ci * c, c), c)
        lw = [lw_ref[s, rows, :] for s in range(bb)]
        cum = [_dot(tril, x, HIGHEST) for x in lw]
        seq = []
        for s in range(bb):
            r = r_ref[s, rows, :]
            k = k_ref[s, rows, :]
            a = a_ref[s, rows, :]
            cum_end = cum[s][c - 1:c, :]
            e_neg = jnp.exp(-cum[s])
            e_end = jnp.exp(cum_end - cum[s])
            k_mod = k * (1.0 + (a - 1.0) * k_a)
            seq.append(dict(
                a=a, v=v_ref[s, rows, :], kk_raw=k * k_k,
                r_t=r * jnp.exp(cum[s]), e_neg=e_neg, e_end=e_end,
                e_prev=jnp.exp(cum[s] - lw[s]), g_end=jnp.exp(cum_end),
                k_t=k_mod * e_neg, k_end=k_mod * e_end, rk=r * k_mod * r_k))
        ch = []
        for s, h in chains:
            q = seq[s]
            sl = slice(h * n, (h + 1) * n)
            kk = q["kk_raw"][:, sl]
            kk = kk * lax.rsqrt(jnp.maximum(jnp.sum(kk * kk, -1, keepdims=True), 1e-24))
            b_h = kk * q["a"][:, sl]
            a_t = -kk * q["e_prev"][:, sl]
            ch.append(dict(
                ar=jnp.concatenate([a_t, q["r_t"][:, sl]], axis=0),
                b_t=b_h * q["e_neg"][:, sl], k_t=q["k_t"][:, sl],
                kb_end=jnp.concatenate([q["k_end"][:, sl], b_h * q["e_end"][:, sl]], axis=0),
                v=q["v"][:, sl], g_end=q["g_end"][:, sl],
                bonus=jnp.sum(q["rk"][:, sl], -1, keepdims=True) * q["v"][:, sl]))
        ab = [jnp.where(mask2, _dot_nt(x["ar"], x["b_t"], prec), 0.0) for x in ch]
        ak = [jnp.where(mask2, _dot_nt(x["ar"], x["k_t"], prec), 0.0) for x in ch]
        t_inv = _unit_lower_inverse([m[:c] for m in ab], eye, same_block, c,
                                    lambda x, y: _dot(x, y, prec))
        s0 = [s_scr[s, h] for s, h in chains]
        x1 = [_dot_nt(x["ar"], si, prec) for x, si in zip(ch, s0)]
        x1 = [xi + _dot(m, x["v"], prec) for xi, m, x in zip(x1, ak, ch)]
        u = [_dot(ti, xi[:c], prec) for ti, xi in zip(t_inv, x1)]
        o = [xi[c:] + _dot(m[c:], ui, prec) for xi, m, ui in zip(x1, ab, u)]
        s_new = [si * x["g_end"] + _dot_tn(jnp.concatenate([x["v"], ui], axis=0), x["kb_end"], prec)
                 for x, si, ui in zip(ch, s0, u)]
        for (s, h), sn in zip(chains, s_new):
            s_scr[s, h] = sn
        normed = []
        for oi in o:
            mean = jnp.mean(oi, -1, keepdims=True)
            var = jnp.mean(jnp.square(oi - mean), -1, keepdims=True)
            normed.append((oi - mean) * lax.rsqrt(var + RW_GN_EPS))
        for s in range(bb):
            idx = [i for i, (si, _) in enumerate(chains) if si == s]
            o_all = jnp.concatenate([normed[i] for i in idx], axis=-1)
            bonus_all = jnp.concatenate([ch[i]["bonus"] for i in idx], axis=-1)
            z_out[s, rows, :] = (o_all * ln_w + ln_b + bonus_all) * g_ref[s, rows, :]
        return carry

    lax.fori_loop(0, nchunk, chunk, 0)

    @pl.when(pl.program_id(2) == pl.num_programs(2) - 1)
    def _():
        s_out[...] = s_scr[...]


def _half_sums(x, lo):
    s_lo = jnp.sum(jnp.where(lo, x, 0.0), -1, keepdims=True)
    s_hi = jnp.sum(jnp.where(lo, 0.0, x), -1, keepdims=True)
    return jnp.where(lo, s_lo, s_hi)


def _wkv_pair_kernel(r_ref, lw_ref, k_ref, v_ref, a_ref, g_ref, hp_ref, s0_ref,
                     z_out, s_out, s_scr, *, bb, tl, c, prec):
    n = RW_HEAD
    npair = r_ref.shape[-1] // LANES
    nchunk = tl // c
    units = [(s, q) for s in range(bb) for q in range(npair)]

    @pl.when(pl.program_id(2) == 0)
    def _():
        for s, q in units:
            s_scr[s, q] = jnp.concatenate([s0_ref[s, 2 * q], s0_ref[s, 2 * q + 1]], axis=-1)

    def iota2(shape):
        return (lax.broadcasted_iota(jnp.int32, shape, 0),
                lax.broadcasted_iota(jnp.int32, shape, 1))

    row, lane = iota2((c, LANES))
    col = lane & (n - 1)
    lo_c = lane < n
    eye = jnp.where(row == col, 1.0, 0.0).astype(F32)
    row2, lane2 = iota2((2 * c, LANES))
    mask2 = (lane2 & (n - 1)) <= jnp.where(row2 < c, row2 - 1, row2 - c)
    rowt, colt = iota2((c, c))
    tril = jnp.where(rowt >= colt, 1.0, 0.0).astype(F32)
    lo_s = iota2((n, LANES))[1] < n

    def same_block(shift):
        return (row >> shift) == (col >> shift)

    def expand(y):
        rows = y.shape[0]
        r_e, l_e = iota2((2 * rows, LANES))
        same_head = (r_e >> (rows.bit_length() - 1)) == (l_e >> (n.bit_length() - 1))
        return jnp.where(same_head, jnp.concatenate([y, y], axis=0), 0.0)

    def mm(x, y):
        return _dot(x, expand(y), prec)

    k_k = hp_ref[0:1, :]
    k_a = hp_ref[1:2, :]
    r_k = hp_ref[2:3, :]
    ln_w = hp_ref[3:4, :]
    ln_b = hp_ref[4:5, :]

    def chunk(ci, carry):
        rows = pl.ds(pl.multiple_of(ci * c, c), c)
        lw = [lw_ref[s, rows, :] for s in range(bb)]
        cum = [_dot(tril, x, HIGHEST) for x in lw]
        seq = []
        for s in range(bb):
            r = r_ref[s, rows, :]
            k = k_ref[s, rows, :]
            a = a_ref[s, rows, :]
            cum_end = cum[s][c - 1:c, :]
            e_neg = jnp.exp(-cum[s])
            e_end = jnp.exp(cum_end - cum[s])
            k_mod = k * (1.0 + (a - 1.0) * k_a)
            seq.append(dict(
                a=a, v=v_ref[s, rows, :], kk_raw=k * k_k,
                r_t=r * jnp.exp(cum[s]), e_neg=e_neg, e_end=e_end,
                e_prev=jnp.exp(cum[s] - lw[s]), g_end=jnp.exp(cum_end),
                k_t=k_mod * e_neg, k_end=k_mod * e_end, rk=r * k_mod * r_k))
        un = []
        for s, q in units:
            p = seq[s]
            sl = slice(q * LANES, (q + 1) * LANES)
            kk = p["kk_raw"][:, sl]
            kk = kk * lax.rsqrt(jnp.maximum(_half_sums(kk * kk, lo_c), 1e-24))
            b_h = kk * p["a"][:, sl]
            a_t = -kk * p["e_prev"][:, sl]
            un.append(dict(
                ar=jnp.concatenate([a_t, p["r_t"][:, sl]], axis=0),
                bk=jnp.concatenate([expand(b_h * p["e_neg"][:, sl]), expand(p["k_t"][:, sl])],
                                   axis=0),
                kb_end=jnp.concatenate([p["k_end"][:, sl], b_h * p["e_end"][:, sl]], axis=0),
                v=p["v"][:, sl], g_end=p["g_end"][:, sl],
                bonus=_half_sums(p["rk"][:, sl], lo_c) * p["v"][:, sl]))
        g_all = [_dot_nt(x["ar"], x["bk"], prec) for x in un]
        ab = [jnp.where(mask2, m[:, :2 * c], 0.0) for m in g_all]
        ak = [jnp.where(mask2, m[:, 2 * c:], 0.0) for m in g_all]
        t_inv = _unit_lower_inverse([m[:c] for m in ab], eye, same_block, c, mm)
        s0 = [s_scr[s, q] for s, q in units]
        x1 = [_dot_nt(x["ar"], expand(si), prec) for x, si in zip(un, s0)]
        x1 = [xi + mm(m, x["v"]) for xi, m, x in zip(x1, ak, un)]
        u = [mm(ti, xi[:c]) for ti, xi in zip(t_inv, x1)]
        o = [xi[c:] + mm(m[c:], ui) for xi, m, ui in zip(x1, ab, u)]
        w = [_dot_tn(jnp.concatenate([x["v"], ui], axis=0), x["kb_end"], prec)
             for x, ui in zip(un, u)]
        for (s, q), si, wi, x in zip(units, s0, w, un):
            s_scr[s, q] = si * x["g_end"] + jnp.where(lo_s, wi[:n], wi[n:])
        normed = []
        for oi in o:
            mean = _half_sums(oi, lo_c) * (1.0 / n)
            var = _half_sums(jnp.square(oi - mean), lo_c) * (1.0 / n)
            normed.append((oi - mean) * lax.rsqrt(var + RW_GN_EPS))
        for s in range(bb):
            idx = [i for i, (si, _) in enumerate(units) if si == s]
            o_all = jnp.concatenate([normed[i] for i in idx], axis=-1)
            bonus_all = jnp.concatenate([un[i]["bonus"] for i in idx], axis=-1)
            z_out[s, rows, :] = (o_all * ln_w + ln_b + bonus_all) * g_ref[s, rows, :]
        return carry

    lax.fori_loop(0, nchunk, chunk, 0)

    @pl.when(pl.program_id(2) == pl.num_programs(2) - 1)
    def _():
        for s, q in units:
            s_out[s, 2 * q] = s_scr[s, q][:, :n]
            s_out[s, 2 * q + 1] = s_scr[s, q][:, n:]


def wkv_scan(r, lw, k, v, a, g, hp, s0, *, bb, tl, c, nheads, prec):
    bsz, L, d = r.shape
    width = nheads * RW_HEAD
    seq = pl.BlockSpec((bb, tl, width), lambda b, h, l: (b, l, h))
    st = pl.BlockSpec((bb, nheads, RW_HEAD, RW_HEAD), lambda b, h, l: (b, h, 0, 0))
    paired = 2 * c == LANES and nheads % 2 == 0
    body = _wkv_pair_kernel if paired else _wkv_kernel
    scr = (bb, nheads // 2, RW_HEAD, LANES) if paired else (bb, nheads, RW_HEAD, RW_HEAD)
    return pl.pallas_call(
        functools.partial(body, bb=bb, tl=tl, c=c, prec=prec),
        grid=(bsz // bb, d // width, L // tl),
        in_specs=[seq] * 6 + [pl.BlockSpec((SUBLANES, width), lambda b, h, l: (0, h)), st],
        out_specs=[seq, st],
        out_shape=[jax.ShapeDtypeStruct((bsz, L, d), F32),
                   jax.ShapeDtypeStruct(s0.shape, F32)],
        scratch_shapes=[pltpu.VMEM(scr, F32)],
        compiler_params=_cparams(("parallel", "parallel", "arbitrary")),
        name="wkv_scan",
    )(r, lw, k, v, a, g, hp, s0)


def rwkv_layer(x, shift, wkv, v_first, prm, cfg, norm_w=None):
    r, lw, k, v, a, g, last = rwkv_proj(x, shift, v_first, prm, norm_w,
                                        bb=cfg["proj_bb"], tl=cfg["proj_tl"])
    z, wkv_new = wkv_scan(r, lw, k, v, a, g, prm["hp"], wkv,
                          bb=cfg["scan_bb"], tl=cfg["wkv_tl"], c=cfg["wkv_c"],
                          nheads=cfg["wkv_heads"], prec=SCAN_PREC)
    return ((z, prm["w_o"]), last.reshape(last.shape[0], -1), wkv_new,
            (v if v_first is None else v_first))


def _prep_rwkv(mu, w0, w1, w2, a0, a1, a2, g1, g2, k_k, k_a, r_k, w_rkv, w_o, lnx_w, lnx_b,
               v_lora=None):
    d = D_MODEL
    zero = jnp.zeros((d,), F32)
    prm = dict(mu=mu, w_rkv=w_rkv.astype(BF16), w_o=w_o.astype(BF16),
               w1=w1.astype(BF16), w2=w2.astype(BF16), a1=a1.astype(BF16), a2=a2.astype(BF16),
               g1=g1.astype(BF16), g2=g2.astype(BF16))
    v0 = zero
    if v_lora is not None:
        v0, v1, v2 = v_lora
        prm["v1"] = v1.astype(BF16)
        prm["v2"] = v2.astype(BF16)
    prm["b0"] = jnp.stack([w0, a0, v0] + [zero] * 5)
    prm["hp"] = jnp.stack([k_k, k_a, r_k.reshape(d), lnx_w, lnx_b] + [zero] * 3)
    return prm


S5_CH = 8
S5_CL = S5_LANES // S5_CH


def _s5_kernel(u_ref, w_ref, cm_ref, ar_ref, ai_ref, d_ref, wv_ref, wg_ref, h0r_ref, h0i_ref,
               y_out, hr_out, hi_out, hs, hc, g_s, *, bsz, tr):
    i = pl.program_id(0)
    cl = S5_CL
    nstep = tr // bsz

    @pl.when(i == 0)
    def _():
        hc[:, :S5_LANES] = h0r_ref[...]
        hc[:, S5_LANES:] = h0i_ref[...]

    nco = hs.shape[0]
    for c0 in range(0, S5_CH, nco):
        chs = range(c0, c0 + nco)
        lanes = [slice(ch * LANES, (ch + 1) * LANES) for ch in chs]
        re = [slice(ch * cl, (ch + 1) * cl) for ch in chs]
        im = [slice(S5_LANES + ch * cl, S5_LANES + (ch + 1) * cl) for ch in chs]
        for k, ch in enumerate(chs):
            hs[k] = _dot(u_ref[:, lanes[k]], w_ref[ch])
        ar = [jnp.broadcast_to(ar_ref[ch], (bsz, cl)) for ch in chs]
        ai = [jnp.broadcast_to(ai_ref[ch], (bsz, cl)) for ch in chs]

        def step(t, carry, ar=ar, ai=ai):
            rows = pl.ds(pl.multiple_of(t * bsz, bsz), bsz)
            out = []
            for k in range(nco):
                hr, hi = carry[k]
                nr = ar[k] * hr - ai[k] * hi + hs[k, rows, :cl]
                ni = ar[k] * hi + ai[k] * hr + hs[k, rows, cl:]
                hs[k, rows, :cl] = nr
                hs[k, rows, cl:] = ni
                out.append((nr, ni))
            return tuple(out)

        fin = lax.fori_loop(0, nstep, step,
                            tuple((hc[:, re[k]], hc[:, im[k]]) for k in range(nco)),
                            unroll=min(nstep, SUBLANES))
        for k, ch in enumerate(chs):
            hc[:, re[k]] = fin[k][0]
            hc[:, im[k]] = fin[k][1]
            y = _dot(hs[k], cm_ref[ch]) + d_ref[:, lanes[k]] * u_ref[:, lanes[k]]
            g_s[:, lanes[k]] = jax.nn.gelu(y).astype(BF16)

    g = g_s[...]
    y_out[...] = _dot(g, wv_ref[...]) * jax.nn.sigmoid(_dot(g, wg_ref[...]))

    @pl.when(i == pl.num_programs(0) - 1)
    def _():
        hr_out[...] = hc[:, :S5_LANES]
        hi_out[...] = hc[:, S5_LANES:]


def s5_mix(u_tm, h0r, h0i, prm, *, bsz, tr, nco):
    rows, d = u_tm.shape
    cl = S5_CL
    tile = pl.BlockSpec((tr, d), lambda i: (i, 0))
    st = _full((bsz, S5_LANES))
    return pl.pallas_call(
        functools.partial(_s5_kernel, bsz=bsz, tr=tr),
        grid=(rows // tr,),
        in_specs=[tile, _resident((S5_CH, LANES, 2 * cl)), _resident((S5_CH, 2 * cl, LANES)),
                  _resident((S5_CH, 1, cl)), _resident((S5_CH, 1, cl)), _resident((1, d)),
                  _resident((d, d)), _resident((d, d)), st, st],
        out_specs=[tile, st, st],
        out_shape=[jax.ShapeDtypeStruct((rows, d), F32),
                   jax.ShapeDtypeStruct((bsz, S5_LANES), F32),
                   jax.ShapeDtypeStruct((bsz, S5_LANES), F32)],
        scratch_shapes=[pltpu.VMEM((nco, tr, 2 * cl), F32), pltpu.VMEM((bsz, 2 * S5_LANES), F32),
                        pltpu.VMEM((tr, d), BF16)],
        compiler_params=_cparams(("arbitrary",)),
        name="s5_mix",
    )(u_tm, prm["w_in"], prm["c_out"], prm["ab_re"], prm["ab_im"], prm["d"],
      prm["glu_v"], prm["glu_g"], h0r, h0i)


def _prep_s5(a_re, a_im, log_dt, b_re, b_im, c_re, c_im, d_skip, glu_v, glu_g):
    dt = jnp.exp(log_dt.astype(F32))[:, None]
    lr, li = a_re.astype(F32), a_im.astype(F32)
    mag = jnp.exp(lr * dt)
    ab_re, ab_im = mag * jnp.cos(li * dt), mag * jnp.sin(li * dt)
    den = lr * lr + li * li
    q_re = ((ab_re - 1.0) * lr + ab_im * li) / den
    q_im = (ab_im * lr - (ab_re - 1.0) * li) / den
    bb_re = q_re[..., None] * b_re - q_im[..., None] * b_im
    bb_im = q_re[..., None] * b_im + q_im[..., None] * b_re
    gl = S5_GROUPS // S5_CH
    eye = jnp.eye(gl, dtype=F32)

    def in_blocks(bb):
        t = bb.reshape(S5_CH, gl, S5_STATE, S5_GROUP)
        return jnp.einsum('cgph,gk->cghkp', t, eye).reshape(S5_CH, LANES, S5_CL)

    def out_blocks(cc):
        t = cc.reshape(S5_CH, gl, S5_GROUP, S5_STATE)
        return jnp.einsum('cghp,gk->ckpgh', t, eye).reshape(S5_CH, S5_CL, LANES)

    return dict(
        w_in=jnp.concatenate([in_blocks(bb_re), in_blocks(bb_im)], axis=-1).astype(BF16),
        c_out=jnp.concatenate([out_blocks(c_re.astype(F32)), -out_blocks(c_im.astype(F32))],
                              axis=1).astype(BF16),
        ab_re=ab_re.reshape(S5_CH, 1, S5_CL), ab_im=ab_im.reshape(S5_CH, 1, S5_CL),
        d=d_skip.reshape(1, D_MODEL),
        glu_v=glu_v.astype(BF16), glu_g=glu_g.astype(BF16))


def s5_layer(xn, h_re0, h_im0, prm, cfg):
    bsz = h_re0.shape[0]
    if cfg["time_major"]:
        L, _, d = xn.shape
        u_tm = xn.reshape(L * bsz, d)
    else:
        _, L, d = xn.shape
        u_tm = jnp.swapaxes(xn, 0, 1).reshape(L * bsz, d)
    y, hr, hi = s5_mix(u_tm, h_re0.reshape(bsz, S5_LANES), h_im0.reshape(bsz, S5_LANES), prm,
                       bsz=bsz, tr=cfg["s5_tr"], nco=cfg["s5_nco"])
    y = y.reshape(L, bsz, d)
    if not cfg["time_major"]:
        y = jnp.swapaxes(y, 0, 1)
    return ((y, None), hr.reshape(bsz, S5_GROUPS, S5_STATE), hi.reshape(bsz, S5_GROUPS, S5_STATE))


def _mamba_in_kernel(x_ref, cp_ref, wz_ref, wx_ref, wdt_ref, cw_ref, cb_ref, dtb_ref,
                     z_out, xs_out, b_out, c_out, dt_out, nc_out, buf, *, bb, tl):
    d = x_ref.shape[-1]
    m = bb * tl
    cd = M_CONV_DIM
    l = pl.program_id(1)

    @pl.when(l == 0)
    def _():
        buf[:, :SUBLANES - (M_CONV - 1), :] = jnp.zeros((bb, SUBLANES - (M_CONV - 1), cd), F32)
        buf[:, SUBLANES - (M_CONV - 1):SUBLANES, :] = cp_ref[...]

    @pl.when(l > 0)
    def _():
        buf[:, :SUBLANES, :] = buf[:, tl:, :]

    x = x_ref[...].reshape(m, d).astype(BF16)
    nbc = M_GROUPS * M_STATE
    for ci in range(cd // nbc):
        cs = slice(ci * nbc, (ci + 1) * nbc)
        xbc = _dot(x, wx_ref[:, cs])
        buf[:, SUBLANES:, cs] = xbc.reshape(bb, tl, nbc)
        conv = cb_ref[:, cs] + xbc * cw_ref[M_CONV - 1:M_CONV, cs]
        for j in range(1, M_CONV):
            sh = buf[:, SUBLANES - j:SUBLANES - j + tl, cs]
            conv = conv + sh.reshape(m, nbc) * cw_ref[M_CONV - 1 - j:M_CONV - j, cs]
        act = (conv * jax.nn.sigmoid(conv)).reshape(bb, tl, nbc)
        if ci * nbc < M_INNER:
            xs_out[:, :, cs] = act
            z_out[:, :, cs] = _dot(x, wz_ref[:, cs]).reshape(bb, tl, nbc)
        elif ci * nbc == M_INNER:
            b_out[...] = act
        else:
            c_out[...] = act
    nc_out[...] = buf[:, tl + SUBLANES - (M_CONV - 1):, :]
    dt_raw = _dot(x, wdt_ref[...])[:, :M_HEADS]
    dt_out[...] = _softplus(dt_raw + dtb_ref[...]).reshape(bb, tl, M_HEADS)


def mamba_in(xn, conv_prev, prm, *, bb, tl):
    bsz, L, d = xn.shape
    cd = M_CONV_DIM
    nbc = M_GROUPS * M_STATE
    seq = lambda w: pl.BlockSpec((bb, tl, w), lambda b, l: (b, l, 0))
    cps = pl.BlockSpec((bb, M_CONV - 1, cd), lambda b, l: (b, 0, 0))
    return pl.pallas_call(
        functools.partial(_mamba_in_kernel, bb=bb, tl=tl),
        grid=(bsz // bb, L // tl),
        in_specs=[seq(d), cps, _resident((d, M_INNER)), _resident((d, cd)),
                  _resident((d, LANES)),
                  _full((M_CONV, cd)), _full((1, cd)), _full((1, M_HEADS))],
        out_specs=[seq(M_INNER), seq(M_INNER), seq(nbc), seq(nbc), seq(M_HEADS), cps],
        out_shape=[jax.ShapeDtypeStruct((bsz, L, M_INNER), F32),
                   jax.ShapeDtypeStruct((bsz, L, M_INNER), F32),
                   jax.ShapeDtypeStruct((bsz, L, nbc), F32),
                   jax.ShapeDtypeStruct((bsz, L, nbc), F32),
                   jax.ShapeDtypeStruct((bsz, L, M_HEADS), F32),
                   jax.ShapeDtypeStruct((bsz, M_CONV - 1, cd), F32)],
        scratch_shapes=[pltpu.VMEM((bb, SUBLANES + tl, cd), F32)],
        compiler_params=_cparams(("parallel", "arbitrary")),
        name="mamba_in",
    )(xn, conv_prev, prm["wz"], prm["wx"], prm["wdt"], prm["conv_w"], prm["conv_b"],
      prm["dt_bias"])


def _ssd_kernel(xs_ref, b_ref, c_ref, z_ref, dt_ref, hv_ref, ex_ref, nw_ref, s0_ref,
                y_out, s_out, s_scr, *, bb, tl, c):
    p, n = M_HEADDIM, M_STATE
    hpg = M_HEADS // M_GROUPS
    gw = M_INNER // M_GROUPS
    nchunk = tl // c

    @pl.when(pl.program_id(1) == 0)
    def _():
        s_scr[...] = s0_ref[...]

    row = lax.broadcasted_iota(jnp.int32, (c, c), 0)
    col = lax.broadcasted_iota(jnp.int32, (c, c), 1)
    incl = row >= col
    tril = jnp.where(incl, 1.0, 0.0).astype(F32)
    triu = jnp.where(row <= col, 1.0, 0.0).astype(F32)
    eye = jnp.where(row == col, 1.0, 0.0).astype(F32)
    row_p = lax.broadcasted_iota(jnp.int32, (c, 2 * c), 0)
    lane_p = lax.broadcasted_iota(jnp.int32, (c, 2 * c), 1)
    lo_pair = lane_p < c
    incl_pair = row_p >= (lane_p & (c - 1))
    row_b = lax.broadcasted_iota(jnp.int32, (2 * c, 2 * p), 0)
    lane_b = lax.broadcasted_iota(jnp.int32, (2 * c, 2 * p), 1)
    bd_mask = (row_b >> (c.bit_length() - 1)) == (lane_b >> (p.bit_length() - 1))
    a_neg = hv_ref[0:1, :]
    d_exp = nw_ref[1:2, :]
    norm_w = nw_ref[0:1, :]
    expand = ex_ref[...]

    def chunk(ci, carry):
        rows = pl.ds(pl.multiple_of(ci * c, c), c)
        pre = []
        for s in range(bb):
            xs = xs_ref[s, rows, :]
            dt = dt_ref[s, rows, :]
            a = dt * a_neg
            cum = _dot(tril, a, HIGHEST)
            cum_end = cum[c - 1:c, :]
            pre.append(dict(
                xs=xs, cum=cum,
                cum_t=_dot_tn(a, triu, HIGHEST),
                dt_t=_dot_tn(dt, eye, HIGHEST),
                e_end=jnp.exp(cum_end),
                xend=xs * _dot(dt * jnp.exp(cum_end - cum), expand),
                ecum_x=_dot(jnp.exp(cum), expand)))
        yd = [[] for _ in range(bb)]
        yo = [[] for _ in range(bb)]
        for gi in range(M_GROUPS):
            for s in range(bb):
                q_ = pre[s]
                xs, cum, cum_t, dt_t = q_["xs"], q_["cum"], q_["cum_t"], q_["dt_t"]
                b_g = b_ref[s, rows, gi * n:(gi + 1) * n]
                c_g = c_ref[s, rows, gi * n:(gi + 1) * n]
                cb = _dot_nt(c_g, b_g)
                cb2 = jnp.concatenate([cb, cb], axis=1)
                for q in range(hpg // 2):
                    h0 = gi * hpg + 2 * q
                    col = jnp.where(lo_pair, cum[:, h0:h0 + 1], cum[:, h0 + 1:h0 + 2])
                    rowv = jnp.concatenate([cum_t[h0:h0 + 1, :], cum_t[h0 + 1:h0 + 2, :]], axis=1)
                    dtrow = jnp.concatenate([dt_t[h0:h0 + 1, :], dt_t[h0 + 1:h0 + 2, :]], axis=1)
                    lmat = jnp.where(incl_pair, jnp.exp(jnp.where(incl_pair, col - rowv, 0.0)), 0.0)
                    x_pair = xs[:, h0 * p:(h0 + 2) * p]
                    x_bd = jnp.where(bd_mask, jnp.concatenate([x_pair, x_pair], axis=0), 0.0)
                    yd[s].append(_dot(cb2 * lmat * dtrow, x_bd))
                s_g = s_scr[s, gi * hpg:(gi + 1) * hpg].reshape(hpg * p, n)
                yo[s].append(_dot_nt(c_g, s_g))
                upd = _dot_tn(q_["xend"][:, gi * gw:(gi + 1) * gw], b_g)
                for hh in range(hpg):
                    h = gi * hpg + hh
                    s_scr[s, h] = (s_g[hh * p:(hh + 1) * p] * q_["e_end"][:, h:h + 1]
                                   + upd[hh * p:(hh + 1) * p])
        for s in range(bb):
            q_ = pre[s]
            y = (jnp.concatenate(yd[s], axis=-1) + jnp.concatenate(yo[s], axis=-1) * q_["ecum_x"]
                 + d_exp * q_["xs"])
            zz = z_ref[s, rows, :]
            y = y * (zz * jax.nn.sigmoid(zz))
            outs = []
            for gi in range(M_GROUPS):
                yg = y[:, gi * gw:(gi + 1) * gw]
                outs.append(yg * lax.rsqrt(jnp.mean(yg * yg, -1, keepdims=True) + EPS))
            y_out[s, rows, :] = jnp.concatenate(outs, axis=-1) * norm_w
        return carry

    lax.fori_loop(0, nchunk, chunk, 0)

    @pl.when(pl.program_id(1) == pl.num_programs(1) - 1)
    def _():
        s_out[...] = s_scr[...]


def ssd_scan(xs, bm, cm, z, dt, prm, s0, *, bb, tl, c):
    bsz, L, _ = xs.shape
    nbc = M_GROUPS * M_STATE
    seq = lambda w: pl.BlockSpec((bb, tl, w), lambda b, l: (b, l, 0))
    st = pl.BlockSpec((bb, M_HEADS, M_HEADDIM, M_STATE), lambda b, l: (b, 0, 0, 0))
    return pl.pallas_call(
        functools.partial(_ssd_kernel, bb=bb, tl=tl, c=c),
        grid=(bsz // bb, L // tl),
        in_specs=[seq(M_INNER), seq(nbc), seq(nbc), seq(M_INNER), seq(M_HEADS),
                  _full((SUBLANES, M_HEADS)), _full((M_HEADS, M_INNER)),
                  _full((SUBLANES, M_INNER)), st],
        out_specs=[seq(M_INNER), st],
        out_shape=[jax.ShapeDtypeStruct((bsz, L, M_INNER), F32),
                   jax.ShapeDtypeStruct(s0.shape, F32)],
        scratch_shapes=[pltpu.VMEM((bb, M_HEADS, M_HEADDIM, M_STATE), F32)],
        compiler_params=_cparams(("parallel", "arbitrary")),
        name="ssd_scan",
    )(xs, bm, cm, z, dt, prm["hv"], prm["expand"], prm["nw"], s0)


def _prep_mamba(in_proj, conv_w, conv_b, dt_bias, a_log, d_skip, norm_w, out_proj):
    d = D_MODEL
    wdt = jnp.zeros((d, LANES), F32).at[:, :M_HEADS].set(in_proj[:, M_INNER + M_CONV_DIM:])
    expand = jnp.repeat(jnp.eye(M_HEADS, dtype=F32), M_HEADDIM, axis=1)
    zrow_h = jnp.zeros((M_HEADS,), F32)
    zrow_i = jnp.zeros((M_INNER,), F32)
    return dict(
        wz=in_proj[:, :M_INNER].astype(BF16),
        wx=in_proj[:, M_INNER:M_INNER + M_CONV_DIM].astype(BF16),
        wdt=wdt.astype(BF16),
        conv_w=conv_w, conv_b=conv_b.reshape(1, M_CONV_DIM),
        dt_bias=dt_bias.reshape(1, M_HEADS),
        hv=jnp.stack([-jnp.exp(a_log.astype(F32))] + [zrow_h] * 7),
        expand=expand,
        nw=jnp.stack([norm_w, jnp.repeat(d_skip, M_HEADDIM)] + [zrow_i] * 6),
        out_proj=out_proj.astype(BF16))


def mamba_layer(xn, conv_prev, ssm_prev, prm, cfg):
    bsz, L, d = xn.shape
    z, xs, bm, cm, dt, new_conv = mamba_in(xn, conv_prev, prm, bb=cfg["mamba_bb"],
                                           tl=cfg["mamba_tl"])
    y, ssm_new = ssd_scan(xs, bm, cm, z, dt, prm, ssm_prev,
                          bb=cfg["ssd_bb"], tl=cfg["ssd_tl"], c=cfg["ssd_c"])
    return (y, prm["out_proj"]), new_conv, ssm_new


def _config(bsz, L):
    if L >= 512:
        return dict(tm=512, ffn_fb=bsz, ffn_ft=512 // bsz, time_major=bsz == SUBLANES,
                    proj_bb=1, proj_tl=256,
                    scan_bb=min(bsz, 8), wkv_tl=256, wkv_c=64, wkv_heads=4,
                    mamba_bb=1, mamba_tl=256,
                    ssd_bb=2, ssd_tl=256, ssd_c=M_CHUNK, s5_tr=1024,
                    s5_nco=2 if bsz <= SUBLANES else 1)
    rows = bsz * L
    tm = min(512, rows)
    return dict(tm=tm, ffn_fb=tm // L, ffn_ft=L, time_major=False,
                proj_bb=min(bsz, 256 // L), proj_tl=L,
                scan_bb=min(bsz, 16), wkv_tl=L, wkv_c=L, wkv_heads=2,
                mamba_bb=min(bsz, 256 // L), mamba_tl=L,
                ssd_bb=min(bsz, 4), ssd_tl=L, ssd_c=min(M_CHUNK, L), s5_tr=rows,
                s5_nco=2 if bsz <= SUBLANES else 1)


def _trunk(x, p, states, layer_prms, shared, cfg):
    bsz, L, d = x.shape
    xn = h = x
    v_first = None
    new_states = []
    for i in range(DEPTH):
        st_a, st_b = states[2 * i], states[2 * i + 1]
        prm = layer_prms[i]
        kind = i % 3
        if kind == 0:
            (mix, wo), n_a, n_b, v_first = rwkv_layer(
                xn, st_a, st_b, v_first, prm, cfg, shared["norm_mix"][0] if i == 0 else None)
        elif kind == 1:
            (mix, wo), n_a, n_b = s5_layer(xn, st_a, st_b, prm, cfg)
        else:
            (mix, wo), n_a, n_b = mamba_layer(xn, st_a, st_b, prm, cfg)
        new_states += [n_a, n_b]
        nxt = shared["norm_mix"][i + 1] if i + 1 < DEPTH else shared["final_norm"]
        mix_tm = cfg["time_major"] and kind == 1
        xn_tm = cfg["time_major"] and (i + 1) % 3 == 1 and i + 1 < DEPTH
        h, xn = ffn_ple(h, mix, wo, p, i, shared, nxt, fb=cfg["ffn_fb"], ft=cfg["ffn_ft"],
                        mix_tm=mix_tm, xn_tm=xn_tm)
    return xn, new_states


def kernel(x_prompt, x_sample, p_prompt, p_sample, state_l0_shift, state_l0_wkv, state_l1_s5_re, state_l1_s5_im, state_l2_conv, state_l2_ssm, state_l3_shift, state_l3_wkv, l0_mu, l0_w0, l0_w1, l0_w2, l0_a0, l0_a1, l0_a2, l0_g1, l0_g2, l0_k_k, l0_k_a, l0_r_k, l0_w_rkv, l0_w_o, l0_lnx_w, l0_lnx_b, l1_a_re, l1_a_im, l1_log_dt, l1_b_re, l1_b_im, l1_c_re, l1_c_im, l1_d, l1_glu_v, l1_glu_g, l2_in_proj, l2_conv_w, l2_conv_b, l2_dt_bias, l2_a_log, l2_d, l2_norm_w, l2_out_proj, l3_mu, l3_w0, l3_w1, l3_w2, l3_a0, l3_a1, l3_a2, l3_g1, l3_g2, l3_k_k, l3_k_a, l3_r_k, l3_w_rkv, l3_w_o, l3_lnx_w, l3_lnx_b, l3_v0, l3_v1, l3_v2, norm_mix, norm_ffn, norm_ple, ffn_w1, ffn_w3, ffn_w2, ple_proj, ple_gate, final_norm):
    layer_prms = (
        _prep_rwkv(l0_mu, l0_w0, l0_w1, l0_w2, l0_a0, l0_a1, l0_a2, l0_g1, l0_g2,
                   l0_k_k, l0_k_a, l0_r_k, l0_w_rkv, l0_w_o, l0_lnx_w, l0_lnx_b),
        _prep_s5(l1_a_re, l1_a_im, l1_log_dt, l1_b_re, l1_b_im, l1_c_re, l1_c_im,
                 l1_d, l1_glu_v, l1_glu_g),
        _prep_mamba(l2_in_proj, l2_conv_w, l2_conv_b, l2_dt_bias, l2_a_log, l2_d,
                    l2_norm_w, l2_out_proj),
        _prep_rwkv(l3_mu, l3_w0, l3_w1, l3_w2, l3_a0, l3_a1, l3_a2, l3_g1, l3_g2,
                   l3_k_k, l3_k_a, l3_r_k, l3_w_rkv, l3_w_o, l3_lnx_w, l3_lnx_b,
                   (l3_v0, l3_v1, l3_v2)),
    )
    shared = dict(norm_mix=norm_mix, norm_ffn=norm_ffn.reshape(DEPTH, 1, D_MODEL),
                  norm_ple=norm_ple.reshape(DEPTH, 1, D_MODEL), final_norm=final_norm,
                  ffn_w1=ffn_w1.astype(BF16), ffn_w3=ffn_w3.astype(BF16),
                  ffn_w2=ffn_w2.astype(BF16), ple_gate=ple_gate.astype(BF16),
                  ple_proj=ple_proj.astype(BF16))
    sample_states = [state_l0_shift, state_l0_wkv, state_l1_s5_re, state_l1_s5_im,
                     state_l2_conv, state_l2_ssm, state_l3_shift, state_l3_wkv]
    bp = x_prompt.shape[0]
    prompt_states = [jnp.zeros((bp,) + s.shape[1:], F32) for s in sample_states]
    y_prompt, new_p = _trunk(x_prompt, p_prompt, prompt_states, layer_prms, shared,
                             _config(*x_prompt.shape[:2]))
    y_sample, new_s = _trunk(x_sample, p_sample, sample_states, layer_prms, shared,
                             _config(*x_sample.shape[:2]))
    return (y_prompt, y_sample, *new_p, *new_s)
```

```python
import functools
import math

import jax
import jax.numpy as jnp
from jax import lax
from jax.experimental import pallas as pl
from jax.experimental.pallas import tpu as pltpu

F32 = jnp.float32
BF16 = jnp.bfloat16
HIGHEST = lax.Precision.HIGHEST

D_MODEL = 1024
DEPTH = 4
PLE_DIM = 256
EPS = 1e-6
D_FF = 2816
RW_HEAD = 64
RW_HEADS = D_MODEL // RW_HEAD
RW_GN_EPS = 64e-5
S5_GROUP = 16
S5_GROUPS = D_MODEL // S5_GROUP
S5_STATE = 64
S5_LANES = S5_GROUPS * S5_STATE
M_INNER = 2 * D_MODEL
M_HEADDIM = 64
M_HEADS = M_INNER // M_HEADDIM
M_STATE = 128
M_GROUPS = 4
M_CONV = 4
M_CHUNK = 64
M_CONV_DIM = M_INNER + 2 * M_GROUPS * M_STATE

LANES = 128
MXU_TILE = 256
SUBLANES = 8
VMEM_LIMIT = 56 * 1024 * 1024

SCAN_PREC = None


def _cparams(sem):
    return pltpu.CompilerParams(dimension_semantics=sem, vmem_limit_bytes=VMEM_LIMIT)


def _operands(a, b, prec):
    if prec is None:
        return a.astype(BF16), b.astype(BF16)
    return a, b


def _dot(a, b, prec=None):
    a, b = _operands(a, b, prec)
    return jnp.dot(a, b, preferred_element_type=F32, precision=prec)


def _dot_nt(a, b, prec=None):
    a, b = _operands(a, b, prec)
    return lax.dot_general(a, b, (((1,), (1,)), ((), ())),
                           preferred_element_type=F32, precision=prec)


def _dot_tn(a, b, prec=None):
    a, b = _operands(a, b, prec)
    return lax.dot_general(a, b, (((0,), (0,)), ((), ())),
                           preferred_element_type=F32, precision=prec)


def _dot_01(m01, x):
    hi = x.astype(BF16)
    rest = x - hi.astype(F32)
    mid = rest.astype(BF16)
    lo = (rest - mid.astype(F32)).astype(BF16)
    mb = m01.astype(BF16)
    return (jnp.dot(mb, hi, preferred_element_type=F32)
            + jnp.dot(mb, mid, preferred_element_type=F32)
            + jnp.dot(mb, lo, preferred_element_type=F32))


def _rms(x, g):
    return x * lax.rsqrt(jnp.mean(x * x, axis=-1, keepdims=True) + EPS) * g


def _log_sigmoid(z):
    return -(jnp.maximum(-z, 0.0) + jnp.log1p(jnp.exp(-jnp.abs(z))))


def _softplus(z):
    return jnp.maximum(z, 0.0) + jnp.log1p(jnp.exp(-jnp.abs(z)))


def _full(shape):
    n = len(shape)
    return pl.BlockSpec(shape, lambda *_: (0,) * n)


def _resident(shape):
    n = len(shape)
    return pl.BlockSpec(shape, lambda *_: (0,) * n, pipeline_mode=pl.Buffered(1))


def _ffn_kernel(*refs, project, mix_tm, xn_tm):
    if project:
        (h_ref, mix_ref, wo_ref, p_ref, nf_ref, np_ref, nn_ref, w1_ref, w3_ref, w2_ref,
         pg_ref, pp_ref, h_out, xn_out) = refs
    else:
        (h_ref, mix_ref, p_ref, nf_ref, np_ref, nn_ref, w1_ref, w3_ref, w2_ref,
         pg_ref, pp_ref, h_out, xn_out) = refs
    fb, ft, d = h_ref.shape
    tm = fb * ft
    mix = mix_ref[...]
    if mix_tm:
        mix = jnp.swapaxes(mix, 0, 1)
    mix = mix.reshape(tm, mix.shape[-1])
    y = _dot(mix, wo_ref[...]) if project else mix
    hin = h_ref[...].reshape(tm, d) + y
    hn = _rms(hin, nf_ref[...]).astype(BF16)
    acc = None
    for ci in range(D_FF // MXU_TILE):
        cs = slice(ci * MXU_TILE, (ci + 1) * MXU_TILE)
        a = _dot(hn, w1_ref[:, cs])
        b = _dot(hn, w3_ref[:, cs])
        t = (a * jax.nn.sigmoid(a) * b).astype(BF16)
        part = _dot(t, w2_ref[cs, :])
        acc = part if acc is None else acc + part
    h2 = hin + acc
    gate = jax.nn.sigmoid(_dot(_rms(h2, np_ref[...]).astype(BF16), pg_ref[...]))
    h3 = h2 + gate * _dot(p_ref[...].reshape(tm, PLE_DIM).astype(BF16), pp_ref[...])
    h_out[...] = h3.reshape(fb, ft, d)
    xn = _rms(h3, nn_ref[...]).reshape(fb, ft, d)
    xn_out[...] = jnp.swapaxes(xn, 0, 1) if xn_tm else xn


def ffn_ple(h, mix, wo, p_all, layer, shared, nn, *, fb, ft, mix_tm=False, xn_tm=False):
    bsz, L, d = h.shape
    km = mix.shape[-1]
    nlt = L // ft
    seq = lambda w: pl.BlockSpec((fb, ft, w), lambda i: (i // nlt, i % nlt, 0))
    tmaj = lambda w: pl.BlockSpec((ft, fb, w), lambda i: (i % nlt, 0, 0))
    lay = lambda *shape: pl.BlockSpec((None,) + shape, lambda i: (layer,) + (0,) * len(shape),
                                      pipeline_mode=pl.Buffered(1))
    ins = [h, mix]
    specs = [seq(d), tmaj(km) if mix_tm else seq(km)]
    if wo is not None:
        ins.append(wo)
        specs.append(_resident(wo.shape))
    ins += [p_all, shared["norm_ffn"], shared["norm_ple"], nn.reshape(1, d),
            shared["ffn_w1"], shared["ffn_w3"], shared["ffn_w2"],
            shared["ple_gate"], shared["ple_proj"]]
    specs += [pl.BlockSpec((None, fb, ft, PLE_DIM), lambda i: (layer, i // nlt, i % nlt, 0)),
              lay(1, d), lay(1, d), _full((1, d)),
              lay(d, D_FF), lay(d, D_FF), lay(D_FF, d), lay(d, d), lay(PLE_DIM, d)]
    xn_shape = (L, bsz, d) if xn_tm else (bsz, L, d)
    return pl.pallas_call(
        functools.partial(_ffn_kernel, project=wo is not None, mix_tm=mix_tm, xn_tm=xn_tm),
        grid=(bsz // fb * nlt,),
        in_specs=specs,
        out_specs=[seq(d), tmaj(d) if xn_tm else seq(d)],
        out_shape=[jax.ShapeDtypeStruct((bsz, L, d), F32), jax.ShapeDtypeStruct(xn_shape, F32)],
        compiler_params=_cparams(("parallel",)),
        name="ffn_ple",
    )(*ins)


def _rwkv_proj_kernel(*refs, bb, tl, with_v, norm_in):
    it = iter(refs)
    x_ref, sh_ref = next(it), next(it)
    vf_ref = next(it) if with_v else None
    nw_ref = next(it) if norm_in else None
    mu_ref, b0_ref, wrkv_ref, w1_ref, w2_ref, a1_ref, a2_ref, g1_ref, g2_ref = (
        next(it) for _ in range(9))
    if with_v:
        v1_ref, v2_ref = next(it), next(it)
    r_out, lw_out, k_out, v_out, a_out, g_out, last_out, carry = it
    d = x_ref.shape[-1]
    m = bb * tl

    @pl.when(pl.program_id(1) == 0)
    def _():
        carry[...] = sh_ref[...]

    x = x_ref[...].reshape(m, d)
    if norm_in:
        x = _rms(x, nw_ref[...])
    prev = jnp.broadcast_to(carry[...], (bb, tl, d)).reshape(m, d)
    rolled = pltpu.roll(x, 1, 0)
    t_in_tile = lax.broadcasted_iota(jnp.int32, (m, d), 0) & (tl - 1)
    x_prev = jnp.where(t_in_tile == 0, prev, rolled)
    last = x.reshape(bb, tl, d)[:, tl - 1:tl, :]
    carry[...] = last
    last_out[...] = last

    xx = x_prev - x

    def mix(j):
        return (x + xx * mu_ref[j:j + 1, :]).astype(BF16)

    shp = (bb, tl, d)
    xw, xa, xg, xv = mix(1), mix(4), mix(5), mix(3)
    zw = b0_ref[0:1, :] + _dot(jnp.tanh(_dot(xw, w1_ref[...])).astype(BF16), w2_ref[...])
    lw_out[...] = (-jnp.exp(_log_sigmoid(zw) - 0.5)).reshape(shp)
    a = jax.nn.sigmoid(b0_ref[1:2, :] + _dot(_dot(xa, a1_ref[...]).astype(BF16), a2_ref[...]))
    a_out[...] = a.reshape(shp)
    g_out[...] = _dot(jax.nn.sigmoid(_dot(xg, g1_ref[...])).astype(BF16),
                      g2_ref[...]).reshape(shp)
    if with_v:
        lam = jax.nn.sigmoid(b0_ref[2:3, :]
                             + _dot(_dot(xv, v1_ref[...]).astype(BF16), v2_ref[...]))
    v = _dot(xv, wrkv_ref[2])
    if with_v:
        v = v + (vf_ref[...].reshape(m, d) - v) * lam
    v_out[...] = v.reshape(shp)
    r_out[...] = _dot(mix(0), wrkv_ref[0]).reshape(shp)
    k_out[...] = _dot(mix(2), wrkv_ref[1]).reshape(shp)


def rwkv_proj(x, shift, v_first, prm, norm_w, *, bb, tl):
    bsz, L, d = x.shape
    with_v = v_first is not None
    norm_in = norm_w is not None
    seq = pl.BlockSpec((bb, tl, d), lambda b, l: (b, l, 0))
    row = pl.BlockSpec((bb, 1, d), lambda b, l: (b, 0, 0))
    ins = [x, shift.reshape(bsz, 1, d)]
    specs = [seq, row]
    if with_v:
        ins.append(v_first)
        specs.append(seq)
    if norm_in:
        ins.append(norm_w.reshape(1, d))
        specs.append(_full((1, d)))
    names = ["mu", "b0", "w_rkv", "w1", "w2", "a1", "a2", "g1", "g2"]
    if with_v:
        names += ["v1", "v2"]
    for nme in names:
        ins.append(prm[nme])
        specs.append(_resident(prm[nme].shape))
    return pl.pallas_call(
        functools.partial(_rwkv_proj_kernel, bb=bb, tl=tl, with_v=with_v, norm_in=norm_in),
        grid=(bsz // bb, L // tl),
        in_specs=specs,
        out_specs=[seq] * 6 + [row],
        out_shape=[jax.ShapeDtypeStruct((bsz, L, d), F32)] * 6
        + [jax.ShapeDtypeStruct((bsz, 1, d), F32)],
        scratch_shapes=[pltpu.VMEM((bb, 1, d), F32)],
        compiler_params=_cparams(("parallel", "arbitrary")),
        name="rwkv_proj",
    )(*ins)


def _unit_lower_inverse(n_mats, eye, same_block, size, mm):
    blk8 = same_block(3)
    n8 = [jnp.where(blk8, m, 0.0) for m in n_mats]
    n8_2 = [mm(m, m) for m in n8]
    t = [eye + m for m in n8]
    t = [ti + mm(ti, m2) for ti, m2 in zip(t, n8_2)]
    n8_4 = [mm(m2, m2) for m2 in n8_2]
    t = [ti + mm(ti, m4) for ti, m4 in zip(t, n8_4)]
    shift = 3
    while (1 << shift) < size:
        inner, outer = same_block(shift), same_block(shift + 1)
        off = [jnp.where(inner, 0.0, jnp.where(outer, m, 0.0)) for m in n_mats]
        left = [mm(ti, oi) for ti, oi in zip(t, off)]
        t = [ti + mm(li, ti) for ti, li in zip(t, left)]
        shift += 1
    return t


def _wkv_kernel(r_ref, lw_ref, k_ref, v_ref, a_ref, g_ref, hp_ref, s0_ref,
                z_out, s_out, s_scr, *, bb, tl, c, prec):
    n = RW_HEAD
    nheads = s_scr.shape[1]
    nchunk = tl // c

    @pl.when(pl.program_id(2) == 0)
    def _():
        s_scr[...] = s0_ref[...]

    row = lax.broadcasted_iota(jnp.int32, (c, c), 0)
    col = lax.broadcasted_iota(jnp.int32, (c, c), 1)
    tril = jnp.where(row >= col, 1.0, 0.0).astype(F32)
    eye = jnp.where(row == col, 1.0, 0.0).astype(F32)

    def same_block(shift):
        return (row >> shift) == (col >> shift)

    row2 = lax.broadcasted_iota(jnp.int32, (2 * c, c), 0)
    col2 = lax.broadcasted_iota(jnp.int32, (2 * c, c), 1)
    mask2 = col2 <= jnp.where(row2 < c, row2 - 1, row2 - c)
    k_k = hp_ref[0:1, :]
    k_a = hp_ref[1:2, :]
    r_k = hp_ref[2:3, :]
    ln_w = hp_ref[3:4, :]
    ln_b = hp_ref[4:5, :]

    chains = [(s, h) for s in range(bb) for h in range(nheads)]

    def chunk(ci, carry):
        rows = pl.ds(pl.multiple_of(ci * c, c), c)
        lw = [lw_ref[s, rows, :] for s in range(bb)]
        cum = [_dot_01(tril, x) for x in lw]
        seq = []
        for s in range(bb):
            r = r_ref[s, rows, :]
            k = k_ref[s, rows, :]
            a = a_ref[s, rows, :]
            cum_end = cum[s][c - 1:c, :]
            e_neg = jnp.exp(-cum[s])
            e_end = jnp.exp(cum_end - cum[s])
            k_mod = k * (1.0 + (a - 1.0) * k_a)
            seq.append(dict(
                a=a, v=v_ref[s, rows, :], kk_raw=k * k_k,
                r_t=r * jnp.exp(cum[s]), e_neg=e_neg, e_end=e_end,
                e_prev=jnp.exp(cum[s] - lw[s]), g_end=jnp.exp(cum_end),
                k_t=k_mod * e_neg, k_end=k_mod * e_end, rk=r * k_mod * r_k))
        ch = []
        for s, h in chains:
            q = seq[s]
            sl = slice(h * n, (h + 1) * n)
            kk = q["kk_raw"][:, sl]
            kk = kk * lax.rsqrt(jnp.maximum(jnp.sum(kk * kk, -1, keepdims=True), 1e-24))
            b_h = kk * q["a"][:, sl]
            a_t = -kk * q["e_prev"][:, sl]
            ch.append(dict(
                ar=jnp.concatenate([a_t, q["r_t"][:, sl]], axis=0),
                b_t=b_h * q["e_neg"][:, sl], k_t=q["k_t"][:, sl],
                kb_end=jnp.concatenate([q["k_end"][:, sl], b_h * q["e_end"][:, sl]], axis=0),
                v=q["v"][:, sl], g_end=q["g_end"][:, sl],
                bonus=jnp.sum(q["rk"][:, sl], -1, keepdims=True) * q["v"][:, sl]))
        ab = [jnp.where(mask2, _dot_nt(x["ar"], x["b_t"], prec), 0.0) for x in ch]
        ak = [jnp.where(mask2, _dot_nt(x["ar"], x["k_t"], prec), 0.0) for x in ch]
        t_inv = _unit_lower_inverse([m[:c] for m in ab], eye, same_block, c,
                                    lambda x, y: _dot(x, y, prec))
        s0 = [s_scr[s, h] for s, h in chains]
        x1 = [_dot_nt(x["ar"], si, prec) for x, si in zip(ch, s0)]
        x1 = [xi + _dot(m, x["v"], prec) for xi, m, x in zip(x1, ak, ch)]
        u = [_dot(ti, xi[:c], prec) for ti, xi in zip(t_inv, x1)]
        o = [xi[c:] + _dot(m[c:], ui, prec) for xi, m, ui in zip(x1, ab, u)]
        s_new = [si * x["g_end"] + _dot_tn(jnp.concatenate([x["v"], ui], axis=0), x["kb_end"], prec)
                 for x, si, ui in zip(ch, s0, u)]
        for (s, h), sn in zip(chains, s_new):
            s_scr[s, h] = sn
        normed = []
        for oi in o:
            mean = jnp.mean(oi, -1, keepdims=True)
            var = jnp.mean(jnp.square(oi - mean), -1, keepdims=True)
            normed.append((oi - mean) * lax.rsqrt(var + RW_GN_EPS))
        for s in range(bb):
            idx = [i for i, (si, _) in enumerate(chains) if si == s]
            o_all = jnp.concatenate([normed[i] for i in idx], axis=-1)
            bonus_all = jnp.concatenate([ch[i]["bonus"] for i in idx], axis=-1)
            z_out[s, rows, :] = (o_all * ln_w + ln_b + bonus_all) * g_ref[s, rows, :]
        return carry

    lax.fori_loop(0, nchunk, chunk, 0)

    @pl.when(pl.program_id(2) == pl.num_programs(2) - 1)
    def _():
        s_out[...] = s_scr[...]


def _half_sums(x, lo):
    s_lo = jnp.sum(jnp.where(lo, x, 0.0), -1, keepdims=True)
    s_hi = jnp.sum(jnp.where(lo, 0.0, x), -1, keepdims=True)
    return jnp.where(lo, s_lo, s_hi)


def _wkv_pair_kernel(r_ref, lw_ref, k_ref, v_ref, a_ref, g_ref, hp_ref, s0_ref,
                     z_out, s_out, s_scr, *, bb, tl, c, prec):
    n = RW_HEAD
    npair = r_ref.shape[-1] // LANES
    nchunk = tl // c
    units = [(s, q) for s in range(bb) for q in range(npair)]

    @pl.when(pl.program_id(2) == 0)
    def _():
        for s, q in units:
            s_scr[s, q] = jnp.concatenate([s0_ref[s, 2 * q], s0_ref[s, 2 * q + 1]], axis=-1)

    def iota2(shape):
        return (lax.broadcasted_iota(jnp.int32, shape, 0),
                lax.broadcasted_iota(jnp.int32, shape, 1))

    row, lane = iota2((c, LANES))
    col = lane & (n - 1)
    lo_c = lane < n
    eye = jnp.where(row == col, 1.0, 0.0).astype(F32)
    row2, lane2 = iota2((2 * c, LANES))
    mask2 = (lane2 & (n - 1)) <= jnp.where(row2 < c, row2 - 1, row2 - c)
    rowt, colt = iota2((c, c))
    tril = jnp.where(rowt >= colt, 1.0, 0.0).astype(F32)
    lo_s = iota2((n, LANES))[1] < n

    def same_block(shift):
        return (row >> shift) == (col >> shift)

    def expand(y):
        rows = y.shape[0]
        r_e, l_e = iota2((2 * rows, LANES))
        same_head = (r_e >> (rows.bit_length() - 1)) == (l_e >> (n.bit_length() - 1))
        return jnp.where(same_head, jnp.concatenate([y, y], axis=0), 0.0)

    def mm(x, y):
        return _dot(x, expand(y), prec)

    k_k = hp_ref[0:1, :]
    k_a = hp_ref[1:2, :]
    r_k = hp_ref[2:3, :]
    ln_w = hp_ref[3:4, :]
    ln_b = hp_ref[4:5, :]

    def chunk(ci, carry):
        rows = pl.ds(pl.multiple_of(ci * c, c), c)
        lw = [lw_ref[s, rows, :] for s in range(bb)]
        cum = [_dot_01(tril, x) for x in lw]
        seq = []
        for s in range(bb):
            r = r_ref[s, rows, :]
            k = k_ref[s, rows, :]
            a = a_ref[s, rows, :]
            cum_end = cum[s][c - 1:c, :]
            e_neg = jnp.exp(-cum[s])
            e_end = jnp.exp(cum_end - cum[s])
            k_mod = k * (1.0 + (a - 1.0) * k_a)
            seq.append(dict(
                a=a, v=v_ref[s, rows, :], kk_raw=k * k_k,
                r_t=r * jnp.exp(cum[s]), e_neg=e_neg, e_end=e_end,
                e_prev=jnp.exp(cum[s] - lw[s]), g_end=jnp.exp(cum_end),
                k_t=k_mod * e_neg, k_end=k_mod * e_end, rk=r * k_mod * r_k))
        un = []
        for s, q in units:
            p = seq[s]
            sl = slice(q * LANES, (q + 1) * LANES)
            kk = p["kk_raw"][:, sl]
            kk = kk * lax.rsqrt(jnp.maximum(_half_sums(kk * kk, lo_c), 1e-24))
            b_h = kk * p["a"][:, sl]
            a_t = -kk * p["e_prev"][:, sl]
            un.append(dict(
                ar=jnp.concatenate([a_t, p["r_t"][:, sl]], axis=0),
                bk=jnp.concatenate([expand(b_h * p["e_neg"][:, sl]), expand(p["k_t"][:, sl])],
                                   axis=0),
                kb_end=jnp.concatenate([p["k_end"][:, sl], b_h * p["e_end"][:, sl]], axis=0),
                v=p["v"][:, sl], g_end=p["g_end"][:, sl],
                bonus=_half_sums(p["rk"][:, sl], lo_c) * p["v"][:, sl]))
        g_all = [_dot_nt(x["ar"], x["bk"], prec) for x in un]
        ab = [jnp.where(mask2, m[:, :2 * c], 0.0) for m in g_all]
        ak = [jnp.where(mask2, m[:, 2 * c:], 0.0) for m in g_all]
        t_inv = _unit_lower_inverse([m[:c] for m in ab], eye, same_block, c, mm)
        s0 = [s_scr[s, q] for s, q in units]
        x1 = [_dot_nt(x["ar"], expand(si), prec) for x, si in zip(un, s0)]
        x1 = [xi + mm(m, x["v"]) for xi, m, x in zip(x1, ak, un)]
        u = [mm(ti, xi[:c]) for ti, xi in zip(t_inv, x1)]
        o = [xi[c:] + mm(m[c:], ui) for xi, m, ui in zip(x1, ab, u)]
        w = [_dot_tn(jnp.concatenate([x["v"], ui], axis=0), x["kb_end"], prec)
             for x, ui in zip(un, u)]
        for (s, q), si, wi, x in zip(units, s0, w, un):
            s_scr[s, q] = si * x["g_end"] + jnp.where(lo_s, wi[:n], wi[n:])
        normed = []
        for oi in o:
            mean = _half_sums(oi, lo_c) * (1.0 / n)
            var = _half_sums(jnp.square(oi - mean), lo_c) * (1.0 / n)
            normed.append((oi - mean) * lax.rsqrt(var + RW_GN_EPS))
        for s in range(bb):
            idx = [i for i, (si, _) in enumerate(units) if si == s]
            o_all = jnp.concatenate([normed[i] for i in idx], axis=-1)
            bonus_all = jnp.concatenate([un[i]["bonus"] for i in idx], axis=-1)
            z_out[s, rows, :] = (o_all * ln_w + ln_b + bonus_all) * g_ref[s, rows, :]
        return carry

    lax.fori_loop(0, nchunk, chunk, 0)

    @pl.when(pl.program_id(2) == pl.num_programs(2) - 1)
    def _():
        for s, q in units:
            s_out[s, 2 * q] = s_scr[s, q][:, :n]
            s_out[s, 2 * q + 1] = s_scr[s, q][:, n:]


def wkv_scan(r, lw, k, v, a, g, hp, s0, *, bb, tl, c, nheads, prec):
    bsz, L, d = r.shape
    width = nheads * RW_HEAD
    seq = pl.BlockSpec((bb, tl, width), lambda b, h, l: (b, l, h))
    st = pl.BlockSpec((bb, nheads, RW_HEAD, RW_HEAD), lambda b, h, l: (b, h, 0, 0))
    paired = 2 * c == LANES and nheads % 2 == 0
    body = _wkv_pair_kernel if paired else _wkv_kernel
    scr = (bb, nheads // 2, RW_HEAD, LANES) if paired else (bb, nheads, RW_HEAD, RW_HEAD)
    return pl.pallas_call(
        functools.partial(body, bb=bb, tl=tl, c=c, prec=prec),
        grid=(bsz // bb, d // width, L // tl),
        in_specs=[seq] * 6 + [pl.BlockSpec((SUBLANES, width), lambda b, h, l: (0, h)), st],
        out_specs=[seq, st],
        out_shape=[jax.ShapeDtypeStruct((bsz, L, d), F32),
                   jax.ShapeDtypeStruct(s0.shape, F32)],
        scratch_shapes=[pltpu.VMEM(scr, F32)],
        compiler_params=_cparams(("parallel", "parallel", "arbitrary")),
        name="wkv_scan",
    )(r, lw, k, v, a, g, hp, s0)


def rwkv_layer(x, shift, wkv, v_first, prm, cfg, norm_w=None):
    r, lw, k, v, a, g, last = rwkv_proj(x, shift, v_first, prm, norm_w,
                                        bb=cfg["proj_bb"], tl=cfg["proj_tl"])
    z, wkv_new = wkv_scan(r, lw, k, v, a, g, prm["hp"], wkv,
                          bb=cfg["scan_bb"], tl=cfg["wkv_tl"], c=cfg["wkv_c"],
                          nheads=cfg["wkv_heads"], prec=SCAN_PREC)
    return ((z, prm["w_o"]), last.reshape(last.shape[0], -1), wkv_new,
            (v if v_first is None else v_first))


def _prep_rwkv(mu, w0, w1, w2, a0, a1, a2, g1, g2, k_k, k_a, r_k, w_rkv, w_o, lnx_w, lnx_b,
               v_lora=None):
    d = D_MODEL
    zero = jnp.zeros((d,), F32)
    prm = dict(mu=mu, w_rkv=w_rkv.astype(BF16), w_o=w_o.astype(BF16),
               w1=w1.astype(BF16), w2=w2.astype(BF16), a1=a1.astype(BF16), a2=a2.astype(BF16),
               g1=g1.astype(BF16), g2=g2.astype(BF16))
    v0 = zero
    if v_lora is not None:
        v0, v1, v2 = v_lora
        prm["v1"] = v1.astype(BF16)
        prm["v2"] = v2.astype(BF16)
    prm["b0"] = jnp.stack([w0, a0, v0] + [zero] * 5)
    prm["hp"] = jnp.stack([k_k, k_a, r_k.reshape(d), lnx_w, lnx_b] + [zero] * 3)
    return prm


S5_CH = 8
S5_CL = S5_LANES // S5_CH


def _s5_kernel(u_ref, w_ref, cm_ref, ar_ref, ai_ref, d_ref, wv_ref, wg_ref, h0r_ref, h0i_ref,
               y_out, hr_out, hi_out, hs, hc, g_s, *, bsz, tr):
    i = pl.program_id(0)
    cl = S5_CL
    nstep = tr // bsz

    @pl.when(i == 0)
    def _():
        hc[:, :S5_LANES] = h0r_ref[...]
        hc[:, S5_LANES:] = h0i_ref[...]

    nco = hs.shape[0]
    for c0 in range(0, S5_CH, nco):
        chs = range(c0, c0 + nco)
        lanes = [slice(ch * LANES, (ch + 1) * LANES) for ch in chs]
        re = [slice(ch * cl, (ch + 1) * cl) for ch in chs]
        im = [slice(S5_LANES + ch * cl, S5_LANES + (ch + 1) * cl) for ch in chs]
        for k, ch in enumerate(chs):
            hs[k] = _dot(u_ref[:, lanes[k]], w_ref[ch])
        ar = [jnp.broadcast_to(ar_ref[ch], (bsz, cl)) for ch in chs]
        ai = [jnp.broadcast_to(ai_ref[ch], (bsz, cl)) for ch in chs]

        def step(t, carry, ar=ar, ai=ai):
            rows = pl.ds(pl.multiple_of(t * bsz, bsz), bsz)
            out = []
            for k in range(nco):
                hr, hi = carry[k]
                nr = ar[k] * hr - ai[k] * hi + hs[k, rows, :cl]
                ni = ar[k] * hi + ai[k] * hr + hs[k, rows, cl:]
                hs[k, rows, :cl] = nr
                hs[k, rows, cl:] = ni
                out.append((nr, ni))
            return tuple(out)

        fin = lax.fori_loop(0, nstep, step,
                            tuple((hc[:, re[k]], hc[:, im[k]]) for k in range(nco)),
                            unroll=min(nstep, SUBLANES))
        for k, ch in enumerate(chs):
            hc[:, re[k]] = fin[k][0]
            hc[:, im[k]] = fin[k][1]
            y = _dot(hs[k], cm_ref[ch]) + d_ref[:, lanes[k]] * u_ref[:, lanes[k]]
            g_s[:, lanes[k]] = jax.nn.gelu(y).astype(BF16)

    g = g_s[...]
    y_out[...] = _dot(g, wv_ref[...]) * jax.nn.sigmoid(_dot(g, wg_ref[...]))

    @pl.when(i == pl.num_programs(0) - 1)
    def _():
        hr_out[...] = hc[:, :S5_LANES]
        hi_out[...] = hc[:, S5_LANES:]


def s5_mix(u_tm, h0r, h0i, prm, *, bsz, tr, nco):
    rows, d = u_tm.shape
    cl = S5_CL
    tile = pl.BlockSpec((tr, d), lambda i: (i, 0))
    st = _full((bsz, S5_LANES))
    return pl.pallas_call(
        functools.partial(_s5_kernel, bsz=bsz, tr=tr),
        grid=(rows // tr,),
        in_specs=[tile, _resident((S5_CH, LANES, 2 * cl)), _resident((S5_CH, 2 * cl, LANES)),
                  _resident((S5_CH, 1, cl)), _resident((S5_CH, 1, cl)), _resident((1, d)),
                  _resident((d, d)), _resident((d, d)), st, st],
        out_specs=[tile, st, st],
        out_shape=[jax.ShapeDtypeStruct((rows, d), F32),
                   jax.ShapeDtypeStruct((bsz, S5_LANES), F32),
                   jax.ShapeDtypeStruct((bsz, S5_LANES), F32)],
        scratch_shapes=[pltpu.VMEM((nco, tr, 2 * cl), F32), pltpu.VMEM((bsz, 2 * S5_LANES), F32),
                        pltpu.VMEM((tr, d), BF16)],
        compiler_params=_cparams(("arbitrary",)),
        name="s5_mix",
    )(u_tm, prm["w_in"], prm["c_out"], prm["ab_re"], prm["ab_im"], prm["d"],
      prm["glu_v"], prm["glu_g"], h0r, h0i)


def _prep_s5(a_re, a_im, log_dt, b_re, b_im, c_re, c_im, d_skip, glu_v, glu_g):
    dt = jnp.exp(log_dt.astype(F32))[:, None]
    lr, li = a_re.astype(F32), a_im.astype(F32)
    mag = jnp.exp(lr * dt)
    ab_re, ab_im = mag * jnp.cos(li * dt), mag * jnp.sin(li * dt)
    den = lr * lr + li * li
    q_re = ((ab_re - 1.0) * lr + ab_im * li) / den
    q_im = (ab_im * lr - (ab_re - 1.0) * li) / den
    bb_re = q_re[..., None] * b_re - q_im[..., None] * b_im
    bb_im = q_re[..., None] * b_im + q_im[..., None] * b_re
    gl = S5_GROUPS // S5_CH
    eye = jnp.eye(gl, dtype=F32)

    def in_blocks(bb):
        t = bb.reshape(S5_CH, gl, S5_STATE, S5_GROUP)
        return jnp.einsum('cgph,gk->cghkp', t, eye).reshape(S5_CH, LANES, S5_CL)

    def out_blocks(cc):
        t = cc.reshape(S5_CH, gl, S5_GROUP, S5_STATE)
        return jnp.einsum('cghp,gk->ckpgh', t, eye).reshape(S5_CH, S5_CL, LANES)

    return dict(
        w_in=jnp.concatenate([in_blocks(bb_re), in_blocks(bb_im)], axis=-1).astype(BF16),
        c_out=jnp.concatenate([out_blocks(c_re.astype(F32)), -out_blocks(c_im.astype(F32))],
                              axis=1).astype(BF16),
        ab_re=ab_re.reshape(S5_CH, 1, S5_CL), ab_im=ab_im.reshape(S5_CH, 1, S5_CL),
        d=d_skip.reshape(1, D_MODEL),
        glu_v=glu_v.astype(BF16), glu_g=glu_g.astype(BF16))


def s5_layer(xn, h_re0, h_im0, prm, cfg):
    bsz = h_re0.shape[0]
    if cfg["time_major"]:
        L, _, d = xn.shape
        u_tm = xn.reshape(L * bsz, d)
    else:
        _, L, d = xn.shape
        u_tm = jnp.swapaxes(xn, 0, 1).reshape(L * bsz, d)
    y, hr, hi = s5_mix(u_tm, h_re0.reshape(bsz, S5_LANES), h_im0.reshape(bsz, S5_LANES), prm,
                       bsz=bsz, tr=cfg["s5_tr"], nco=cfg["s5_nco"])
    y = y.reshape(L, bsz, d)
    if not cfg["time_major"]:
        y = jnp.swapaxes(y, 0, 1)
    return ((y, None), hr.reshape(bsz, S5_GROUPS, S5_STATE), hi.reshape(bsz, S5_GROUPS, S5_STATE))


def _mamba_in_kernel(x_ref, cp_ref, wz_ref, wx_ref, wdt_ref, cw_ref, cb_ref, dtb_ref,
                     z_out, xs_out, b_out, c_out, dt_out, nc_out, buf, *, bb, tl):
    d = x_ref.shape[-1]
    m = bb * tl
    cd = M_CONV_DIM
    l = pl.program_id(1)

    @pl.when(l == 0)
    def _():
        buf[:, :SUBLANES - (M_CONV - 1), :] = jnp.zeros((bb, SUBLANES - (M_CONV - 1), cd), F32)
        buf[:, SUBLANES - (M_CONV - 1):SUBLANES, :] = cp_ref[...]

    @pl.when(l > 0)
    def _():
        buf[:, :SUBLANES, :] = buf[:, tl:, :]

    x = x_ref[...].reshape(m, d).astype(BF16)
    nbc = M_GROUPS * M_STATE
    for ci in range(cd // nbc):
        cs = slice(ci * nbc, (ci + 1) * nbc)
        xbc = _dot(x, wx_ref[:, cs])
        buf[:, SUBLANES:, cs] = xbc.reshape(bb, tl, nbc)
        conv = cb_ref[:, cs] + xbc * cw_ref[M_CONV - 1:M_CONV, cs]
        for j in range(1, M_CONV):
            sh = buf[:, SUBLANES - j:SUBLANES - j + tl, cs]
            conv = conv + sh.reshape(m, nbc) * cw_ref[M_CONV - 1 - j:M_CONV - j, cs]
        act = (conv * jax.nn.sigmoid(conv)).reshape(bb, tl, nbc)
        if ci * nbc < M_INNER:
            xs_out[:, :, cs] = act
            z_out[:, :, cs] = _dot(x, wz_ref[:, cs]).reshape(bb, tl, nbc)
        elif ci * nbc == M_INNER:
            b_out[...] = act
        else:
            c_out[...] = act
    nc_out[...] = buf[:, tl + SUBLANES - (M_CONV - 1):, :]
    dt_raw = _dot(x, wdt_ref[...])[:, :M_HEADS]
    dt_out[...] = _softplus(dt_raw + dtb_ref[...]).reshape(bb, tl, M_HEADS)


def mamba_in(xn, conv_prev, prm, *, bb, tl):
    bsz, L, d = xn.shape
    cd = M_CONV_DIM
    nbc = M_GROUPS * M_STATE
    seq = lambda w: pl.BlockSpec((bb, tl, w), lambda b, l: (b, l, 0))
    cps = pl.BlockSpec((bb, M_CONV - 1, cd), lambda b, l: (b, 0, 0))
    return pl.pallas_call(
        functools.partial(_mamba_in_kernel, bb=bb, tl=tl),
        grid=(bsz // bb, L // tl),
        in_specs=[seq(d), cps, _resident((d, M_INNER)), _resident((d, cd)),
                  _resident((d, LANES)),
                  _full((M_CONV, cd)), _full((1, cd)), _full((1, M_HEADS))],
        out_specs=[seq(M_INNER), seq(M_INNER), seq(nbc), seq(nbc), seq(M_HEADS), cps],
        out_shape=[jax.ShapeDtypeStruct((bsz, L, M_INNER), F32),
                   jax.ShapeDtypeStruct((bsz, L, M_INNER), F32),
                   jax.ShapeDtypeStruct((bsz, L, nbc), F32),
                   jax.ShapeDtypeStruct((bsz, L, nbc), F32),
                   jax.ShapeDtypeStruct((bsz, L, M_HEADS), F32),
                   jax.ShapeDtypeStruct((bsz, M_CONV - 1, cd), F32)],
        scratch_shapes=[pltpu.VMEM((bb, SUBLANES + tl, cd), F32)],
        compiler_params=_cparams(("parallel", "arbitrary")),
        name="mamba_in",
    )(xn, conv_prev, prm["wz"], prm["wx"], prm["wdt"], prm["conv_w"], prm["conv_b"],
      prm["dt_bias"])


def _ssd_kernel(xs_ref, b_ref, c_ref, z_ref, dt_ref, hv_ref, ex_ref, nw_ref, s0_ref,
                y_out, s_out, s_scr, *, bb, tl, c):
    p, n = M_HEADDIM, M_STATE
    hpg = M_HEADS // M_GROUPS
    gw = M_INNER // M_GROUPS
    nchunk = tl // c

    @pl.when(pl.program_id(1) == 0)
    def _():
        s_scr[...] = s0_ref[...]

    row = lax.broadcasted_iota(jnp.int32, (c, c), 0)
    col = lax.broadcasted_iota(jnp.int32, (c, c), 1)
    incl = row >= col
    tril = jnp.where(incl, 1.0, 0.0).astype(F32)
    triu = jnp.where(row <= col, 1.0, 0.0).astype(F32)
    eye = jnp.where(row == col, 1.0, 0.0).astype(F32)
    row_p = lax.broadcasted_iota(jnp.int32, (c, 2 * c), 0)
    lane_p = lax.broadcasted_iota(jnp.int32, (c, 2 * c), 1)
    lo_pair = lane_p < c
    incl_pair = row_p >= (lane_p & (c - 1))
    row_b = lax.broadcasted_iota(jnp.int32, (2 * c, 2 * p), 0)
    lane_b = lax.broadcasted_iota(jnp.int32, (2 * c, 2 * p), 1)
    bd_mask = (row_b >> (c.bit_length() - 1)) == (lane_b >> (p.bit_length() - 1))
    a_neg = hv_ref[0:1, :]
    d_exp = nw_ref[1:2, :]
    norm_w = nw_ref[0:1, :]
    expand = ex_ref[...]

    def chunk(ci, carry):
        rows = pl.ds(pl.multiple_of(ci * c, c), c)
        pre = []
        for s in range(bb):
            xs = xs_ref[s, rows, :]
            dt = dt_ref[s, rows, :]
            a = dt * a_neg
            cum = _dot(tril, a, HIGHEST)
            cum_end = cum[c - 1:c, :]
            pre.append(dict(
                xs=xs, cum=cum,
                cum_t=_dot_tn(a, triu, HIGHEST),
                dt_t=_dot_tn(dt, eye, HIGHEST),
                e_end=jnp.exp(cum_end),
                xend=xs * _dot(dt * jnp.exp(cum_end - cum), expand),
                ecum_x=_dot(jnp.exp(cum), expand)))
        yd = [[] for _ in range(bb)]
        yo = [[] for _ in range(bb)]
        for gi in range(M_GROUPS):
            for s in range(bb):
                q_ = pre[s]
                xs, cum, cum_t, dt_t = q_["xs"], q_["cum"], q_["cum_t"], q_["dt_t"]
                b_g = b_ref[s, rows, gi * n:(gi + 1) * n]
                c_g = c_ref[s, rows, gi * n:(gi + 1) * n]
                cb = _dot_nt(c_g, b_g)
                cb2 = jnp.concatenate([cb, cb], axis=1)
                for q in range(hpg // 2):
                    h0 = gi * hpg + 2 * q
                    col = jnp.where(lo_pair, cum[:, h0:h0 + 1], cum[:, h0 + 1:h0 + 2])
                    rowv = jnp.concatenate([cum_t[h0:h0 + 1, :], cum_t[h0 + 1:h0 + 2, :]], axis=1)
                    dtrow = jnp.concatenate([dt_t[h0:h0 + 1, :], dt_t[h0 + 1:h0 + 2, :]], axis=1)
                    lmat = jnp.where(incl_pair, jnp.exp(jnp.where(incl_pair, col - rowv, 0.0)), 0.0)
                    x_pair = xs[:, h0 * p:(h0 + 2) * p]
                    x_bd = jnp.where(bd_mask, jnp.concatenate([x_pair, x_pair], axis=0), 0.0)
                    yd[s].append(_dot(cb2 * lmat * dtrow, x_bd))
                s_g = s_scr[s, gi * hpg:(gi + 1) * hpg].reshape(hpg * p, n)
                yo[s].append(_dot_nt(c_g, s_g))
                upd = _dot_tn(q_["xend"][:, gi * gw:(gi + 1) * gw], b_g)
                for hh in range(hpg):
                    h = gi * hpg + hh
                    s_scr[s, h] = (s_g[hh * p:(hh + 1) * p] * q_["e_end"][:, h:h + 1]
                                   + upd[hh * p:(hh + 1) * p])
        for s in range(bb):
            q_ = pre[s]
            y = (jnp.concatenate(yd[s], axis=-1) + jnp.concatenate(yo[s], axis=-1) * q_["ecum_x"]
                 + d_exp * q_["xs"])
            zz = z_ref[s, rows, :]
            y = y * (zz * jax.nn.sigmoid(zz))
            outs = []
            for gi in range(M_GROUPS):
                yg = y[:, gi * gw:(gi + 1) * gw]
                outs.append(yg * lax.rsqrt(jnp.mean(yg * yg, -1, keepdims=True) + EPS))
            y_out[s, rows, :] = jnp.concatenate(outs, axis=-1) * norm_w
        return carry

    lax.fori_loop(0, nchunk, chunk, 0)

    @pl.when(pl.program_id(1) == pl.num_programs(1) - 1)
    def _():
        s_out[...] = s_scr[...]


def ssd_scan(xs, bm, cm, z, dt, prm, s0, *, bb, tl, c):
    bsz, L, _ = xs.shape
    nbc = M_GROUPS * M_STATE
    seq = lambda w: pl.BlockSpec((bb, tl, w), lambda b, l: (b, l, 0))
    st = pl.BlockSpec((bb, M_HEADS, M_HEADDIM, M_STATE), lambda b, l: (b, 0, 0, 0))
    return pl.pallas_call(
        functools.partial(_ssd_kernel, bb=bb, tl=tl, c=c),
        grid=(bsz // bb, L // tl),
        in_specs=[seq(M_INNER), seq(nbc), seq(nbc), seq(M_INNER), seq(M_HEADS),
                  _full((SUBLANES, M_HEADS)), _full((M_HEADS, M_INNER)),
                  _full((SUBLANES, M_INNER)), st],
        out_specs=[seq(M_INNER), st],
        out_shape=[jax.ShapeDtypeStruct((bsz, L, M_INNER), F32),
                   jax.ShapeDtypeStruct(s0.shape, F32)],
        scratch_shapes=[pltpu.VMEM((bb, M_HEADS, M_HEADDIM, M_STATE), F32)],
        compiler_params=_cparams(("parallel", "arbitrary")),
        name="ssd_scan",
    )(xs, bm, cm, z, dt, prm["hv"], prm["expand"], prm["nw"], s0)


def _prep_mamba(in_proj, conv_w, conv_b, dt_bias, a_log, d_skip, norm_w, out_proj):
    d = D_MODEL
    wdt = jnp.zeros((d, LANES), F32).at[:, :M_HEADS].set(in_proj[:, M_INNER + M_CONV_DIM:])
    expand = jnp.repeat(jnp.eye(M_HEADS, dtype=F32), M_HEADDIM, axis=1)
    zrow_h = jnp.zeros((M_HEADS,), F32)
    zrow_i = jnp.zeros((M_INNER,), F32)
    return dict(
        wz=in_proj[:, :M_INNER].astype(BF16),
        wx=in_proj[:, M_INNER:M_INNER + M_CONV_DIM].astype(BF16),
        wdt=wdt.astype(BF16),
        conv_w=conv_w, conv_b=conv_b.reshape(1, M_CONV_DIM),
        dt_bias=dt_bias.reshape(1, M_HEADS),
        hv=jnp.stack([-jnp.exp(a_log.astype(F32))] + [zrow_h] * 7),
        expand=expand,
        nw=jnp.stack([norm_w, jnp.repeat(d_skip, M_HEADDIM)] + [zrow_i] * 6),
        out_proj=out_proj.astype(BF16))


def mamba_layer(xn, conv_prev, ssm_prev, prm, cfg):
    bsz, L, d = xn.shape
    z, xs, bm, cm, dt, new_conv = mamba_in(xn, conv_prev, prm, bb=cfg["mamba_bb"],
                                           tl=cfg["mamba_tl"])
    y, ssm_new = ssd_scan(xs, bm, cm, z, dt, prm, ssm_prev,
                          bb=cfg["ssd_bb"], tl=cfg["ssd_tl"], c=cfg["ssd_c"])
    return (y, prm["out_proj"]), new_conv, ssm_new


def _config(bsz, L):
    if L >= 512:
        return dict(tm=512, ffn_fb=bsz, ffn_ft=512 // bsz, time_major=bsz == SUBLANES,
                    proj_bb=1, proj_tl=256,
                    scan_bb=min(bsz, 8), wkv_tl=128, wkv_c=64, wkv_heads=8,
                    mamba_bb=1, mamba_tl=256,
                    ssd_bb=2, ssd_tl=256, ssd_c=M_CHUNK, s5_tr=1024,
                    s5_nco=2 if bsz <= SUBLANES else 1)
    rows = bsz * L
    tm = min(512, rows)
    return dict(tm=tm, ffn_fb=tm // L, ffn_ft=L, time_major=False,
                proj_bb=min(bsz, 256 // L), proj_tl=L,
                scan_bb=min(bsz, 16), wkv_tl=L, wkv_c=L, wkv_heads=2,
                mamba_bb=min(bsz, 256 // L), mamba_tl=L,
                ssd_bb=min(bsz, 4), ssd_tl=L, ssd_c=min(M_CHUNK, L), s5_tr=rows,
                s5_nco=2 if bsz <= SUBLANES else 1)


def _trunk(x, p, states, layer_prms, shared, cfg):
    bsz, L, d = x.shape
    xn = h = x
    v_first = None
    new_states = []
    for i in range(DEPTH):
        st_a, st_b = states[2 * i], states[2 * i + 1]
        prm = layer_prms[i]
        kind = i % 3
        if kind == 0:
            (mix, wo), n_a, n_b, v_first = rwkv_layer(
                xn, st_a, st_b, v_first, prm, cfg, shared["norm_mix"][0] if i == 0 else None)
        elif kind == 1:
            (mix, wo), n_a, n_b = s5_layer(xn, st_a, st_b, prm, cfg)
        else:
            (mix, wo), n_a, n_b = mamba_layer(xn, st_a, st_b, prm, cfg)
        new_states += [n_a, n_b]
        nxt = shared["norm_mix"][i + 1] if i + 1 < DEPTH else shared["final_norm"]
        mix_tm = cfg["time_major"] and kind == 1
        xn_tm = cfg["time_major"] and (i + 1) % 3 == 1 and i + 1 < DEPTH
        h, xn = ffn_ple(h, mix, wo, p, i, shared, nxt, fb=cfg["ffn_fb"], ft=cfg["ffn_ft"],
                        mix_tm=mix_tm, xn_tm=xn_tm)
    return xn, new_states


def kernel(x_prompt, x_sample, p_prompt, p_sample, state_l0_shift, state_l0_wkv, state_l1_s5_re, state_l1_s5_im, state_l2_conv, state_l2_ssm, state_l3_shift, state_l3_wkv, l0_mu, l0_w0, l0_w1, l0_w2, l0_a0, l0_a1, l0_a2, l0_g1, l0_g2, l0_k_k, l0_k_a, l0_r_k, l0_w_rkv, l0_w_o, l0_lnx_w, l0_lnx_b, l1_a_re, l1_a_im, l1_log_dt, l1_b_re, l1_b_im, l1_c_re, l1_c_im, l1_d, l1_glu_v, l1_glu_g, l2_in_proj, l2_conv_w, l2_conv_b, l2_dt_bias, l2_a_log, l2_d, l2_norm_w, l2_out_proj, l3_mu, l3_w0, l3_w1, l3_w2, l3_a0, l3_a1, l3_a2, l3_g1, l3_g2, l3_k_k, l3_k_a, l3_r_k, l3_w_rkv, l3_w_o, l3_lnx_w, l3_lnx_b, l3_v0, l3_v1, l3_v2, norm_mix, norm_ffn, norm_ple, ffn_w1, ffn_w3, ffn_w2, ple_proj, ple_gate, final_norm):
    layer_prms = (
        _prep_rwkv(l0_mu, l0_w0, l0_w1, l0_w2, l0_a0, l0_a1, l0_a2, l0_g1, l0_g2,
                   l0_k_k, l0_k_a, l0_r_k, l0_w_rkv, l0_w_o, l0_lnx_w, l0_lnx_b),
        _prep_s5(l1_a_re, l1_a_im, l1_log_dt, l1_b_re, l1_b_im, l1_c_re, l1_c_im,
                 l1_d, l1_glu_v, l1_glu_g),
        _prep_mamba(l2_in_proj, l2_conv_w, l2_conv_b, l2_dt_bias, l2_a_log, l2_d,
                    l2_norm_w, l2_out_proj),
        _prep_rwkv(l3_mu, l3_w0, l3_w1, l3_w2, l3_a0, l3_a1, l3_a2, l3_g1, l3_g2,
                   l3_k_k, l3_k_a, l3_r_k, l3_w_rkv, l3_w_o, l3_lnx_w, l3_lnx_b,
                   (l3_v0, l3_v1, l3_v2)),
    )
    shared = dict(norm_mix=norm_mix, norm_ffn=norm_ffn.reshape(DEPTH, 1, D_MODEL),
                  norm_ple=norm_ple.reshape(DEPTH, 1, D_MODEL), final_norm=final_norm,
                  ffn_w1=ffn_w1.astype(BF16), ffn_w3=ffn_w3.astype(BF16),
                  ffn_w2=ffn_w2.astype(BF16), ple_gate=ple_gate.astype(BF16),
                  ple_proj=ple_proj.astype(BF16))
    sample_states = [state_l0_shift, state_l0_wkv, state_l1_s5_re, state_l1_s5_im,
                     state_l2_conv, state_l2_ssm, state_l3_shift, state_l3_wkv]
    bp = x_prompt.shape[0]
    prompt_states = [jnp.zeros((bp,) + s.shape[1:], F32) for s in sample_states]
    y_prompt, new_p = _trunk(x_prompt, p_prompt, prompt_states, layer_prms, shared,
                             _config(*x_prompt.shape[:2]))
    y_sample, new_s = _trunk(x_sample, p_sample, sample_states, layer_prms, shared,
                             _config(*x_sample.shape[:2]))
    return (y_prompt, y_sample, *new_p, *new_s)
```
